```python
import math
import jax, jax.numpy as jnp
from jax import lax
import numpy as np

D_MODEL = 1024
BATCH = 2
SEQ = 8192
DEPTH = 1
DEC_BATCH = 16
DEC_SEQ = 32
PAST_LEN = 2048

CHUNK = 64
Q_BLOCK = 128
H_A = 8
DH_A = 64
DV_A = 2 * DH_A
W_A = H_A * DV_A
H_M = 4
DH_M = D_MODEL // H_M
W_M = H_M * DH_M
D_FF = 2816
CONV_W = 3
EPS = 1e-6
COL_SIZES = (W_A, W_A, W_A, W_M, W_M, W_M, H_M, H_M, W_M, D_MODEL, D_MODEL)
D_IN = 3 * W_A + 4 * W_M + 2 * H_M + 2 * D_MODEL

kernel_name = 'diffattn_mlstm_convffn_stream_step'


def _split_points():
    return [int(v) for v in np.cumsum(COL_SIZES)[:-1]]


def _rmsnorm(x, g):
    xf = x.astype(jnp.float32)
    y = xf * lax.rsqrt(jnp.mean(xf * xf, axis=-1, keepdims=True) + EPS)
    return (y * g.astype(jnp.float32)).astype(x.dtype)


def _diff_softmax_mix(q, k, v, lam, mask):
    s = jnp.einsum('bqhmd,bkhmd->bmhqk', q, k).astype(jnp.float32) * (DH_A ** -0.5)
    if mask is not None:
        s = jnp.where(mask, s, -jnp.inf)
    p = jax.nn.softmax(s, axis=-1)
    a = p[:, 0] - lam * p[:, 1]
    return jnp.einsum('bhqk,bkhd->bqhd', a.astype(v.dtype), v)


def _diff_attn_prompt(q, k, v, lam):
    B, S = q.shape[:2]
    nb = S // Q_BLOCK
    qb = jnp.moveaxis(q.reshape(B, nb, Q_BLOCK, H_A, 2, DH_A), 1, 0)
    k_chunk = jnp.arange(S) // CHUNK

    def one_block(args):
        q_blk, i = args
        q_chunk = (i * Q_BLOCK + jnp.arange(Q_BLOCK)) // CHUNK
        mask = k_chunk[None, :] <= q_chunk[:, None]
        return _diff_softmax_mix(q_blk, k, v, lam, mask)

    o = lax.map(one_block, (qb, jnp.arange(nb)))
    return jnp.moveaxis(o, 0, 1).reshape(B, S, H_A, DV_A)


def _mlstm_chunk(state, xs):
    C, n, m = state
    q, k, v, ig, lf = xs
    T = q.shape[1]
    b = jnp.cumsum(lf, axis=1)
    d = b[:, :, None, :] - b[:, None, :, :] + ig[:, None, :, :]
    causal = jnp.tril(jnp.ones((T, T), dtype=bool))
    d = jnp.where(causal[None, :, :, None], d, -jnp.inf)
    inter = b + m[:, None, :]
    m_t = jnp.maximum(inter, jnp.max(d, axis=2))
    w_intra = jnp.exp(d - m_t[:, :, None, :])
    w_inter = jnp.exp(inter - m_t)
    qs = q * (DH_M ** -0.5)
    s = jnp.einsum('bthd,bshd->btsh', qs, k) * w_intra
    num = jnp.einsum('btsh,bshd->bthd', s, v) + w_inter[..., None] * jnp.einsum('bhvd,bthd->bthv', C, qs)
    den = jnp.sum(s, axis=2) + w_inter * jnp.einsum('bhd,bthd->bth', n, qs)
    h = num / jnp.maximum(jnp.abs(den), jnp.exp(-m_t))[..., None]
    g_last = b[:, -1, :]
    logw = g_last[:, None, :] - b + ig
    m_new = jnp.maximum(g_last + m, jnp.max(logw, axis=1))
    ws = jnp.exp(logw - m_new[:, None, :])
    wc = jnp.exp(g_last + m - m_new)
    C_new = wc[..., None, None] * C + jnp.einsum('bsh,bshv,bshd->bhvd', ws, v, k)
    n_new = wc[..., None] * n + jnp.einsum('bsh,bshd->bhd', ws, k)
    return (C_new, n_new, m_new), h


def _mlstm(q, k, v, ig, lf, state):
    B, L = q.shape[:2]
    T = min(CHUNK, L)
    nc = L // T

    def to_chunks(a):
        a = a.astype(jnp.float32)
        return jnp.moveaxis(a.reshape((B, nc, T) + a.shape[2:]), 1, 0)

    xs = tuple(to_chunks(a) for a in (q, k, v, ig, lf))
    state = tuple(s.astype(jnp.float32) for s in state)
    state, h = lax.scan(_mlstm_chunk, state, xs)
    h = jnp.moveaxis(h, 0, 1).reshape(B, L, H_M, DH_M)
    return h.astype(q.dtype), state


def _layer(x, cache_k, cache_v, C0, n0, m0, conv0, g_pre_mix, w_in, b_gates, lam_q1, lam_k1,
           lam_q2, lam_k2, g_attn_head, g_mlstm_head, w_out, g_post_mix, g_pre_ffn, w_up,
           conv_w, conv_b, w_down, g_post_ffn, lam_init):
    B, L, _ = x.shape
    h = _rmsnorm(x, g_pre_mix)
    z = h @ w_in
    qa, ka, va, qm, km, vm, ig, fg, om, ga, gm = jnp.split(z, _split_points(), axis=-1)

    qa = qa.reshape(B, L, H_A, 2, DH_A)
    ka = ka.reshape(B, L, H_A, 2, DH_A)
    va = va.reshape(B, L, H_A, DV_A)
    f32 = jnp.float32
    lam = (jnp.exp(jnp.sum(lam_q1.astype(f32) * lam_k1.astype(f32)))
           - jnp.exp(jnp.sum(lam_q2.astype(f32) * lam_k2.astype(f32))) + lam_init)
    if cache_k is None:
        attn = _diff_attn_prompt(qa, ka, va, lam)
    else:
        P = cache_k.shape[1]
        k_all = jnp.concatenate([cache_k.reshape(B, P, H_A, 2, DH_A).astype(ka.dtype), ka], axis=1)
        v_all = jnp.concatenate([cache_v.astype(va.dtype), va], axis=1)
        attn = _diff_softmax_mix(qa, k_all, v_all, lam, None)
    attn = (_rmsnorm(attn, g_attn_head) * (1.0 - lam_init)).reshape(B, L, W_A)

    ig = ig + b_gates[:H_M]
    lf = jax.nn.log_sigmoid((fg + b_gates[H_M:]).astype(f32))
    hm, (C1, n1, m1) = _mlstm(qm.reshape(B, L, H_M, DH_M), km.reshape(B, L, H_M, DH_M),
                              vm.reshape(B, L, H_M, DH_M), ig, lf, (C0, n0, m0))
    hm = jax.nn.sigmoid(om) * _rmsnorm(hm, g_mlstm_head).reshape(B, L, W_M)

    merged = jax.nn.sigmoid(ga) * attn + jax.nn.sigmoid(gm) * hm
    x = x + _rmsnorm(merged @ w_out, g_post_mix)

    h2 = _rmsnorm(x, g_pre_ffn)
    u, g = jnp.split(h2 @ w_up, 2, axis=-1)
    gp = jnp.concatenate([conv0.astype(g.dtype), g], axis=1)
    gc = conv_b + sum(conv_w[j] * gp[:, j:j + L] for j in range(CONV_W))
    ff = (jax.nn.gelu(gc) * u) @ w_down
    x = x + _rmsnorm(ff, g_post_ffn)

    new_k = ka.reshape(B, L, H_A, 2 * DH_A)
    return (x, new_k, va, C1.astype(x.dtype), n1.astype(x.dtype), m1.astype(x.dtype),
            gp[:, L:])


def setup_inputs(seed: int = 0) -> dict:
    key = jax.random.key(seed)
    ks = jax.random.split(key, 32)
    f32 = jnp.float32

    def nrm(k, shape, s):
        return jax.random.normal(k, shape, f32) * s

    def gain(k, shape):
        return 1.0 + nrm(k, shape, 0.01)

    b_i = nrm(ks[20], (DEPTH, H_M), 0.1)
    b_f = jnp.linspace(3.0, 6.0, H_M, dtype=f32)[None, :] + nrm(ks[21], (DEPTH, H_M), 0.1)
    return {
        'x_prompt': nrm(ks[0], (BATCH, SEQ, D_MODEL), 1.0),
        'x_sample': nrm(ks[1], (DEC_BATCH, DEC_SEQ, D_MODEL), 1.0),
        'cache_k': nrm(ks[2], (DEPTH, DEC_BATCH, PAST_LEN, H_A, 2 * DH_A), 1.0),
        'cache_v': nrm(ks[3], (DEPTH, DEC_BATCH, PAST_LEN, H_A, DV_A), 1.0),
        'state_C': nrm(ks[4], (DEPTH, DEC_BATCH, H_M, DH_M, DH_M), 1.0),
        'state_n': nrm(ks[5], (DEPTH, DEC_BATCH, H_M, DH_M), 2.0),
        'state_m': 3.0 + nrm(ks[6], (DEPTH, DEC_BATCH, H_M), 0.5),
        'state_conv': nrm(ks[7], (DEPTH, DEC_BATCH, CONV_W - 1, D_FF), 1.0),
        'g_pre_mix': gain(ks[8], (DEPTH, D_MODEL)),
        'w_in': nrm(ks[9], (DEPTH, D_MODEL, D_IN), D_MODEL ** -0.5),
        'b_gates': jnp.concatenate([b_i, b_f], axis=-1),
        'lam_q1': nrm(ks[10], (DEPTH, DH_A), 0.1),
        'lam_k1': nrm(ks[11], (DEPTH, DH_A), 0.1),
        'lam_q2': nrm(ks[12], (DEPTH, DH_A), 0.1),
        'lam_k2': nrm(ks[13], (DEPTH, DH_A), 0.1),
        'g_attn_head': gain(ks[14], (DEPTH, DV_A)),
        'g_mlstm_head': gain(ks[15], (DEPTH, H_M, DH_M)),
        'w_out': nrm(ks[16], (DEPTH, D_MODEL, D_MODEL), D_MODEL ** -0.5),
        'g_post_mix': gain(ks[17], (DEPTH, D_MODEL)),
        'g_pre_ffn': gain(ks[18], (DEPTH, D_MODEL)),
        'w_up': nrm(ks[19], (DEPTH, D_MODEL, 2 * D_FF), D_MODEL ** -0.5),
        'conv_w': nrm(ks[22], (DEPTH, CONV_W, D_FF), CONV_W ** -0.5),
        'conv_b': nrm(ks[23], (DEPTH, D_FF), 0.01),
        'w_down': nrm(ks[24], (DEPTH, D_FF, D_MODEL), D_FF ** -0.5),
        'g_post_ffn': gain(ks[25], (DEPTH, D_MODEL)),
    }


def reference(x_prompt, x_sample, cache_k, cache_v, state_C, state_n, state_m, state_conv,
              g_pre_mix, w_in, b_gates, lam_q1, lam_k1, lam_q2, lam_k2, g_attn_head,
              g_mlstm_head, w_out, g_post_mix, g_pre_ffn, w_up, conv_w, conv_b, w_down,
              g_post_ffn):
    yp, ys = x_prompt, x_sample
    Bp = x_prompt.shape[0]
    dt = x_prompt.dtype
    outs_p, outs_s = [], []
    for l in range(DEPTH):
        lam_init = 0.8 - 0.6 * math.exp(-0.3 * l)
        w = (g_pre_mix[l], w_in[l], b_gates[l], lam_q1[l], lam_k1[l], lam_q2[l], lam_k2[l],
             g_attn_head[l], g_mlstm_head[l], w_out[l], g_post_mix[l], g_pre_ffn[l], w_up[l],
             conv_w[l], conv_b[l], w_down[l], g_post_ffn[l])
        C0 = jnp.zeros((Bp, H_M, DH_M, DH_M), dt)
        n0 = jnp.zeros((Bp, H_M, DH_M), dt)
        m0 = jnp.zeros((Bp, H_M), dt)
        conv0 = jnp.zeros((Bp, CONV_W - 1, D_FF), dt)
        yp, *sp = _layer(yp, None, None, C0, n0, m0, conv0, *w, lam_init)
        ys, *ss = _layer(ys, cache_k[l], cache_v[l], state_C[l], state_n[l], state_m[l],
                         state_conv[l], *w, lam_init)
        outs_p.append(sp)
        outs_s.append(ss)
    k_p, v_p, C_p, n_p, m_p, conv_p = [jnp.stack([o[i] for o in outs_p]) for i in range(6)]
    k_s, v_s, C_s, n_s, m_s, conv_s = [jnp.stack([o[i] for o in outs_s]) for i in range(6)]
    return (yp, ys, k_p, v_p, C_p, n_p, m_p, conv_p, k_s, v_s, C_s, n_s, m_s, conv_s)
```

```python
import functools
import math

import jax
import jax.numpy as jnp
import numpy as np
from jax import lax
from jax.experimental import pallas as pl
from jax.experimental.pallas import tpu as pltpu

F32 = jnp.float32
BF16 = jnp.bfloat16

D_MODEL = 1024
CHUNK = 64
H_A = 8
DH_A = 64
DV_A = 2 * DH_A
H_M = 4
DH_M = D_MODEL // H_M
D_FF = 2816
CONV_W = 3
EPS = 1e-6
N_SEG = 9
GATE_LANES = 128
W_CAT = N_SEG * D_MODEL + GATE_LANES

V7X_VMEM_LIMIT = 56 * 1024 * 1024

PROJ_ROWS = 256
ATTN_TILE = 512
MLSTM_CHUNK = 256
FFN_ROWS = 256

NT_DIMS = (((1,), (1,)), ((), ()))
TN_DIMS = (((0,), (0,)), ((), ()))


def _rms(x, g):
    return x * lax.rsqrt(jnp.mean(x * x, axis=-1, keepdims=True) + EPS) * g


def _resident(shape):
    nd = len(shape)
    return pl.BlockSpec(shape, lambda *_: (0,) * nd, pipeline_mode=pl.Buffered(1))


def _inproj_kernel(x_ref, g_ref, w_ref, qa_ref, kf_ref, kb_ref, vf_ref, vb_ref,
                   qm_ref, km_ref, vm_ref, om_ref, ga_ref, gm_ref, gz_ref, *, nbt, lt):
    hb = _rms(x_ref[...], g_ref[...]).astype(BF16)

    def seg(j, width=D_MODEL):
        return jnp.dot(hb, w_ref[:, j * D_MODEL:j * D_MODEL + width],
                       preferred_element_type=F32)

    def put_heads(ref, val):
        vb = val.astype(BF16)
        for h in range(H_A):
            ref[:, h] = vb[:, h * DV_A:(h + 1) * DV_A].reshape(nbt, lt, DV_A)

    put_heads(qa_ref, seg(0))
    k = seg(1)
    kf_ref[...] = k
    put_heads(kb_ref, k)
    v = seg(2)
    vf_ref[...] = v
    put_heads(vb_ref, v)
    qm_ref[...] = seg(3).astype(BF16)
    km_ref[...] = seg(4).astype(BF16)
    vm_ref[...] = seg(5).astype(BF16)
    om_ref[...] = seg(6)
    ga_ref[...] = seg(7)
    gm_ref[...] = seg(8)
    gz_ref[...] = seg(9, GATE_LANES)


def _inproj(x2d, g, w_cat, nb, seq):
    n = x2d.shape[0]
    tm = PROJ_ROWS
    if seq >= tm:
        nbt, lt = 1, tm
        per_b = seq // tm
        head_map = lambda i: (i // per_b, 0, i % per_b, 0)
    else:
        nbt, lt = tm // seq, seq
        head_map = lambda i: (i, 0, 0, 0)
    row = pl.BlockSpec((tm, D_MODEL), lambda i: (i, 0))
    head = pl.BlockSpec((nbt, H_A, lt, DV_A), head_map)
    full32 = jax.ShapeDtypeStruct((n, D_MODEL), F32)
    full16 = jax.ShapeDtypeStruct((n, D_MODEL), BF16)
    hm16 = jax.ShapeDtypeStruct((nb, H_A, seq, DV_A), BF16)
    return pl.pallas_call(
        functools.partial(_inproj_kernel, nbt=nbt, lt=lt),
        grid=(n // tm,),
        in_specs=[row, _resident((1, D_MODEL)), _resident((D_MODEL, W_CAT))],
        out_specs=[head, row, head, row, head, row, row, row, row, row, row,
                   pl.BlockSpec((tm, GATE_LANES), lambda i: (i, 0))],
        out_shape=[hm16, full32, hm16, full32, hm16, full16, full16, full16,
                   full32, full32, full32, jax.ShapeDtypeStruct((n, GATE_LANES), F32)],
        compiler_params=pltpu.CompilerParams(
            dimension_semantics=("arbitrary",), vmem_limit_bytes=V7X_VMEM_LIMIT),
        name="inproj",
    )(x2d, g, w_cat)


def _lam(lq1, lk1, lq2, lk2, lam_init):
    return (jnp.exp(jnp.sum(lq1 * lk1, axis=-1, keepdims=True))
            - jnp.exp(jnp.sum(lq2 * lk2, axis=-1, keepdims=True)) + lam_init)


def _split_maps(q):
    lane = lax.broadcasted_iota(jnp.int32, q.shape, 1)
    qs = q * (DH_A ** -0.5)
    zero = jnp.zeros_like(qs)
    return jnp.where(lane < DH_A, qs, zero), jnp.where(lane >= DH_A, qs, zero)


def _attn_prompt_kernel(qi_tab, kj_tab, q_ref, k_ref, v_ref, lq1_ref, lk1_ref, lq2_ref, lk2_ref,
                        gh_ref, o_ref, qz_s, m_s, l_s, acc_s, *, lam_init):
    p = pl.program_id(1)
    qi = qi_tab[p]
    kj = kj_tab[p]
    t = ATTN_TILE

    @pl.when(kj == 0)
    def _init():
        for h in range(H_A):
            q1, q2 = _split_maps(q_ref[h])
            qz_s[0, h] = q1
            qz_s[1, h] = q2
        m_s[...] = jnp.full(m_s.shape, -jnp.inf, F32)
        l_s[...] = jnp.zeros(l_s.shape, F32)
        acc_s[...] = jnp.zeros(acc_s.shape, F32)

    def sweep(masked):
        if masked:
            r = lax.broadcasted_iota(jnp.int32, (t, t), 0) // CHUNK
            c = lax.broadcasted_iota(jnp.int32, (t, t), 1) // CHUNK
            visible = c <= r

        def head(h, carry):
            k = k_ref[h]
            v = v_ref[h]
            for mp in range(2):
                s = lax.dot_general(qz_s[mp, h], k, NT_DIMS, preferred_element_type=F32)
                if masked:
                    s = jnp.where(visible, s, -jnp.inf)
                m_old = m_s[mp, h]
                m_new = jnp.maximum(m_old, jnp.max(s, axis=-1, keepdims=True))
                alpha = jnp.exp(m_old - m_new)
                pr = jnp.exp(s - m_new)
                l_s[mp, h] = alpha * l_s[mp, h] + jnp.sum(pr, axis=-1, keepdims=True)
                acc_s[mp, h] = alpha * acc_s[mp, h] + jnp.dot(
                    pr.astype(BF16), v, preferred_element_type=F32)
                m_s[mp, h] = m_new
            return carry

        lax.fori_loop(0, H_A, head, 0)

    @pl.when(kj < qi)
    def _full():
        sweep(False)

    @pl.when(kj == qi)
    def _diag():
        sweep(True)
        lam = _lam(lq1_ref[...], lk1_ref[...], lq2_ref[...], lk2_ref[...], lam_init)
        for h in range(H_A):
            o = acc_s[0, h] / l_s[0, h] - lam * (acc_s[1, h] / l_s[1, h])
            o_ref[:, h * DV_A:(h + 1) * DV_A] = _rms(o, gh_ref[...]) * (1.0 - lam_init)


def _attn_prompt(q, k, v, lq1, lk1, lq2, lk2, gh, lam_init):
    b, _, s, _ = q.shape
    t = ATTN_TILE
    nq = s // t
    pairs = [(i, j) for i in range(nq) for j in range(i + 1)]
    qi_tab = jnp.asarray(np.array([a for a, _ in pairs], np.int32))
    kj_tab = jnp.asarray(np.array([c for _, c in pairs], np.int32))
    qspec = pl.BlockSpec((None, H_A, t, DV_A), lambda bb, p, qi, kj: (bb, 0, qi[p], 0))
    kspec = pl.BlockSpec((None, H_A, t, DV_A), lambda bb, p, qi, kj: (bb, 0, kj[p], 0))
    small = lambda shape: pl.BlockSpec(shape, lambda bb, p, qi, kj: (0, 0))
    grid_spec = pltpu.PrefetchScalarGridSpec(
        num_scalar_prefetch=2,
        grid=(b, len(pairs)),
        in_specs=[qspec, kspec, kspec, small((1, DH_A)), small((1, DH_A)), small((1, DH_A)),
                  small((1, DH_A)), small((1, DV_A))],
        out_specs=pl.BlockSpec((None, t, H_A * DV_A), lambda bb, p, qi, kj: (bb, qi[p], 0)),
        scratch_shapes=[pltpu.VMEM((2, H_A, t, DV_A), BF16),
                        pltpu.VMEM((2, H_A, t, 1), F32),
                        pltpu.VMEM((2, H_A, t, 1), F32),
                        pltpu.VMEM((2, H_A, t, DV_A), F32)],
    )
    return pl.pallas_call(
        functools.partial(_attn_prompt_kernel, lam_init=lam_init),
        grid_spec=grid_spec,
        out_shape=jax.ShapeDtypeStruct((b, s, H_A * DV_A), F32),
        compiler_params=pltpu.CompilerParams(
            dimension_semantics=("arbitrary", "arbitrary"), vmem_limit_bytes=V7X_VMEM_LIMIT),
        name="attn_prompt",
    )(qi_tab, kj_tab, q, k, v, lq1, lk1, lq2, lk2, gh)


def _attn_sample_kernel(q_ref, kn_ref, vn_ref, ck_ref, cv_ref, lq1_ref, lk1_ref, lq2_ref, lk2_ref,
                        gh_ref, o_ref, *, lam_init):
    lam = _lam(lq1_ref[...], lk1_ref[...], lq2_ref[...], lk2_ref[...], lam_init)
    nq = q_ref.shape[1]
    for h in range(H_A):
        sl = slice(h * DV_A, (h + 1) * DV_A)
        q1, q2 = _split_maps(q_ref[h])
        qz = jnp.concatenate([q1, q2], axis=0)
        kc = ck_ref[:, sl].astype(BF16)
        sc = lax.dot_general(qz, kc, NT_DIMS, preferred_element_type=F32)
        sn = lax.dot_general(qz, kn_ref[h], NT_DIMS, preferred_element_type=F32)
        m = jnp.maximum(jnp.max(sc, axis=-1, keepdims=True), jnp.max(sn, axis=-1, keepdims=True))
        pc = jnp.exp(sc - m)
        pn = jnp.exp(sn - m)
        inv = 1.0 / (jnp.sum(pc, axis=-1, keepdims=True) + jnp.sum(pn, axis=-1, keepdims=True))
        pc = pc * inv
        pn = pn * inv
        ac = pc[:nq] - lam * pc[nq:]
        an = pn[:nq] - lam * pn[nq:]
        o = (jnp.dot(ac.astype(BF16), cv_ref[:, sl].astype(BF16), preferred_element_type=F32)
             + jnp.dot(an.astype(BF16), vn_ref[h], preferred_element_type=F32))
        o_ref[:, sl] = _rms(o, gh_ref[...]) * (1.0 - lam_init)


def _attn_sample(q, kn, vn, cache_k, cache_v, lq1, lk1, lq2, lk2, gh, lam_init):
    b, _, l, _ = q.shape
    p = cache_k.shape[1]
    hspec = pl.BlockSpec((None, H_A, l, DV_A), lambda i: (i, 0, 0, 0))
    cspec = pl.BlockSpec((None, p, H_A * DV_A), lambda i: (i, 0, 0))
    small = lambda shape: pl.BlockSpec(shape, lambda i: (0, 0))
    return pl.pallas_call(
        functools.partial(_attn_sample_kernel, lam_init=lam_init),
        grid=(b,),
        in_specs=[hspec, hspec, hspec, cspec, cspec, small((1, DH_A)), small((1, DH_A)),
                  small((1, DH_A)), small((1, DH_A)), small((1, DV_A))],
        out_specs=pl.BlockSpec((None, l, H_A * DV_A), lambda i: (i, 0, 0)),
        out_shape=jax.ShapeDtypeStruct((b, l, H_A * DV_A), F32),
        compiler_params=pltpu.CompilerParams(
            dimension_semantics=("arbitrary",), vmem_limit_bytes=V7X_VMEM_LIMIT),
        name="attn_sample",
    )(q, kn, vn, cache_k, cache_v, lq1, lk1, lq2, lk2, gh)


def _split3(x):
    hi = x.astype(BF16)
    r1 = x - hi.astype(F32)
    mid = r1.astype(BF16)
    lo = (r1 - mid.astype(F32)).astype(BF16)
    return hi, mid, lo


def _mlstm_kernel(q_ref, k_ref, v_ref, gz_ref, bias_ref, gh_ref, c0_ref, n0_ref, m0_ref,
                  h_ref, c_ref, n_ref, m_ref, c_s, n_s, m_s, *, t):
    c = pl.program_id(1)

    @pl.when(c == 0)
    def _load_state():
        c_s[...] = c0_ref[...]
        n_s[...] = n0_ref[...]
        m_s[...] = m0_ref[...]

    gz = gz_ref[...] + bias_ref[...]
    lane = lax.broadcasted_iota(jnp.int32, gz.shape, 1)
    lf = jnp.minimum(gz, 0.0) - jnp.log1p(jnp.exp(-jnp.abs(gz)))
    lf = jnp.where((lane >= H_M) & (lane < 2 * H_M), lf, 0.0)
    row = lax.broadcasted_iota(jnp.int32, (t, t), 0)
    col = lax.broadcasted_iota(jnp.int32, (t, t), 1)
    causal = col <= row
    tril = jnp.where(causal, 1.0, 0.0).astype(BF16)
    bcum = sum(jnp.dot(tril, piece, preferred_element_type=F32) for piece in _split3(lf))
    gc = jnp.where(lane < H_M, gz, bcum)
    sel = jnp.where(lax.broadcasted_iota(jnp.int32, (8, GATE_LANES), 0)
                    == lax.broadcasted_iota(jnp.int32, (8, GATE_LANES), 1), 1.0, 0.0).astype(BF16)
    gr = sum(lax.dot_general(sel, piece, NT_DIMS, preferred_element_type=F32)
             for piece in _split3(gc))

    for h in range(H_M):
        sl = slice(h * DH_M, (h + 1) * DH_M)
        ig_c, b_c = gc[:, h:h + 1], gc[:, H_M + h:H_M + h + 1]
        ig_r, b_r = gr[h:h + 1, :], gr[H_M + h:H_M + h + 1, :]
        m_prev = m_s[:, h:h + 1]
        c_prev = c_s[h]
        n_prev = n_s[h:h + 1, :]
        q = q_ref[:, sl] * (DH_M ** -0.5)
        k = k_ref[:, sl]
        v = v_ref[:, sl]

        d = jnp.where(causal, b_c - b_r + ig_r, -jnp.inf)
        inter = b_c + m_prev
        m_t = jnp.maximum(inter, jnp.max(d, axis=-1, keepdims=True))
        w_intra = jnp.exp(d - m_t)
        w_inter = jnp.exp(inter - m_t)
        s = lax.dot_general(q, k, NT_DIMS, preferred_element_type=F32) * w_intra
        num = (jnp.dot(s.astype(BF16), v, preferred_element_type=F32)
               + w_inter * lax.dot_general(q, c_prev.astype(BF16), NT_DIMS,
                                           preferred_element_type=F32))
        den = (jnp.sum(s, axis=-1, keepdims=True)
               + w_inter * jnp.sum(q.astype(F32) * n_prev, axis=-1, keepdims=True))
        hh = num / jnp.maximum(jnp.abs(den), jnp.exp(-m_t))
        h_ref[:, sl] = _rms(hh, gh_ref[h:h + 1, :])

        g_last = b_c[t - 1:t, :]
        logw = g_last - b_c + ig_c
        m_new = jnp.maximum(g_last + m_prev, jnp.max(logw, axis=0, keepdims=True))
        ws = jnp.exp(logw - m_new)
        wc = jnp.exp(g_last + m_prev - m_new)
        vw = (v.astype(F32) * ws).astype(BF16)
        c_s[h] = wc * c_prev + lax.dot_general(vw, k, TN_DIMS, preferred_element_type=F32)
        n_s[h:h + 1, :] = wc * n_prev + jnp.sum(k.astype(F32) * ws, axis=0, keepdims=True)
        m_s[:, h:h + 1] = m_new

    @pl.when(c == pl.num_programs(1) - 1)
    def _store_state():
        c_ref[...] = c_s[...]
        n_ref[...] = n_s[...]
        m_ref[...] = m_s[...]


def _mlstm(q, k, v, gz, bias, gh, c0, n0, m0):
    b, l, _ = q.shape
    t = min(MLSTM_CHUNK, l)
    nc = l // t
    seq = lambda width: pl.BlockSpec((None, t, width), lambda i, j: (i, j, 0))
    const = lambda shape: pl.BlockSpec(shape, lambda i, j: (0,) * len(shape))
    cspec = pl.BlockSpec((None, H_M, DH_M, DH_M), lambda i, j: (i, 0, 0, 0))
    nspec = pl.BlockSpec((None, H_M, DH_M), lambda i, j: (i, 0, 0))
    mspec = pl.BlockSpec((None, 1, H_M), lambda i, j: (i, 0, 0))
    return pl.pallas_call(
        functools.partial(_mlstm_kernel, t=t),
        grid=(b, nc),
        in_specs=[seq(D_MODEL), seq(D_MODEL), seq(D_MODEL), seq(GATE_LANES),
                  const((1, GATE_LANES)), const((H_M, DH_M)), cspec, nspec, mspec],
        out_specs=[seq(D_MODEL), cspec, nspec, mspec],
        out_shape=[jax.ShapeDtypeStruct((b, l, D_MODEL), F32),
                   jax.ShapeDtypeStruct((b, H_M, DH_M, DH_M), F32),
                   jax.ShapeDtypeStruct((b, H_M, DH_M), F32),
                   jax.ShapeDtypeStruct((b, 1, H_M), F32)],
        scratch_shapes=[pltpu.VMEM((H_M, DH_M, DH_M), F32),
                        pltpu.VMEM((H_M, DH_M), F32),
                        pltpu.VMEM((1, H_M), F32)],
        compiler_params=pltpu.CompilerParams(
            dimension_semantics=("arbitrary", "arbitrary"), vmem_limit_bytes=V7X_VMEM_LIMIT),
        name="mlstm",
    )(q, k, v, gz, bias, gh, c0, n0, m0)


def _gelu_tanh(x):
    return 0.5 * x * (1.0 + jnp.tanh(math.sqrt(2.0 / math.pi) * (x + 0.044715 * (x * x * x))))


def _ffn_kernel(x_ref, attn_ref, hm_ref, om_ref, ga_ref, gm_ref, conv0_ref, wout_ref, wup_ref,
                wdn_ref, gpm_ref, gpf_ref, gpo_ref, cw_ref, cb_ref, y_ref, cs_ref, g_s,
                *, nb, l, tiles_per_batch):
    i = pl.program_id(0)
    sig = jax.nn.sigmoid
    merged = (sig(ga_ref[...]) * attn_ref[...]
              + sig(gm_ref[...]) * (sig(om_ref[...]) * hm_ref[...]))
    x1 = x_ref[...] + _rms(jnp.dot(merged.astype(BF16), wout_ref[...],
                                   preferred_element_type=F32), gpm_ref[...])
    h2 = _rms(x1, gpf_ref[...]).astype(BF16)
    u = jnp.dot(h2, wup_ref[:, :D_FF], preferred_element_type=F32)
    g = jnp.dot(h2, wup_ref[:, D_FF:], preferred_element_type=F32)

    @pl.when(i % tiles_per_batch == 0)
    def _from_state():
        g_s[:, 6:8, :] = conv0_ref[...]

    @pl.when(i % tiles_per_batch != 0)
    def _from_prev_tile():
        g_s[:, 6:8, :] = g_s[:, l + 6:l + 8, :]

    g_s[:, 8:8 + l, :] = g.reshape(nb, l, D_FF)
    cs_ref[...] = g_s[:, l + 6:l + 8, :]
    gconv = (cb_ref[...] + cw_ref[0:1, :] * g_s[:, 6:6 + l, :] + cw_ref[1:2, :] * g_s[:, 7:7 + l, :]
             + cw_ref[2:3, :] * g_s[:, 8:8 + l, :]).reshape(nb * l, D_FF)
    ff = jnp.dot((_gelu_tanh(gconv) * u).astype(BF16), wdn_ref[...], preferred_element_type=F32)
    y_ref[...] = x1 + _rms(ff, gpo_ref[...])


def _ffn(x2d, attn, hm, om, ga, gm, conv0, wout, wup, wdn, gpm, gpf, gpo, cw, cb, nb_total, seq):
    n = x2d.shape[0]
    if seq >= FFN_ROWS:
        tm, nb, l = FFN_ROWS, 1, FFN_ROWS
        tiles_per_batch = seq // tm
    else:
        tm, nb, l = n, nb_total, seq
        tiles_per_batch = 1
    row = pl.BlockSpec((tm, D_MODEL), lambda i: (i, 0))
    cstate = pl.BlockSpec((nb, CONV_W - 1, D_FF), lambda i: (i // tiles_per_batch, 0, 0))
    return pl.pallas_call(
        functools.partial(_ffn_kernel, nb=nb, l=l, tiles_per_batch=tiles_per_batch),
        grid=(n // tm,),
        in_specs=[row, row, row, row, row, row, cstate,
                  _resident((D_MODEL, D_MODEL)), _resident((D_MODEL, 2 * D_FF)),
                  _resident((D_FF, D_MODEL)), _resident((1, D_MODEL)), _resident((1, D_MODEL)),
                  _resident((1, D_MODEL)), _resident((CONV_W, D_FF)), _resident((1, D_FF))],
        out_specs=[row, cstate],
        out_shape=[jax.ShapeDtypeStruct((n, D_MODEL), F32),
                   jax.ShapeDtypeStruct((nb_total, CONV_W - 1, D_FF), F32)],
        scratch_shapes=[pltpu.VMEM((nb, l + 8, D_FF), F32)],
        compiler_params=pltpu.CompilerParams(
            dimension_semantics=("arbitrary",), vmem_limit_bytes=V7X_VMEM_LIMIT),
        name="merge_ffn",
    )(x2d, attn, hm, om, ga, gm, conv0, wout, wup, wdn, gpm, gpf, gpo, cw, cb)


def _layer(x, cache_k, cache_v, c0, n0, m0, conv0, wts, lam_init):
    (g_pre_mix, w_cat, gate_bias, lq1, lk1, lq2, lk2, g_attn_head, g_mlstm_head, wout, g_post_mix,
     g_pre_ffn, wup, conv_w, conv_b, wdn, g_post_ffn) = wts
    b, l, _ = x.shape
    x2d = x.reshape(b * l, D_MODEL)
    (qa, kf, kb, vf, vb, qm, km, vm, om, ga, gm, gz) = _inproj(x2d, g_pre_mix, w_cat, b, l)

    if cache_k is None:
        attn = _attn_prompt(qa, kb, vb, lq1, lk1, lq2, lk2, g_attn_head, lam_init)
    else:
        p = cache_k.shape[1]
        attn = _attn_sample(qa, kb, vb, cache_k.reshape(b, p, H_A * DV_A),
                            cache_v.reshape(b, p, H_A * DV_A),
                            lq1, lk1, lq2, lk2, g_attn_head, lam_init)

    seq3 = lambda a: a.reshape(b, l, a.shape[-1])
    hm, c1, n1, m1 = _mlstm(seq3(qm), seq3(km), seq3(vm), seq3(gz), gate_bias, g_mlstm_head,
                            c0, n0, m0.reshape(b, 1, H_M))

    y, conv1 = _ffn(x2d, attn.reshape(b * l, D_MODEL), hm.reshape(b * l, D_MODEL), om, ga, gm,
                    conv0, wout, wup, wdn, g_post_mix, g_pre_ffn, g_post_ffn, conv_w, conv_b, b, l)
    return (y.reshape(b, l, D_MODEL), kf.reshape(b, l, H_A, DV_A), vf.reshape(b, l, H_A, DV_A),
            c1, n1, m1.reshape(b, H_M), conv1)


def kernel(x_prompt, x_sample, cache_k, cache_v, state_C, state_n, state_m, state_conv, g_pre_mix, w_in, b_gates, lam_q1, lam_k1, lam_q2, lam_k2, g_attn_head, g_mlstm_head, w_out, g_post_mix, g_pre_ffn, w_up, conv_w, conv_b, w_down, g_post_ffn):
    depth = w_in.shape[0]
    bp = x_prompt.shape[0]
    yp, ys = x_prompt, x_sample
    outs_p, outs_s = [], []
    n_main = 6 * D_MODEL
    for li in range(depth):
        lam_init = 0.8 - 0.6 * math.exp(-0.3 * li)
        w = w_in[li]
        w_cat = jnp.concatenate(
            [w[:, :n_main], w[:, n_main + 2 * H_M:], w[:, n_main:n_main + 2 * H_M],
             jnp.zeros((D_MODEL, GATE_LANES - 2 * H_M), w.dtype)], axis=1).astype(BF16)
        gate_bias = jnp.concatenate(
            [b_gates[li], jnp.zeros((GATE_LANES - 2 * H_M,), F32)]).reshape(1, GATE_LANES)
        row = lambda a: a.reshape(1, -1)
        wts = (row(g_pre_mix[li]), w_cat, gate_bias, row(lam_q1[li]), row(lam_k1[li]),
               row(lam_q2[li]), row(lam_k2[li]), row(g_attn_head[li]), g_mlstm_head[li],
               w_out[li].astype(BF16), row(g_post_mix[li]), row(g_pre_ffn[li]),
               w_up[li].astype(BF16), conv_w[li], row(conv_b[li]), w_down[li].astype(BF16),
               row(g_post_ffn[li]))
        c0 = jnp.zeros((bp, H_M, DH_M, DH_M), F32)
        n0 = jnp.zeros((bp, H_M, DH_M), F32)
        m0 = jnp.zeros((bp, H_M), F32)
        conv0 = jnp.zeros((bp, CONV_W - 1, D_FF), F32)
        yp, *sp = _layer(yp, None, None, c0, n0, m0, conv0, wts, lam_init)
        ys, *ss = _layer(ys, cache_k[li], cache_v[li], state_C[li], state_n[li], state_m[li],
                         state_conv[li], wts, lam_init)
        outs_p.append(sp)
        outs_s.append(ss)
    k_p, v_p, c_p, n_p, m_p, conv_p = [jnp.stack([o[i] for o in outs_p]) for i in range(6)]
    k_s, v_s, c_s, n_s, m_s, conv_s = [jnp.stack([o[i] for o in outs_s]) for i in range(6)]
    return (yp, ys, k_p, v_p, c_p, n_p, m_p, conv_p, k_s, v_s, c_s, n_s, m_s, conv_s)
```

```python
import functools
import math

import jax
import jax.numpy as jnp
import numpy as np
from jax import lax
from jax.experimental import pallas as pl
from jax.experimental.pallas import tpu as pltpu

F32 = jnp.float32
BF16 = jnp.bfloat16

D_MODEL = 1024
CHUNK = 64
H_A = 8
DH_A = 64
DV_A = 2 * DH_A
H_M = 4
DH_M = D_MODEL // H_M
D_FF = 2816
CONV_W = 3
EPS = 1e-6
N_SEG = 9
GATE_LANES = 128
W_CAT = N_SEG * D_MODEL + GATE_LANES
QK_SCALE = DH_A ** -0.5 * math.log2(math.e)

V7X_VMEM_LIMIT = 56 * 1024 * 1024

PROJ_ROWS = 256
ATTN_TILE = 512
MLSTM_CHUNK = 256
FFN_ROWS = 256

NT_DIMS = (((1,), (1,)), ((), ()))
TN_DIMS = (((0,), (0,)), ((), ()))


def _rms(x, g):
    return x * lax.rsqrt(jnp.mean(x * x, axis=-1, keepdims=True) + EPS) * g


def _resident(shape):
    nd = len(shape)
    return pl.BlockSpec(shape, lambda *_: (0,) * nd, pipeline_mode=pl.Buffered(1))


def _inproj_kernel(x_ref, g_ref, w_ref, qa_ref, kf_ref, kb_ref, vf_ref, vb_ref,
                   qm_ref, km_ref, vm_ref, om_ref, ga_ref, gm_ref, gz_ref, *, nbt, lt, transposed):
    hb = _rms(x_ref[...], g_ref[...]).astype(BF16)

    def seg(j, width=D_MODEL):
        return jnp.dot(hb, w_ref[:, j * D_MODEL:j * D_MODEL + width],
                       preferred_element_type=F32)

    def put_heads(ref, val, transpose):
        for h in range(H_A):
            piece = val[:, h * DV_A:(h + 1) * DV_A]
            if transpose:
                ref[0, h] = piece.T.astype(BF16)
            else:
                ref[:, h] = piece.astype(BF16).reshape(nbt, lt, DV_A)

    put_heads(qa_ref, seg(0) * QK_SCALE, transposed)
    k = seg(1)
    kf_ref[...] = k
    put_heads(kb_ref, k, False)
    v = seg(2)
    vf_ref[...] = v
    put_heads(vb_ref, v, transposed)
    qm_ref[...] = seg(3).astype(BF16)
    km_ref[...] = seg(4).astype(BF16)
    vm_ref[...] = seg(5).astype(BF16)
    om_ref[...] = seg(6)
    ga_ref[...] = seg(7)
    gm_ref[...] = seg(8)
    gz_ref[...] = seg(9, GATE_LANES)


def _inproj(x2d, g, w_cat, nb, seq):
    n = x2d.shape[0]
    tm = PROJ_ROWS
    transposed = seq >= tm
    if transposed:
        nbt, lt = 1, tm
        per_b = seq // tm
        head = pl.BlockSpec((1, H_A, lt, DV_A), lambda i: (i // per_b, 0, i % per_b, 0))
        head_t = pl.BlockSpec((1, H_A, DV_A, lt), lambda i: (i // per_b, 0, 0, i % per_b))
        hm16_t = jax.ShapeDtypeStruct((nb, H_A, DV_A, seq), BF16)
    else:
        nbt, lt = tm // seq, seq
        head = head_t = pl.BlockSpec((nbt, H_A, lt, DV_A), lambda i: (i, 0, 0, 0))
        hm16_t = jax.ShapeDtypeStruct((nb, H_A, seq, DV_A), BF16)
    row = pl.BlockSpec((tm, D_MODEL), lambda i: (i, 0))
    full32 = jax.ShapeDtypeStruct((n, D_MODEL), F32)
    full16 = jax.ShapeDtypeStruct((n, D_MODEL), BF16)
    hm16 = jax.ShapeDtypeStruct((nb, H_A, seq, DV_A), BF16)
    return pl.pallas_call(
        functools.partial(_inproj_kernel, nbt=nbt, lt=lt, transposed=transposed),
        grid=(n // tm,),
        in_specs=[row, _resident((1, D_MODEL)), _resident((D_MODEL, W_CAT))],
        out_specs=[head_t, row, head, row, head_t, row, row, row, row, row, row,
                   pl.BlockSpec((tm, GATE_LANES), lambda i: (i, 0))],
        out_shape=[hm16_t, full32, hm16, full32, hm16_t, full16, full16, full16,
                   full32, full32, full32, jax.ShapeDtypeStruct((n, GATE_LANES), F32)],
        compiler_params=pltpu.CompilerParams(
            dimension_semantics=("arbitrary",), vmem_limit_bytes=V7X_VMEM_LIMIT),
        name="inproj",
    )(x2d, g, w_cat)


def _lam(lq1, lk1, lq2, lk2, lam_init):
    return (jnp.exp(jnp.sum(lq1 * lk1, axis=-1, keepdims=True))
            - jnp.exp(jnp.sum(lq2 * lk2, axis=-1, keepdims=True)) + lam_init)


def _split_maps(q, axis):
    idx = lax.broadcasted_iota(jnp.int32, q.shape, axis)
    zero = jnp.zeros_like(q)
    return jnp.where(idx < DH_A, q, zero), jnp.where(idx >= DH_A, q, zero)


def _attn_prompt_kernel(qi_tab, kj_tab, qt_ref, k_ref, vt_ref, lq1_ref, lk1_ref, lq2_ref, lk2_ref,
                        gcol_ref, o_ref, qz_s, m_s, l_s, acc_s, *, lam_init):
    p = pl.program_id(1)
    qi = qi_tab[p]
    kj = kj_tab[p]
    t = ATTN_TILE

    @pl.when(kj == 0)
    def _init():
        for h in range(H_A):
            q1, q2 = _split_maps(qt_ref[h], 0)
            qz_s[0, h] = q1
            qz_s[1, h] = q2
        m_s[...] = jnp.full(m_s.shape, -jnp.inf, F32)
        l_s[...] = jnp.zeros(l_s.shape, F32)
        acc_s[...] = jnp.zeros(acc_s.shape, F32)

    def sweep(masked):
        if masked:
            kr = lax.broadcasted_iota(jnp.int32, (t, t), 0) // CHUNK
            qc = lax.broadcasted_iota(jnp.int32, (t, t), 1) // CHUNK
            visible = kr <= qc
        for h in range(H_A):
            k = k_ref[h]
            vt = vt_ref[h]
            for mp in range(2):
                s = jnp.dot(k, qz_s[mp, h], preferred_element_type=F32)
                if masked:
                    s = jnp.where(visible, s, -jnp.inf)
                m_old = m_s[mp, h]
                m_new = jnp.maximum(m_old, jnp.max(s, axis=0, keepdims=True))
                alpha = jnp.exp2(m_old - m_new)
                pr = jnp.exp2(s - m_new)
                l_s[mp, h] = alpha * l_s[mp, h] + jnp.sum(pr, axis=0, keepdims=True)
                acc_s[mp, h] = alpha * acc_s[mp, h] + jnp.dot(
                    vt, pr.astype(BF16), preferred_element_type=F32)
                m_s[mp, h] = m_new

    @pl.when(kj < qi)
    def _full():
        sweep(False)

    @pl.when(kj == qi)
    def _diag():
        sweep(True)
        lam = _lam(lq1_ref[...], lk1_ref[...], lq2_ref[...], lk2_ref[...], lam_init)
        for h in range(H_A):
            ot = acc_s[0, h] / l_s[0, h] - lam * (acc_s[1, h] / l_s[1, h])
            ms = jnp.mean(ot * ot, axis=0, keepdims=True)
            ot = ot * lax.rsqrt(ms + EPS) * gcol_ref[...] * (1.0 - lam_init)
            o_ref[:, h * DV_A:(h + 1) * DV_A] = ot.T


def _attn_prompt(qt, k, vt, lq1, lk1, lq2, lk2, gh, lam_init):
    b, _, s, _ = k.shape
    t = ATTN_TILE
    nq = s // t
    pairs = [(i, j) for i in range(nq) for j in range(i + 1)]
    qi_tab = jnp.asarray(np.array([a for a, _ in pairs], np.int32))
    kj_tab = jnp.asarray(np.array([c for _, c in pairs], np.int32))
    qspec = pl.BlockSpec((None, H_A, DV_A, t), lambda bb, p, qi, kj: (bb, 0, 0, qi[p]))
    kspec = pl.BlockSpec((None, H_A, t, DV_A), lambda bb, p, qi, kj: (bb, 0, kj[p], 0))
    vspec = pl.BlockSpec((None, H_A, DV_A, t), lambda bb, p, qi, kj: (bb, 0, 0, kj[p]))
    small = lambda shape: pl.BlockSpec(shape, lambda bb, p, qi, kj: (0, 0))
    grid_spec = pltpu.PrefetchScalarGridSpec(
        num_scalar_prefetch=2,
        grid=(b, len(pairs)),
        in_specs=[qspec, kspec, vspec, small((1, DH_A)), small((1, DH_A)), small((1, DH_A)),
                  small((1, DH_A)), small((DV_A, 1))],
        out_specs=pl.BlockSpec((None, t, H_A * DV_A), lambda bb, p, qi, kj: (bb, qi[p], 0)),
        scratch_shapes=[pltpu.VMEM((2, H_A, DV_A, t), BF16),
                        pltpu.VMEM((2, H_A, 1, t), F32),
                        pltpu.VMEM((2, H_A, 1, t), F32),
                        pltpu.VMEM((2, H_A, DV_A, t), F32)],
    )
    return pl.pallas_call(
        functools.partial(_attn_prompt_kernel, lam_init=lam_init),
        grid_spec=grid_spec,
        out_shape=jax.ShapeDtypeStruct((b, s, H_A * DV_A), F32),
        compiler_params=pltpu.CompilerParams(
            dimension_semantics=("arbitrary", "arbitrary"), vmem_limit_bytes=V7X_VMEM_LIMIT),
        name="attn_prompt",
    )(qi_tab, kj_tab, qt, k, vt, lq1, lk1, lq2, lk2, gh.reshape(DV_A, 1))


def _attn_sample_kernel(q_ref, kn_ref, vn_ref, ck_ref, cv_ref, lq1_ref, lk1_ref, lq2_ref, lk2_ref,
                        gh_ref, o_ref, *, lam_init):
    lam = _lam(lq1_ref[...], lk1_ref[...], lq2_ref[...], lk2_ref[...], lam_init)
    nq = q_ref.shape[1]
    for h in range(H_A):
        sl = slice(h * DV_A, (h + 1) * DV_A)
        q1, q2 = _split_maps(q_ref[h], 1)
        qz = jnp.concatenate([q1, q2], axis=0)
        kc = ck_ref[:, sl].astype(BF16)
        sc = lax.dot_general(qz, kc, NT_DIMS, preferred_element_type=F32)
        sn = lax.dot_general(qz, kn_ref[h], NT_DIMS, preferred_element_type=F32)
        m = jnp.maximum(jnp.max(sc, axis=-1, keepdims=True), jnp.max(sn, axis=-1, keepdims=True))
        pc = jnp.exp2(sc - m)
        pn = jnp.exp2(sn - m)
        inv = 1.0 / (jnp.sum(pc, axis=-1, keepdims=True) + jnp.sum(pn, axis=-1, keepdims=True))
        pc = pc * inv
        pn = pn * inv
        ac = pc[:nq] - lam * pc[nq:]
        an = pn[:nq] - lam * pn[nq:]
        o = (jnp.dot(ac.astype(BF16), cv_ref[:, sl].astype(BF16), preferred_element_type=F32)
             + jnp.dot(an.astype(BF16), vn_ref[h], preferred_element_type=F32))
        o_ref[:, sl] = _rms(o, gh_ref[...]) * (1.0 - lam_init)


def _attn_sample(q, kn, vn, cache_k, cache_v, lq1, lk1, lq2, lk2, gh, lam_init):
    b, _, l, _ = q.shape
    p = cache_k.shape[1]
    hspec = pl.BlockSpec((None, H_A, l, DV_A), lambda i: (i, 0, 0, 0))
    cspec = pl.BlockSpec((None, p, H_A * DV_A), lambda i: (i, 0, 0))
    small = lambda shape: pl.BlockSpec(shape, lambda i: (0, 0))
    return pl.pallas_call(
        functools.partial(_attn_sample_kernel, lam_init=lam_init),
        grid=(b,),
        in_specs=[hspec, hspec, hspec, cspec, cspec, small((1, DH_A)), small((1, DH_A)),
                  small((1, DH_A)), small((1, DH_A)), small((1, DV_A))],
        out_specs=pl.BlockSpec((None, l, H_A * DV_A), lambda i: (i, 0, 0)),
        out_shape=jax.ShapeDtypeStruct((b, l, H_A * DV_A), F32),
        compiler_params=pltpu.CompilerParams(
            dimension_semantics=("arbitrary",), vmem_limit_bytes=V7X_VMEM_LIMIT),
        name="attn_sample",
    )(q, kn, vn, cache_k, cache_v, lq1, lk1, lq2, lk2, gh)


def _split3(x):
    hi = x.astype(BF16)
    r1 = x - hi.astype(F32)
    mid = r1.astype(BF16)
    lo = (r1 - mid.astype(F32)).astype(BF16)
    return hi, mid, lo


def _mlstm_kernel(q_ref, k_ref, v_ref, gz_ref, bias_ref, gh_ref, c0_ref, n0_ref, m0_ref,
                  h_ref, c_ref, n_ref, m_ref, c_s, n_s, m_s, *, t):
    c = pl.program_id(1)

    @pl.when(c == 0)
    def _load_state():
        c_s[...] = c0_ref[...]
        n_s[...] = n0_ref[...]
        m_s[...] = m0_ref[...]

    gz = gz_ref[...] + bias_ref[...]
    lane = lax.broadcasted_iota(jnp.int32, gz.shape, 1)
    lf = jnp.minimum(gz, 0.0) - jnp.log1p(jnp.exp(-jnp.abs(gz)))
    lf = jnp.where((lane >= H_M) & (lane < 2 * H_M), lf, 0.0)
    row = lax.broadcasted_iota(jnp.int32, (t, t), 0)
    col = lax.broadcasted_iota(jnp.int32, (t, t), 1)
    causal = col <= row
    tril = jnp.where(causal, 1.0, 0.0).astype(BF16)
    bcum = sum(jnp.dot(tril, piece, preferred_element_type=F32) for piece in _split3(lf))
    gc = jnp.where(lane < H_M, gz, bcum)
    sel = jnp.where(lax.broadcasted_iota(jnp.int32, (8, GATE_LANES), 0)
                    == lax.broadcasted_iota(jnp.int32, (8, GATE_LANES), 1), 1.0, 0.0).astype(BF16)
    gr = sum(lax.dot_general(sel, piece, NT_DIMS, preferred_element_type=F32)
             for piece in _split3(gc))

    for h in range(H_M):
        sl = slice(h * DH_M, (h + 1) * DH_M)
        ig_c, b_c = gc[:, h:h + 1], gc[:, H_M + h:H_M + h + 1]
        ig_r, b_r = gr[h:h + 1, :], gr[H_M + h:H_M + h + 1, :]
        m_prev = m_s[:, h:h + 1]
        c_prev = c_s[h]
        n_prev = n_s[h:h + 1, :]
        q = q_ref[:, sl] * (DH_M ** -0.5)
        k = k_ref[:, sl]
        v = v_ref[:, sl]

        d = jnp.where(causal, b_c - b_r + ig_r, -jnp.inf)
        inter = b_c + m_prev
        m_t = jnp.maximum(inter, jnp.max(d, axis=-1, keepdims=True))
        w_intra = jnp.exp(d - m_t)
        w_inter = jnp.exp(inter - m_t)
        s = lax.dot_general(q, k, NT_DIMS, preferred_element_type=F32) * w_intra
        num = (jnp.dot(s.astype(BF16), v, preferred_element_type=F32)
               + w_inter * lax.dot_general(q, c_prev.astype(BF16), NT_DIMS,
                                           preferred_element_type=F32))
        den = (jnp.sum(s, axis=-1, keepdims=True)
               + w_inter * jnp.sum(q.astype(F32) * n_prev, axis=-1, keepdims=True))
        hh = num / jnp.maximum(jnp.abs(den), jnp.exp(-m_t))
        h_ref[:, sl] = _rms(hh, gh_ref[h:h + 1, :])

        g_last = b_c[t - 1:t, :]
        logw = g_last - b_c + ig_c
        m_new = jnp.maximum(g_last + m_prev, jnp.max(logw, axis=0, keepdims=True))
        ws = jnp.exp(logw - m_new)
        wc = jnp.exp(g_last + m_prev - m_new)
        vw = (v.astype(F32) * ws).astype(BF16)
        c_s[h] = wc * c_prev + lax.dot_general(vw, k, TN_DIMS, preferred_element_type=F32)
        n_s[h:h + 1, :] = wc * n_prev + jnp.sum(k.astype(F32) * ws, axis=0, keepdims=True)
        m_s[:, h:h + 1] = m_new

    @pl.when(c == pl.num_programs(1) - 1)
    def _store_state():
        c_ref[...] = c_s[...]
        n_ref[...] = n_s[...]
        m_ref[...] = m_s[...]


def _mlstm(q, k, v, gz, bias, gh, c0, n0, m0):
    b, l, _ = q.shape
    t = min(MLSTM_CHUNK, l)
    nc = l // t
    seq = lambda width: pl.BlockSpec((None, t, width), lambda i, j: (i, j, 0))
    const = lambda shape: pl.BlockSpec(shape, lambda i, j: (0,) * len(shape))
    cspec = pl.BlockSpec((None, H_M, DH_M, DH_M), lambda i, j: (i, 0, 0, 0))
    nspec = pl.BlockSpec((None, H_M, DH_M), lambda i, j: (i, 0, 0))
    mspec = pl.BlockSpec((None, 1, H_M), lambda i, j: (i, 0, 0))
    return pl.pallas_call(
        functools.partial(_mlstm_kernel, t=t),
        grid=(b, nc),
        in_specs=[seq(D_MODEL), seq(D_MODEL), seq(D_MODEL), seq(GATE_LANES),
                  const((1, GATE_LANES)), const((H_M, DH_M)), cspec, nspec, mspec],
        out_specs=[seq(D_MODEL), cspec, nspec, mspec],
        out_shape=[jax.ShapeDtypeStruct((b, l, D_MODEL), F32),
                   jax.ShapeDtypeStruct((b, H_M, DH_M, DH_M), F32),
                   jax.ShapeDtypeStruct((b, H_M, DH_M), F32),
                   jax.ShapeDtypeStruct((b, 1, H_M), F32)],
        scratch_shapes=[pltpu.VMEM((H_M, DH_M, DH_M), F32),
                        pltpu.VMEM((H_M, DH_M), F32),
                        pltpu.VMEM((1, H_M), F32)],
        compiler_params=pltpu.CompilerParams(
            dimension_semantics=("arbitrary", "arbitrary"), vmem_limit_bytes=V7X_VMEM_LIMIT),
        name="mlstm",
    )(q, k, v, gz, bias, gh, c0, n0, m0)


def _gelu_tanh(x):
    return 0.5 * x * (1.0 + jnp.tanh(math.sqrt(2.0 / math.pi) * (x + 0.044715 * (x * x * x))))


def _ffn_kernel(x_ref, attn_ref, hm_ref, om_ref, ga_ref, gm_ref, conv0_ref, wout_ref, wup_ref,
                wdn_ref, gpm_ref, gpf_ref, gpo_ref, cw_ref, cb_ref, y_ref, cs_ref, g_s,
                *, nb, l, tiles_per_batch):
    i = pl.program_id(0)
    sig = jax.nn.sigmoid
    merged = (sig(ga_ref[...]) * attn_ref[...]
              + sig(gm_ref[...]) * (sig(om_ref[...]) * hm_ref[...]))
    x1 = x_ref[...] + _rms(jnp.dot(merged.astype(BF16), wout_ref[...],
                                   preferred_element_type=F32), gpm_ref[...])
    h2 = _rms(x1, gpf_ref[...]).astype(BF16)
    u = jnp.dot(h2, wup_ref[:, :D_FF], preferred_element_type=F32)
    g = jnp.dot(h2, wup_ref[:, D_FF:], preferred_element_type=F32)

    @pl.when(i % tiles_per_batch == 0)
    def _from_state():
        g_s[:, 6:8, :] = conv0_ref[...]

    @pl.when(i % tiles_per_batch != 0)
    def _from_prev_tile():
        g_s[:, 6:8, :] = g_s[:, l + 6:l + 8, :]

    g_s[:, 8:8 + l, :] = g.reshape(nb, l, D_FF)
    cs_ref[...] = g_s[:, l + 6:l + 8, :]
    gconv = (cb_ref[...] + cw_ref[0:1, :] * g_s[:, 6:6 + l, :] + cw_ref[1:2, :] * g_s[:, 7:7 + l, :]
             + cw_ref[2:3, :] * g_s[:, 8:8 + l, :]).reshape(nb * l, D_FF)
    ff = jnp.dot((_gelu_tanh(gconv) * u).astype(BF16), wdn_ref[...], preferred_element_type=F32)
    y_ref[...] = x1 + _rms(ff, gpo_ref[...])


def _ffn(x2d, attn, hm, om, ga, gm, conv0, wout, wup, wdn, gpm, gpf, gpo, cw, cb, nb_total, seq):
    n = x2d.shape[0]
    if seq >= FFN_ROWS:
        tm, nb, l = FFN_ROWS, 1, FFN_ROWS
        tiles_per_batch = seq // tm
    else:
        tm, nb, l = n, nb_total, seq
        tiles_per_batch = 1
    row = pl.BlockSpec((tm, D_MODEL), lambda i: (i, 0))
    cstate = pl.BlockSpec((nb, CONV_W - 1, D_FF), lambda i: (i // tiles_per_batch, 0, 0))
    return pl.pallas_call(
        functools.partial(_ffn_kernel, nb=nb, l=l, tiles_per_batch=tiles_per_batch),
        grid=(n // tm,),
        in_specs=[row, row, row, row, row, row, cstate,
                  _resident((D_MODEL, D_MODEL)), _resident((D_MODEL, 2 * D_FF)),
                  _resident((D_FF, D_MODEL)), _resident((1, D_MODEL)), _resident((1, D_MODEL)),
                  _resident((1, D_MODEL)), _resident((CONV_W, D_FF)), _resident((1, D_FF))],
        out_specs=[row, cstate],
        out_shape=[jax.ShapeDtypeStruct((n, D_MODEL), F32),
                   jax.ShapeDtypeStruct((nb_total, CONV_W - 1, D_FF), F32)],
        scratch_shapes=[pltpu.VMEM((nb, l + 8, D_FF), F32)],
        compiler_params=pltpu.CompilerParams(
            dimension_semantics=("arbitrary",), vmem_limit_bytes=V7X_VMEM_LIMIT),
        name="merge_ffn",
    )(x2d, attn, hm, om, ga, gm, conv0, wout, wup, wdn, gpm, gpf, gpo, cw, cb)


def _layer(x, cache_k, cache_v, c0, n0, m0, conv0, wts, lam_init):
    (g_pre_mix, w_cat, gate_bias, lq1, lk1, lq2, lk2, g_attn_head, g_mlstm_head, wout, g_post_mix,
     g_pre_ffn, wup, conv_w, conv_b, wdn, g_post_ffn) = wts
    b, l, _ = x.shape
    x2d = x.reshape(b * l, D_MODEL)
    (qa, kf, kb, vf, vb, qm, km, vm, om, ga, gm, gz) = _inproj(x2d, g_pre_mix, w_cat, b, l)

    if cache_k is None:
        attn = _attn_prompt(qa, kb, vb, lq1, lk1, lq2, lk2, g_attn_head, lam_init)
    else:
        p = cache_k.shape[1]
        attn = _attn_sample(qa, kb, vb, cache_k.reshape(b, p, H_A * DV_A),
                            cache_v.reshape(b, p, H_A * DV_A),
                            lq1, lk1, lq2, lk2, g_attn_head, lam_init)

    seq3 = lambda a: a.reshape(b, l, a.shape[-1])
    hm, c1, n1, m1 = _mlstm(seq3(qm), seq3(km), seq3(vm), seq3(gz), gate_bias, g_mlstm_head,
                            c0, n0, m0.reshape(b, 1, H_M))

    y, conv1 = _ffn(x2d, attn.reshape(b * l, D_MODEL), hm.reshape(b * l, D_MODEL), om, ga, gm,
                    conv0, wout, wup, wdn, g_post_mix, g_pre_ffn, g_post_ffn, conv_w, conv_b, b, l)
    return (y.reshape(b, l, D_MODEL), kf.reshape(b, l, H_A, DV_A), vf.reshape(b, l, H_A, DV_A),
            c1, n1, m1.reshape(b, H_M), conv1)


def kernel(x_prompt, x_sample, cache_k, cache_v, state_C, state_n, state_m, state_conv, g_pre_mix, w_in, b_gates, lam_q1, lam_k1, lam_q2, lam_k2, g_attn_head, g_mlstm_head, w_out, g_post_mix, g_pre_ffn, w_up, conv_w, conv_b, w_down, g_post_ffn):
    depth = w_in.shape[0]
    bp = x_prompt.shape[0]
    yp, ys = x_prompt, x_sample
    outs_p, outs_s = [], []
    n_main = 6 * D_MODEL
    for li in range(depth):
        lam_init = 0.8 - 0.6 * math.exp(-0.3 * li)
        w = w_in[li]
        w_cat = jnp.concatenate(
            [w[:, :n_main], w[:, n_main + 2 * H_M:], w[:, n_main:n_main + 2 * H_M],
             jnp.zeros((D_MODEL, GATE_LANES - 2 * H_M), w.dtype)], axis=1).astype(BF16)
        gate_bias = jnp.concatenate(
            [b_gates[li], jnp.zeros((GATE_LANES - 2 * H_M,), F32)]).reshape(1, GATE_LANES)
        row = lambda a: a.reshape(1, -1)
        wts = (row(g_pre_mix[li]), w_cat, gate_bias, row(lam_q1[li]), row(lam_k1[li]),
               row(lam_q2[li]), row(lam_k2[li]), row(g_attn_head[li]), g_mlstm_head[li],
               w_out[li].astype(BF16), row(g_post_mix[li]), row(g_pre_ffn[li]),
               w_up[li].astype(BF16), conv_w[li], row(conv_b[li]), w_down[li].astype(BF16),
               row(g_post_ffn[li]))
        c0 = jnp.zeros((bp, H_M, DH_M, DH_M), F32)
        n0 = jnp.zeros((bp, H_M, DH_M), F32)
        m0 = jnp.zeros((bp, H_M), F32)
        conv0 = jnp.zeros((bp, CONV_W - 1, D_FF), F32)
        yp, *sp = _layer(yp, None, None, c0, n0, m0, conv0, wts, lam_init)
        ys, *ss = _layer(ys, cache_k[li], cache_v[li], state_C[li], state_n[li], state_m[li],
                         state_conv[li], wts, lam_init)
        outs_p.append(sp)
        outs_s.append(ss)
    k_p, v_p, c_p, n_p, m_p, conv_p = [jnp.stack([o[i] for o in outs_p]) for i in range(6)]
    k_s, v_s, c_s, n_s, m_s, conv_s = [jnp.stack([o[i] for o in outs_s]) for i in range(6)]
    return (yp, ys, k_p, v_p, c_p, n_p, m_p, conv_p, k_s, v_s, c_s, n_s, m_s, conv_s)
```

```python
import functools
import math

import jax
import jax.numpy as jnp
import numpy as np
from jax import lax
from jax.experimental import pallas as pl
from jax.experimental.pallas import tpu as pltpu

F32 = jnp.float32
BF16 = jnp.bfloat16

D_MODEL = 1024
CHUNK = 64
H_A = 8
DH_A = 64
DV_A = 2 * DH_A
H_M = 4
DH_M = D_MODEL // H_M
D_FF = 2816
CONV_W = 3
EPS = 1e-6
N_SEG = 9
GATE_LANES = 128
W_CAT = N_SEG * D_MODEL + GATE_LANES
QK_SCALE = DH_A ** -0.5 * math.log2(math.e)

V7X_VMEM_LIMIT = 56 * 1024 * 1024

PROJ_ROWS = 256
ATTN_TILE = 512
MLSTM_CHUNK = 256
FFN_ROWS = 256

NT_DIMS = (((1,), (1,)), ((), ()))
TN_DIMS = (((0,), (0,)), ((), ()))


def _rms(x, g):
    return x * lax.rsqrt(jnp.mean(x * x, axis=-1, keepdims=True) + EPS) * g


def _resident(shape):
    nd = len(shape)
    return pl.BlockSpec(shape, lambda *_: (0,) * nd, pipeline_mode=pl.Buffered(1))


def _inproj_kernel(x_ref, g_ref, w_ref, qa_ref, kf_ref, kb_ref, vf_ref, vb_ref,
                   qm_ref, km_ref, vm_ref, om_ref, ga_ref, gm_ref, gz_ref, *, nbt, lt, transposed):
    hb = _rms(x_ref[...], g_ref[...]).astype(BF16)

    def seg(j, width=D_MODEL):
        return jnp.dot(hb, w_ref[:, j * D_MODEL:j * D_MODEL + width],
                       preferred_element_type=F32)

    def put_heads(ref, val, transpose):
        for h in range(H_A):
            piece = val[:, h * DV_A:(h + 1) * DV_A]
            if transpose:
                ref[0, h] = piece.T.astype(BF16)
            else:
                ref[:, h] = piece.astype(BF16).reshape(nbt, lt, DV_A)

    put_heads(qa_ref, seg(0) * QK_SCALE, transposed)
    k = seg(1)
    kf_ref[...] = k
    put_heads(kb_ref, k, False)
    v = seg(2)
    vf_ref[...] = v
    put_heads(vb_ref, v, transposed)
    qm_ref[...] = seg(3).astype(BF16)
    km_ref[...] = seg(4).astype(BF16)
    vm_ref[...] = seg(5).astype(BF16)
    om_ref[...] = seg(6)
    ga_ref[...] = seg(7)
    gm_ref[...] = seg(8)
    gz_ref[...] = seg(9, GATE_LANES)


def _inproj(x2d, g, w_cat, nb, seq):
    n = x2d.shape[0]
    tm = PROJ_ROWS
    transposed = seq >= tm
    if transposed:
        nbt, lt = 1, tm
        per_b = seq // tm
        head = pl.BlockSpec((1, H_A, lt, DV_A), lambda i: (i // per_b, 0, i % per_b, 0))
        head_t = pl.BlockSpec((1, H_A, DV_A, lt), lambda i: (i // per_b, 0, 0, i % per_b))
        hm16_t = jax.ShapeDtypeStruct((nb, H_A, DV_A, seq), BF16)
    else:
        nbt, lt = tm // seq, seq
        head = head_t = pl.BlockSpec((nbt, H_A, lt, DV_A), lambda i: (i, 0, 0, 0))
        hm16_t = jax.ShapeDtypeStruct((nb, H_A, seq, DV_A), BF16)
    row = pl.BlockSpec((tm, D_MODEL), lambda i: (i, 0))
    full32 = jax.ShapeDtypeStruct((n, D_MODEL), F32)
    full16 = jax.ShapeDtypeStruct((n, D_MODEL), BF16)
    hm16 = jax.ShapeDtypeStruct((nb, H_A, seq, DV_A), BF16)
    return pl.pallas_call(
        functools.partial(_inproj_kernel, nbt=nbt, lt=lt, transposed=transposed),
        grid=(n // tm,),
        in_specs=[row, _resident((1, D_MODEL)), _resident((D_MODEL, W_CAT))],
        out_specs=[head_t, row, head, row, head_t, row, row, row, row, row, row,
                   pl.BlockSpec((tm, GATE_LANES), lambda i: (i, 0))],
        out_shape=[hm16_t, full32, hm16, full32, hm16_t, full16, full16, full16,
                   full32, full32, full32, jax.ShapeDtypeStruct((n, GATE_LANES), F32)],
        compiler_params=pltpu.CompilerParams(
            dimension_semantics=("arbitrary",), vmem_limit_bytes=V7X_VMEM_LIMIT),
        name="inproj",
    )(x2d, g, w_cat)


def _lam(lq1, lk1, lq2, lk2, lam_init):
    return (jnp.exp(jnp.sum(lq1 * lk1, axis=-1, keepdims=True))
            - jnp.exp(jnp.sum(lq2 * lk2, axis=-1, keepdims=True)) + lam_init)


def _split_maps(q, axis):
    idx = lax.broadcasted_iota(jnp.int32, q.shape, axis)
    zero = jnp.zeros_like(q)
    return jnp.where(idx < DH_A, q, zero), jnp.where(idx >= DH_A, q, zero)


def _attn_prompt_kernel(qi_tab, kj_tab, qt_ref, k_ref, vt_ref, lq1_ref, lk1_ref, lq2_ref, lk2_ref,
                        gcol_ref, o_ref, qz_s, m_s, l_s, acc_s, *, lam_init):
    p = pl.program_id(1)
    qi = qi_tab[p]
    kj = kj_tab[p]
    t = ATTN_TILE

    @pl.when(kj == 0)
    def _init():
        for h in range(H_A):
            q1, q2 = _split_maps(qt_ref[h], 0)
            qz_s[0, h] = q1
            qz_s[1, h] = q2
        m_s[...] = jnp.full(m_s.shape, -jnp.inf, F32)
        l_s[...] = jnp.zeros(l_s.shape, F32)
        acc_s[...] = jnp.zeros(acc_s.shape, F32)

    def sweep(masked):
        if masked:
            kr = lax.broadcasted_iota(jnp.int32, (t, t), 0) // CHUNK
            qc = lax.broadcasted_iota(jnp.int32, (t, t), 1) // CHUNK
            visible = kr <= qc
        for h in range(H_A):
            k = k_ref[h]
            vt = vt_ref[h]
            for mp in range(2):
                s = jnp.dot(k, qz_s[mp, h], preferred_element_type=F32)
                if masked:
                    s = jnp.where(visible, s, -jnp.inf)
                m_old = m_s[mp, h]
                m_new = jnp.maximum(m_old, jnp.max(s, axis=0, keepdims=True))
                alpha = jnp.exp2(m_old - m_new)
                pr = jnp.exp2(s - m_new)
                l_s[mp, h] = alpha * l_s[mp, h] + jnp.sum(pr, axis=0, keepdims=True)
                acc_s[mp, h] = alpha * acc_s[mp, h] + jnp.dot(
                    vt, pr.astype(BF16), preferred_element_type=F32)
                m_s[mp, h] = m_new

    @pl.when(kj < qi)
    def _full():
        sweep(False)

    @pl.when(kj == qi)
    def _diag():
        sweep(True)
        lam = _lam(lq1_ref[...], lk1_ref[...], lq2_ref[...], lk2_ref[...], lam_init)
        for h in range(H_A):
            ot = acc_s[0, h] / l_s[0, h] - lam * (acc_s[1, h] / l_s[1, h])
            ms = jnp.mean(ot * ot, axis=0, keepdims=True)
            ot = ot * lax.rsqrt(ms + EPS) * gcol_ref[...] * (1.0 - lam_init)
            o_ref[:, h * DV_A:(h + 1) * DV_A] = ot.T


def _attn_prompt(qt, k, vt, lq1, lk1, lq2, lk2, gh, lam_init):
    b, _, s, _ = k.shape
    t = ATTN_TILE
    nq = s // t
    pairs = [(i, j) for i in range(nq) for j in range(i + 1)]
    qi_tab = jnp.asarray(np.array([a for a, _ in pairs], np.int32))
    kj_tab = jnp.asarray(np.array([c for _, c in pairs], np.int32))
    qspec = pl.BlockSpec((None, H_A, DV_A, t), lambda bb, p, qi, kj: (bb, 0, 0, qi[p]))
    kspec = pl.BlockSpec((None, H_A, t, DV_A), lambda bb, p, qi, kj: (bb, 0, kj[p], 0))
    vspec = pl.BlockSpec((None, H_A, DV_A, t), lambda bb, p, qi, kj: (bb, 0, 0, kj[p]))
    small = lambda shape: pl.BlockSpec(shape, lambda bb, p, qi, kj: (0, 0))
    grid_spec = pltpu.PrefetchScalarGridSpec(
        num_scalar_prefetch=2,
        grid=(b, len(pairs)),
        in_specs=[qspec, kspec, vspec, small((1, DH_A)), small((1, DH_A)), small((1, DH_A)),
                  small((1, DH_A)), small((DV_A, 1))],
        out_specs=pl.BlockSpec((None, t, H_A * DV_A), lambda bb, p, qi, kj: (bb, qi[p], 0)),
        scratch_shapes=[pltpu.VMEM((2, H_A, DV_A, t), BF16),
                        pltpu.VMEM((2, H_A, 1, t), F32),
                        pltpu.VMEM((2, H_A, 1, t), F32),
                        pltpu.VMEM((2, H_A, DV_A, t), F32)],
    )
    return pl.pallas_call(
        functools.partial(_attn_prompt_kernel, lam_init=lam_init),
        grid_spec=grid_spec,
        out_shape=jax.ShapeDtypeStruct((b, s, H_A * DV_A), F32),
        compiler_params=pltpu.CompilerParams(
            dimension_semantics=("arbitrary", "arbitrary"), vmem_limit_bytes=V7X_VMEM_LIMIT),
        name="attn_prompt",
    )(qi_tab, kj_tab, qt, k, vt, lq1, lk1, lq2, lk2, gh.reshape(DV_A, 1))


def _attn_sample_kernel(q_ref, kn_ref, vn_ref, ck_ref, cv_ref, lq1_ref, lk1_ref, lq2_ref, lk2_ref,
                        gh_ref, o_ref, *, lam_init):
    lam = _lam(lq1_ref[...], lk1_ref[...], lq2_ref[...], lk2_ref[...], lam_init)
    nq = q_ref.shape[1]
    past = ck_ref.shape[0] // H_A
    for h in range(H_A):
        sl = slice(h * DV_A, (h + 1) * DV_A)
        rows = pl.ds(h, past, stride=H_A)
        q1, q2 = _split_maps(q_ref[h], 1)
        qz = jnp.concatenate([q1, q2], axis=0)
        kc = ck_ref[rows, :].astype(BF16)
        sc = lax.dot_general(qz, kc, NT_DIMS, preferred_element_type=F32)
        sn = lax.dot_general(qz, kn_ref[h], NT_DIMS, preferred_element_type=F32)
        m = jnp.maximum(jnp.max(sc, axis=-1, keepdims=True), jnp.max(sn, axis=-1, keepdims=True))
        pc = jnp.exp2(sc - m)
        pn = jnp.exp2(sn - m)
        inv = 1.0 / (jnp.sum(pc, axis=-1, keepdims=True) + jnp.sum(pn, axis=-1, keepdims=True))
        pc = pc * inv
        pn = pn * inv
        ac = pc[:nq] - lam * pc[nq:]
        an = pn[:nq] - lam * pn[nq:]
        o = (jnp.dot(ac.astype(BF16), cv_ref[rows, :].astype(BF16), preferred_element_type=F32)
             + jnp.dot(an.astype(BF16), vn_ref[h], preferred_element_type=F32))
        o_ref[:, sl] = _rms(o, gh_ref[...]) * (1.0 - lam_init)


def _attn_sample(q, kn, vn, cache_k, cache_v, b_off, lq1, lk1, lq2, lk2, gh, lam_init):
    b, _, l, _ = q.shape
    hspec = pl.BlockSpec((None, H_A, l, DV_A), lambda i: (i, 0, 0, 0))
    cspec = pl.BlockSpec((None,) + cache_k.shape[1:], lambda i: (b_off + i, 0, 0))
    small = lambda shape: pl.BlockSpec(shape, lambda i: (0, 0))
    return pl.pallas_call(
        functools.partial(_attn_sample_kernel, lam_init=lam_init),
        grid=(b,),
        in_specs=[hspec, hspec, hspec, cspec, cspec, small((1, DH_A)), small((1, DH_A)),
                  small((1, DH_A)), small((1, DH_A)), small((1, DV_A))],
        out_specs=pl.BlockSpec((None, l, H_A * DV_A), lambda i: (i, 0, 0)),
        out_shape=jax.ShapeDtypeStruct((b, l, H_A * DV_A), F32),
        compiler_params=pltpu.CompilerParams(
            dimension_semantics=("arbitrary",), vmem_limit_bytes=V7X_VMEM_LIMIT),
        name="attn_sample",
    )(q, kn, vn, cache_k, cache_v, lq1, lk1, lq2, lk2, gh)


def _split3(x):
    hi = x.astype(BF16)
    r1 = x - hi.astype(F32)
    mid = r1.astype(BF16)
    lo = (r1 - mid.astype(F32)).astype(BF16)
    return hi, mid, lo


def _mlstm_kernel(q_ref, k_ref, v_ref, gz_ref, bias_ref, gh_ref, c0_ref, n0_ref, m0_ref,
                  h_ref, c_ref, n_ref, m_ref, c_s, n_s, m_s, *, t):
    c = pl.program_id(1)

    @pl.when(c == 0)
    def _load_state():
        c_s[...] = c0_ref[...]
        n_s[...] = n0_ref[...]
        m_s[...] = m0_ref[...]

    gz = gz_ref[...] + bias_ref[...]
    lane = lax.broadcasted_iota(jnp.int32, gz.shape, 1)
    lf = jnp.minimum(gz, 0.0) - jnp.log1p(jnp.exp(-jnp.abs(gz)))
    lf = jnp.where((lane >= H_M) & (lane < 2 * H_M), lf, 0.0)
    row = lax.broadcasted_iota(jnp.int32, (t, t), 0)
    col = lax.broadcasted_iota(jnp.int32, (t, t), 1)
    causal = col <= row
    tril = jnp.where(causal, 1.0, 0.0).astype(BF16)
    bcum = sum(jnp.dot(tril, piece, preferred_element_type=F32) for piece in _split3(lf))
    gc = jnp.where(lane < H_M, gz, bcum)
    sel = jnp.where(lax.broadcasted_iota(jnp.int32, (8, GATE_LANES), 0)
                    == lax.broadcasted_iota(jnp.int32, (8, GATE_LANES), 1), 1.0, 0.0).astype(BF16)
    gr = sum(lax.dot_general(sel, piece, NT_DIMS, preferred_element_type=F32)
             for piece in _split3(gc))

    for h in range(H_M):
        sl = slice(h * DH_M, (h + 1) * DH_M)
        ig_c, b_c = gc[:, h:h + 1], gc[:, H_M + h:H_M + h + 1]
        ig_r, b_r = gr[h:h + 1, :], gr[H_M + h:H_M + h + 1, :]
        m_prev = m_s[:, h:h + 1]
        c_prev = c_s[h]
        n_prev = n_s[h:h + 1, :]
        q = q_ref[:, sl] * (DH_M ** -0.5)
        k = k_ref[:, sl]
        v = v_ref[:, sl]

        d = jnp.where(causal, b_c - b_r + ig_r, -jnp.inf)
        inter = b_c + m_prev
        m_t = jnp.maximum(inter, jnp.max(d, axis=-1, keepdims=True))
        w_intra = jnp.exp(d - m_t)
        w_inter = jnp.exp(inter - m_t)
        s = lax.dot_general(q, k, NT_DIMS, preferred_element_type=F32) * w_intra
        num = (jnp.dot(s.astype(BF16), v, preferred_element_type=F32)
               + w_inter * lax.dot_general(q, c_prev.astype(BF16), NT_DIMS,
                                           preferred_element_type=F32))
        den = (jnp.sum(s, axis=-1, keepdims=True)
               + w_inter * jnp.sum(q.astype(F32) * n_prev, axis=-1, keepdims=True))
        hh = num / jnp.maximum(jnp.abs(den), jnp.exp(-m_t))
        h_ref[:, sl] = _rms(hh, gh_ref[h:h + 1, :])

        g_last = b_c[t - 1:t, :]
        logw = g_last - b_c + ig_c
        m_new = jnp.maximum(g_last + m_prev, jnp.max(logw, axis=0, keepdims=True))
        ws = jnp.exp(logw - m_new)
        wc = jnp.exp(g_last + m_prev - m_new)
        vw = (v.astype(F32) * ws).astype(BF16)
        c_s[h] = wc * c_prev + lax.dot_general(vw, k, TN_DIMS, preferred_element_type=F32)
        n_s[h:h + 1, :] = wc * n_prev + jnp.sum(k.astype(F32) * ws, axis=0, keepdims=True)
        m_s[:, h:h + 1] = m_new

    @pl.when(c == pl.num_programs(1) - 1)
    def _store_state():
        c_ref[...] = c_s[...]
        n_ref[...] = n_s[...]
        m_ref[...] = m_s[...]


def _mlstm(q, k, v, gz, bias, gh, c0, n0, m0):
    b, l, _ = q.shape
    t = min(MLSTM_CHUNK, l)
    nc = l // t
    seq = lambda width: pl.BlockSpec((None, t, width), lambda i, j: (i, j, 0))
    const = lambda shape: pl.BlockSpec(shape, lambda i, j: (0,) * len(shape))
    cspec = pl.BlockSpec((None, H_M, DH_M, DH_M), lambda i, j: (i, 0, 0, 0))
    nspec = pl.BlockSpec((None, H_M, DH_M), lambda i, j: (i, 0, 0))
    mspec = pl.BlockSpec((None, 1, H_M), lambda i, j: (i, 0, 0))
    return pl.pallas_call(
        functools.partial(_mlstm_kernel, t=t),
        grid=(b, nc),
        in_specs=[seq(D_MODEL), seq(D_MODEL), seq(D_MODEL), seq(GATE_LANES),
                  const((1, GATE_LANES)), const((H_M, DH_M)), cspec, nspec, mspec],
        out_specs=[seq(D_MODEL), cspec, nspec, mspec],
        out_shape=[jax.ShapeDtypeStruct((b, l, D_MODEL), F32),
                   jax.ShapeDtypeStruct((b, H_M, DH_M, DH_M), F32),
                   jax.ShapeDtypeStruct((b, H_M, DH_M), F32),
                   jax.ShapeDtypeStruct((b, 1, H_M), F32)],
        scratch_shapes=[pltpu.VMEM((H_M, DH_M, DH_M), F32),
                        pltpu.VMEM((H_M, DH_M), F32),
                        pltpu.VMEM((1, H_M), F32)],
        compiler_params=pltpu.CompilerParams(
            dimension_semantics=("arbitrary", "arbitrary"), vmem_limit_bytes=V7X_VMEM_LIMIT),
        name="mlstm",
    )(q, k, v, gz, bias, gh, c0, n0, m0)


def _gelu_tanh(x):
    return 0.5 * x * (1.0 + jnp.tanh(math.sqrt(2.0 / math.pi) * (x + 0.044715 * (x * x * x))))


def _ffn_kernel(x_ref, attn_ref, hm_ref, om_ref, ga_ref, gm_ref, conv0_ref, wout_ref, wup_ref,
                wdn_ref, gpm_ref, gpf_ref, gpo_ref, cw_ref, cb_ref, y_ref, cs_ref, g_s,
                *, nb, l, tiles_per_batch):
    i = pl.program_id(0)
    sig = jax.nn.sigmoid
    merged = (sig(ga_ref[...]) * attn_ref[...]
              + sig(gm_ref[...]) * (sig(om_ref[...]) * hm_ref[...]))
    x1 = x_ref[...] + _rms(jnp.dot(merged.astype(BF16), wout_ref[...],
                                   preferred_element_type=F32), gpm_ref[...])
    h2 = _rms(x1, gpf_ref[...]).astype(BF16)
    u = jnp.dot(h2, wup_ref[:, :D_FF], preferred_element_type=F32)
    g = jnp.dot(h2, wup_ref[:, D_FF:], preferred_element_type=F32)

    @pl.when(i % tiles_per_batch == 0)
    def _from_state():
        g_s[:, 6:8, :] = conv0_ref[...]

    @pl.when(i % tiles_per_batch != 0)
    def _from_prev_tile():
        g_s[:, 6:8, :] = g_s[:, l + 6:l + 8, :]

    g_s[:, 8:8 + l, :] = g.reshape(nb, l, D_FF)
    cs_ref[...] = g_s[:, l + 6:l + 8, :]
    gconv = (cb_ref[...] + cw_ref[0:1, :] * g_s[:, 6:6 + l, :] + cw_ref[1:2, :] * g_s[:, 7:7 + l, :]
             + cw_ref[2:3, :] * g_s[:, 8:8 + l, :]).reshape(nb * l, D_FF)
    ff = jnp.dot((_gelu_tanh(gconv) * u).astype(BF16), wdn_ref[...], preferred_element_type=F32)
    y_ref[...] = x1 + _rms(ff, gpo_ref[...])


def _ffn(x2d, attn, hm, om, ga, gm, conv0, wout, wup, wdn, gpm, gpf, gpo, cw, cb, nb_total, seq):
    n = x2d.shape[0]
    if seq >= FFN_ROWS:
        tm, nb, l = FFN_ROWS, 1, FFN_ROWS
        tiles_per_batch = seq // tm
    else:
        tm, nb, l = n, nb_total, seq
        tiles_per_batch = 1
    row = pl.BlockSpec((tm, D_MODEL), lambda i: (i, 0))
    cstate = pl.BlockSpec((nb, CONV_W - 1, D_FF), lambda i: (i // tiles_per_batch, 0, 0))
    return pl.pallas_call(
        functools.partial(_ffn_kernel, nb=nb, l=l, tiles_per_batch=tiles_per_batch),
        grid=(n // tm,),
        in_specs=[row, row, row, row, row, row, cstate,
                  _resident((D_MODEL, D_MODEL)), _resident((D_MODEL, 2 * D_FF)),
                  _resident((D_FF, D_MODEL)), _resident((1, D_MODEL)), _resident((1, D_MODEL)),
                  _resident((1, D_MODEL)), _resident((CONV_W, D_FF)), _resident((1, D_FF))],
        out_specs=[row, cstate],
        out_shape=[jax.ShapeDtypeStruct((n, D_MODEL), F32),
                   jax.ShapeDtypeStruct((nb_total, CONV_W - 1, D_FF), F32)],
        scratch_shapes=[pltpu.VMEM((nb, l + 8, D_FF), F32)],
        compiler_params=pltpu.CompilerParams(
            dimension_semantics=("arbitrary",), vmem_limit_bytes=V7X_VMEM_LIMIT),
        name="merge_ffn",
    )(x2d, attn, hm, om, ga, gm, conv0, wout, wup, wdn, gpm, gpf, gpo, cw, cb)


def _layer(x, caches, c0, n0, m0, conv0, wts, lam_init):
    (g_pre_mix, w_cat, gate_bias, lq1, lk1, lq2, lk2, g_attn_head, g_mlstm_head, wout, g_post_mix,
     g_pre_ffn, wup, conv_w, conv_b, wdn, g_post_ffn) = wts
    b, l, _ = x.shape
    x2d = x.reshape(b * l, D_MODEL)
    (qa, kf, kb, vf, vb, qm, km, vm, om, ga, gm, gz) = _inproj(x2d, g_pre_mix, w_cat, b, l)

    if caches is None:
        attn = _attn_prompt(qa, kb, vb, lq1, lk1, lq2, lk2, g_attn_head, lam_init)
    else:
        cache_k, cache_v, b_off = caches
        attn = _attn_sample(qa, kb, vb, cache_k, cache_v, b_off,
                            lq1, lk1, lq2, lk2, g_attn_head, lam_init)

    seq3 = lambda a: a.reshape(b, l, a.shape[-1])
    hm, c1, n1, m1 = _mlstm(seq3(qm), seq3(km), seq3(vm), seq3(gz), gate_bias, g_mlstm_head,
                            c0, n0, m0.reshape(b, 1, H_M))

    y, conv1 = _ffn(x2d, attn.reshape(b * l, D_MODEL), hm.reshape(b * l, D_MODEL), om, ga, gm,
                    conv0, wout, wup, wdn, g_post_mix, g_pre_ffn, g_post_ffn, conv_w, conv_b, b, l)
    return (y.reshape(b, l, D_MODEL), kf.reshape(b, l, H_A, DV_A), vf.reshape(b, l, H_A, DV_A),
            c1, n1, m1.reshape(b, H_M), conv1)


def kernel(x_prompt, x_sample, cache_k, cache_v, state_C, state_n, state_m, state_conv, g_pre_mix, w_in, b_gates, lam_q1, lam_k1, lam_q2, lam_k2, g_attn_head, g_mlstm_head, w_out, g_post_mix, g_pre_ffn, w_up, conv_w, conv_b, w_down, g_post_ffn):
    depth = w_in.shape[0]
    bp = x_prompt.shape[0]
    yp, ys = x_prompt, x_sample
    outs_p, outs_s = [], []
    n_main = 6 * D_MODEL
    for li in range(depth):
        lam_init = 0.8 - 0.6 * math.exp(-0.3 * li)
        w = w_in[li]
        w_cat = jnp.concatenate(
            [w[:, :n_main], w[:, n_main + 2 * H_M:], w[:, n_main:n_main + 2 * H_M],
             jnp.zeros((D_MODEL, GATE_LANES - 2 * H_M), w.dtype)], axis=1).astype(BF16)
        gate_bias = jnp.concatenate(
            [b_gates[li], jnp.zeros((GATE_LANES - 2 * H_M,), F32)]).reshape(1, GATE_LANES)
        row = lambda a: a.reshape(1, -1)
        wts = (row(g_pre_mix[li]), w_cat, gate_bias, row(lam_q1[li]), row(lam_k1[li]),
               row(lam_q2[li]), row(lam_k2[li]), row(g_attn_head[li]), g_mlstm_head[li],
               w_out[li].astype(BF16), row(g_post_mix[li]), row(g_pre_ffn[li]),
               w_up[li].astype(BF16), conv_w[li], row(conv_b[li]), w_down[li].astype(BF16),
               row(g_post_ffn[li]))
        c0 = jnp.zeros((bp, H_M, DH_M, DH_M), F32)
        n0 = jnp.zeros((bp, H_M, DH_M), F32)
        m0 = jnp.zeros((bp, H_M), F32)
        conv0 = jnp.zeros((bp, CONV_W - 1, D_FF), F32)
        yp, *sp = _layer(yp, None, c0, n0, m0, conv0, wts, lam_init)
        bs, past = cache_k.shape[1], cache_k.shape[2]
        caches = (cache_k.reshape(depth * bs, past * H_A, DV_A),
                  cache_v.reshape(depth * bs, past * H_A, DV_A), li * bs)
        ys, *ss = _layer(ys, caches, state_C[li], state_n[li], state_m[li],
                         state_conv[li], wts, lam_init)
        outs_p.append(sp)
        outs_s.append(ss)
    k_p, v_p, c_p, n_p, m_p, conv_p = [jnp.stack([o[i] for o in outs_p]) for i in range(6)]
    k_s, v_s, c_s, n_s, m_s, conv_s = [jnp.stack([o[i] for o in outs_s]) for i in range(6)]
    return (yp, ys, k_p, v_p, c_p, n_p, m_p, conv_p, k_s, v_s, c_s, n_s, m_s, conv_s)
```

```python
import functools
import math

import jax
import jax.numpy as jnp
import numpy as np
from jax import lax
from jax.experimental import pallas as pl
from jax.experimental.pallas import tpu as pltpu

F32 = jnp.float32
BF16 = jnp.bfloat16

D_MODEL = 1024
CHUNK = 64
H_A = 8
DH_A = 64
DV_A = 2 * DH_A
H_M = 4
DH_M = D_MODEL // H_M
D_FF = 2816
CONV_W = 3
EPS = 1e-6
N_SEG = 9
GATE_LANES = 128
W_CAT = N_SEG * D_MODEL + GATE_LANES
QK_SCALE = DH_A ** -0.5 * math.log2(math.e)

V7X_VMEM_LIMIT = 56 * 1024 * 1024

PROJ_ROWS = 256
ATTN_TILE = 512
MLSTM_CHUNK = 256
FFN_ROWS = 256

NT_DIMS = (((1,), (1,)), ((), ()))
TN_DIMS = (((0,), (0,)), ((), ()))


def _rms(x, g):
    return x * lax.rsqrt(jnp.mean(x * x, axis=-1, keepdims=True) + EPS) * g


def _resident(shape):
    nd = len(shape)
    return pl.BlockSpec(shape, lambda *_: (0,) * nd, pipeline_mode=pl.Buffered(1))


def _inproj_kernel(x_ref, g_ref, w_ref, qa_ref, kf_ref, kb_ref, vf_ref, vb_ref,
                   qm_ref, km_ref, vm_ref, om_ref, ga_ref, gm_ref, gz_ref, *, nbt, lt, transposed):
    hb = _rms(x_ref[...], g_ref[...]).astype(BF16)

    def seg(j, width=D_MODEL):
        return jnp.dot(hb, w_ref[:, j * D_MODEL:j * D_MODEL + width],
                       preferred_element_type=F32)

    def put_heads(ref, val, transpose):
        for h in range(H_A):
            piece = val[:, h * DV_A:(h + 1) * DV_A]
            if transpose:
                ref[0, h] = piece.T.astype(BF16)
            else:
                ref[:, h] = piece.astype(BF16).reshape(nbt, lt, DV_A)

    put_heads(qa_ref, seg(0) * QK_SCALE, transposed)
    k = seg(1)
    kf_ref[...] = k
    put_heads(kb_ref, k, False)
    v = seg(2)
    vf_ref[...] = v
    put_heads(vb_ref, v, transposed)
    qm_ref[...] = seg(3).astype(BF16)
    km_ref[...] = seg(4).astype(BF16)
    vm_ref[...] = seg(5).astype(BF16)
    om_ref[...] = seg(6)
    ga_ref[...] = seg(7)
    gm_ref[...] = seg(8)
    gz_ref[...] = seg(9, GATE_LANES)


def _inproj(x2d, g, w_cat, nb, seq):
    n = x2d.shape[0]
    tm = PROJ_ROWS
    transposed = seq >= tm
    if transposed:
        nbt, lt = 1, tm
        per_b = seq // tm
        head = pl.BlockSpec((1, H_A, lt, DV_A), lambda i: (i // per_b, 0, i % per_b, 0))
        head_t = pl.BlockSpec((1, H_A, DV_A, lt), lambda i: (i // per_b, 0, 0, i % per_b))
        hm16_t = jax.ShapeDtypeStruct((nb, H_A, DV_A, seq), BF16)
    else:
        nbt, lt = tm // seq, seq
        head = head_t = pl.BlockSpec((nbt, H_A, lt, DV_A), lambda i: (i, 0, 0, 0))
        hm16_t = jax.ShapeDtypeStruct((nb, H_A, seq, DV_A), BF16)
    row = pl.BlockSpec((tm, D_MODEL), lambda i: (i, 0))
    full32 = jax.ShapeDtypeStruct((n, D_MODEL), F32)
    full16 = jax.ShapeDtypeStruct((n, D_MODEL), BF16)
    hm16 = jax.ShapeDtypeStruct((nb, H_A, seq, DV_A), BF16)
    return pl.pallas_call(
        functools.partial(_inproj_kernel, nbt=nbt, lt=lt, transposed=transposed),
        grid=(n // tm,),
        in_specs=[row, _resident((1, D_MODEL)), _resident((D_MODEL, W_CAT))],
        out_specs=[head_t, row, head, row, head_t, row, row, row, row, row, row,
                   pl.BlockSpec((tm, GATE_LANES), lambda i: (i, 0))],
        out_shape=[hm16_t, full32, hm16, full32, hm16_t, full16, full16, full16,
                   full32, full32, full32, jax.ShapeDtypeStruct((n, GATE_LANES), F32)],
        compiler_params=pltpu.CompilerParams(
            dimension_semantics=("arbitrary",), vmem_limit_bytes=V7X_VMEM_LIMIT),
        name="inproj",
    )(x2d, g, w_cat)


def _lam(lq1, lk1, lq2, lk2, lam_init):
    return (jnp.exp(jnp.sum(lq1 * lk1, axis=-1, keepdims=True))
            - jnp.exp(jnp.sum(lq2 * lk2, axis=-1, keepdims=True)) + lam_init)


def _split_maps(q, axis):
    idx = lax.broadcasted_iota(jnp.int32, q.shape, axis)
    zero = jnp.zeros_like(q)
    return jnp.where(idx < DH_A, q, zero), jnp.where(idx >= DH_A, q, zero)


def _attn_prompt_kernel(qi_tab, kj_tab, qt_ref, k_ref, vt_ref, lq1_ref, lk1_ref, lq2_ref, lk2_ref,
                        gcol_ref, o_ref, qz_s, m_s, l_s, acc_s, *, lam_init):
    p = pl.program_id(1)
    qi = qi_tab[p]
    kj = kj_tab[p]
    t = ATTN_TILE

    @pl.when(kj == 0)
    def _init():
        for h in range(H_A):
            q1, q2 = _split_maps(qt_ref[h], 0)
            qz_s[0, h] = q1
            qz_s[1, h] = q2
        m_s[...] = jnp.full(m_s.shape, -jnp.inf, F32)
        l_s[...] = jnp.zeros(l_s.shape, F32)
        acc_s[...] = jnp.zeros(acc_s.shape, F32)

    def sweep(masked):
        if masked:
            kr = lax.broadcasted_iota(jnp.int32, (t, t), 0) // CHUNK
            qc = lax.broadcasted_iota(jnp.int32, (t, t), 1) // CHUNK
            visible = kr <= qc

        def scores(h, mp):
            s = jnp.dot(k_ref[h], qz_s[mp, h], preferred_element_type=F32)
            return jnp.where(visible, s, -jnp.inf) if masked else s

        pairs = [(h, mp) for h in range(H_A) for mp in range(2)]
        s_next = scores(*pairs[0])
        for i, (h, mp) in enumerate(pairs):
            s = s_next
            m_old = m_s[mp, h]
            m_new = jnp.maximum(m_old, jnp.max(s, axis=0, keepdims=True))
            if i + 1 < len(pairs):
                s_next = scores(*pairs[i + 1])
            alpha = jnp.exp2(m_old - m_new)
            pr = jnp.exp2(s - m_new)
            l_s[mp, h] = alpha * l_s[mp, h] + jnp.sum(pr, axis=0, keepdims=True)
            acc_s[mp, h] = alpha * acc_s[mp, h] + jnp.dot(
                vt_ref[h], pr.astype(BF16), preferred_element_type=F32)
            m_s[mp, h] = m_new

    @pl.when(kj < qi)
    def _full():
        sweep(False)

    @pl.when(kj == qi)
    def _diag():
        sweep(True)
        lam = _lam(lq1_ref[...], lk1_ref[...], lq2_ref[...], lk2_ref[...], lam_init)
        for h in range(H_A):
            ot = acc_s[0, h] / l_s[0, h] - lam * (acc_s[1, h] / l_s[1, h])
            ms = jnp.mean(ot * ot, axis=0, keepdims=True)
            ot = ot * lax.rsqrt(ms + EPS) * gcol_ref[...] * (1.0 - lam_init)
            o_ref[:, h * DV_A:(h + 1) * DV_A] = ot.T


def _attn_prompt(qt, k, vt, lq1, lk1, lq2, lk2, gh, lam_init):
    b, _, s, _ = k.shape
    t = ATTN_TILE
    nq = s // t
    pairs = [(i, j) for i in range(nq) for j in range(i + 1)]
    qi_tab = jnp.asarray(np.array([a for a, _ in pairs], np.int32))
    kj_tab = jnp.asarray(np.array([c for _, c in pairs], np.int32))
    qspec = pl.BlockSpec((None, H_A, DV_A, t), lambda bb, p, qi, kj: (bb, 0, 0, qi[p]))
    kspec = pl.BlockSpec((None, H_A, t, DV_A), lambda bb, p, qi, kj: (bb, 0, kj[p], 0))
    vspec = pl.BlockSpec((None, H_A, DV_A, t), lambda bb, p, qi, kj: (bb, 0, 0, kj[p]))
    small = lambda shape: pl.BlockSpec(shape, lambda bb, p, qi, kj: (0, 0))
    grid_spec = pltpu.PrefetchScalarGridSpec(
        num_scalar_prefetch=2,
        grid=(b, len(pairs)),
        in_specs=[qspec, kspec, vspec, small((1, DH_A)), small((1, DH_A)), small((1, DH_A)),
                  small((1, DH_A)), small((DV_A, 1))],
        out_specs=pl.BlockSpec((None, t, H_A * DV_A), lambda bb, p, qi, kj: (bb, qi[p], 0)),
        scratch_shapes=[pltpu.VMEM((2, H_A, DV_A, t), BF16),
                        pltpu.VMEM((2, H_A, 1, t), F32),
                        pltpu.VMEM((2, H_A, 1, t), F32),
                        pltpu.VMEM((2, H_A, DV_A, t), F32)],
    )
    return pl.pallas_call(
        functools.partial(_attn_prompt_kernel, lam_init=lam_init),
        grid_spec=grid_spec,
        out_shape=jax.ShapeDtypeStruct((b, s, H_A * DV_A), F32),
        compiler_params=pltpu.CompilerParams(
            dimension_semantics=("arbitrary", "arbitrary"), vmem_limit_bytes=V7X_VMEM_LIMIT),
        name="attn_prompt",
    )(qi_tab, kj_tab, qt, k, vt, lq1, lk1, lq2, lk2, gh.reshape(DV_A, 1))


def _attn_sample_kernel(q_ref, kn_ref, vn_ref, ck_ref, cv_ref, lq1_ref, lk1_ref, lq2_ref, lk2_ref,
                        gh_ref, o_ref, *, lam_init):
    lam = _lam(lq1_ref[...], lk1_ref[...], lq2_ref[...], lk2_ref[...], lam_init)
    nq = q_ref.shape[1]
    past = ck_ref.shape[0] // H_A
    for h in range(H_A):
        sl = slice(h * DV_A, (h + 1) * DV_A)
        rows = pl.ds(h, past, stride=H_A)
        q1, q2 = _split_maps(q_ref[h], 1)
        qz = jnp.concatenate([q1, q2], axis=0)
        kc = ck_ref[rows, :].astype(BF16)
        sc = lax.dot_general(qz, kc, NT_DIMS, preferred_element_type=F32)
        sn = lax.dot_general(qz, kn_ref[h], NT_DIMS, preferred_element_type=F32)
        m = jnp.maximum(jnp.max(sc, axis=-1, keepdims=True), jnp.max(sn, axis=-1, keepdims=True))
        pc = jnp.exp2(sc - m)
        pn = jnp.exp2(sn - m)
        inv = 1.0 / (jnp.sum(pc, axis=-1, keepdims=True) + jnp.sum(pn, axis=-1, keepdims=True))
        pc = pc * inv
        pn = pn * inv
        ac = pc[:nq] - lam * pc[nq:]
        an = pn[:nq] - lam * pn[nq:]
        o = (jnp.dot(ac.astype(BF16), cv_ref[rows, :].astype(BF16), preferred_element_type=F32)
             + jnp.dot(an.astype(BF16), vn_ref[h], preferred_element_type=F32))
        o_ref[:, sl] = _rms(o, gh_ref[...]) * (1.0 - lam_init)


def _attn_sample(q, kn, vn, cache_k, cache_v, b_off, lq1, lk1, lq2, lk2, gh, lam_init):
    b, _, l, _ = q.shape
    hspec = pl.BlockSpec((None, H_A, l, DV_A), lambda i: (i, 0, 0, 0))
    cspec = pl.BlockSpec((None,) + cache_k.shape[1:], lambda i: (b_off + i, 0, 0))
    small = lambda shape: pl.BlockSpec(shape, lambda i: (0, 0))
    return pl.pallas_call(
        functools.partial(_attn_sample_kernel, lam_init=lam_init),
        grid=(b,),
        in_specs=[hspec, hspec, hspec, cspec, cspec, small((1, DH_A)), small((1, DH_A)),
                  small((1, DH_A)), small((1, DH_A)), small((1, DV_A))],
        out_specs=pl.BlockSpec((None, l, H_A * DV_A), lambda i: (i, 0, 0)),
        out_shape=jax.ShapeDtypeStruct((b, l, H_A * DV_A), F32),
        compiler_params=pltpu.CompilerParams(
            dimension_semantics=("arbitrary",), vmem_limit_bytes=V7X_VMEM_LIMIT),
        name="attn_sample",
    )(q, kn, vn, cache_k, cache_v, lq1, lk1, lq2, lk2, gh)


def _split3(x):
    hi = x.astype(BF16)
    r1 = x - hi.astype(F32)
    mid = r1.astype(BF16)
    lo = (r1 - mid.astype(F32)).astype(BF16)
    return hi, mid, lo


def _mlstm_kernel(q_ref, k_ref, v_ref, gz_ref, bias_ref, gh_ref, c0_ref, n0_ref, m0_ref,
                  h_ref, c_ref, n_ref, m_ref, c_s, n_s, m_s, *, t):
    c = pl.program_id(1)

    @pl.when(c == 0)
    def _load_state():
        c_s[...] = c0_ref[...]
        n_s[...] = n0_ref[...]
        m_s[...] = m0_ref[...]

    gz = gz_ref[...] + bias_ref[...]
    lane = lax.broadcasted_iota(jnp.int32, gz.shape, 1)
    lf = jnp.minimum(gz, 0.0) - jnp.log1p(jnp.exp(-jnp.abs(gz)))
    lf = jnp.where((lane >= H_M) & (lane < 2 * H_M), lf, 0.0)
    row = lax.broadcasted_iota(jnp.int32, (t, t), 0)
    col = lax.broadcasted_iota(jnp.int32, (t, t), 1)
    causal = col <= row
    tril = jnp.where(causal, 1.0, 0.0).astype(BF16)
    bcum = sum(jnp.dot(tril, piece, preferred_element_type=F32) for piece in _split3(lf))
    gc = jnp.where(lane < H_M, gz, bcum)
    sel = jnp.where(lax.broadcasted_iota(jnp.int32, (8, GATE_LANES), 0)
                    == lax.broadcasted_iota(jnp.int32, (8, GATE_LANES), 1), 1.0, 0.0).astype(BF16)
    gr = sum(lax.dot_general(sel, piece, NT_DIMS, preferred_element_type=F32)
             for piece in _split3(gc))

    for h in range(H_M):
        sl = slice(h * DH_M, (h + 1) * DH_M)
        ig_c, b_c = gc[:, h:h + 1], gc[:, H_M + h:H_M + h + 1]
        ig_r, b_r = gr[h:h + 1, :], gr[H_M + h:H_M + h + 1, :]
        m_prev = m_s[:, h:h + 1]
        c_prev = c_s[h]
        n_prev = n_s[h:h + 1, :]
        q = q_ref[:, sl] * (DH_M ** -0.5)
        k = k_ref[:, sl]
        v = v_ref[:, sl]

        d = jnp.where(causal, b_c - b_r + ig_r, -jnp.inf)
        inter = b_c + m_prev
        m_t = jnp.maximum(inter, jnp.max(d, axis=-1, keepdims=True))
        w_intra = jnp.exp(d - m_t)
        w_inter = jnp.exp(inter - m_t)
        s = lax.dot_general(q, k, NT_DIMS, preferred_element_type=F32) * w_intra
        num = (jnp.dot(s.astype(BF16), v, preferred_element_type=F32)
               + w_inter * lax.dot_general(q, c_prev.astype(BF16), NT_DIMS,
                                           preferred_element_type=F32))
        den = (jnp.sum(s, axis=-1, keepdims=True)
               + w_inter * jnp.sum(q.astype(F32) * n_prev, axis=-1, keepdims=True))
        hh = num / jnp.maximum(jnp.abs(den), jnp.exp(-m_t))
        h_ref[:, sl] = _rms(hh, gh_ref[h:h + 1, :])

        g_last = b_c[t - 1:t, :]
        logw = g_last - b_c + ig_c
        m_new = jnp.maximum(g_last + m_prev, jnp.max(logw, axis=0, keepdims=True))
        ws = jnp.exp(logw - m_new)
        wc = jnp.exp(g_last + m_prev - m_new)
        vw = (v.astype(F32) * ws).astype(BF16)
        c_s[h] = wc * c_prev + lax.dot_general(vw, k, TN_DIMS, preferred_element_type=F32)
        n_s[h:h + 1, :] = wc * n_prev + jnp.sum(k.astype(F32) * ws, axis=0, keepdims=True)
        m_s[:, h:h + 1] = m_new

    @pl.when(c == pl.num_programs(1) - 1)
    def _store_state():
        c_ref[...] = c_s[...]
        n_ref[...] = n_s[...]
        m_ref[...] = m_s[...]


def _mlstm(q, k, v, gz, bias, gh, c0, n0, m0):
    b, l, _ = q.shape
    t = min(MLSTM_CHUNK, l)
    nc = l // t
    seq = lambda width: pl.BlockSpec((None, t, width), lambda i, j: (i, j, 0))
    const = lambda shape: pl.BlockSpec(shape, lambda i, j: (0,) * len(shape))
    cspec = pl.BlockSpec((None, H_M, DH_M, DH_M), lambda i, j: (i, 0, 0, 0))
    nspec = pl.BlockSpec((None, H_M, DH_M), lambda i, j: (i, 0, 0))
    mspec = pl.BlockSpec((None, 1, H_M), lambda i, j: (i, 0, 0))
    return pl.pallas_call(
        functools.partial(_mlstm_kernel, t=t),
        grid=(b, nc),
        in_specs=[seq(D_MODEL), seq(D_MODEL), seq(D_MODEL), seq(GATE_LANES),
                  const((1, GATE_LANES)), const((H_M, DH_M)), cspec, nspec, mspec],
        out_specs=[seq(D_MODEL), cspec, nspec, mspec],
        out_shape=[jax.ShapeDtypeStruct((b, l, D_MODEL), F32),
                   jax.ShapeDtypeStruct((b, H_M, DH_M, DH_M), F32),
                   jax.ShapeDtypeStruct((b, H_M, DH_M), F32),
                   jax.ShapeDtypeStruct((b, 1, H_M), F32)],
        scratch_shapes=[pltpu.VMEM((H_M, DH_M, DH_M), F32),
                        pltpu.VMEM((H_M, DH_M), F32),
                        pltpu.VMEM((1, H_M), F32)],
        compiler_params=pltpu.CompilerParams(
            dimension_semantics=("arbitrary", "arbitrary"), vmem_limit_bytes=V7X_VMEM_LIMIT),
        name="mlstm",
    )(q, k, v, gz, bias, gh, c0, n0, m0)


def _gelu_tanh(x):
    return 0.5 * x * (1.0 + jnp.tanh(math.sqrt(2.0 / math.pi) * (x + 0.044715 * (x * x * x))))


def _ffn_kernel(x_ref, attn_ref, hm_ref, om_ref, ga_ref, gm_ref, conv0_ref, wout_ref, wup_ref,
                wdn_ref, gpm_ref, gpf_ref, gpo_ref, cw_ref, cb_ref, y_ref, cs_ref, g_s,
                *, nb, l, tiles_per_batch):
    i = pl.program_id(0)
    sig = jax.nn.sigmoid
    merged = (sig(ga_ref[...]) * attn_ref[...]
              + sig(gm_ref[...]) * (sig(om_ref[...]) * hm_ref[...]))
    x1 = x_ref[...] + _rms(jnp.dot(merged.astype(BF16), wout_ref[...],
                                   preferred_element_type=F32), gpm_ref[...])
    h2 = _rms(x1, gpf_ref[...]).astype(BF16)
    u = jnp.dot(h2, wup_ref[:, :D_FF], preferred_element_type=F32)
    g = jnp.dot(h2, wup_ref[:, D_FF:], preferred_element_type=F32)

    @pl.when(i % tiles_per_batch == 0)
    def _from_state():
        g_s[:, 6:8, :] = conv0_ref[...]

    @pl.when(i % tiles_per_batch != 0)
    def _from_prev_tile():
        g_s[:, 6:8, :] = g_s[:, l + 6:l + 8, :]

    g_s[:, 8:8 + l, :] = g.reshape(nb, l, D_FF)
    cs_ref[...] = g_s[:, l + 6:l + 8, :]
    gconv = (cb_ref[...] + cw_ref[0:1, :] * g_s[:, 6:6 + l, :] + cw_ref[1:2, :] * g_s[:, 7:7 + l, :]
             + cw_ref[2:3, :] * g_s[:, 8:8 + l, :]).reshape(nb * l, D_FF)
    ff = jnp.dot((_gelu_tanh(gconv) * u).astype(BF16), wdn_ref[...], preferred_element_type=F32)
    y_ref[...] = x1 + _rms(ff, gpo_ref[...])


def _ffn(x2d, attn, hm, om, ga, gm, conv0, wout, wup, wdn, gpm, gpf, gpo, cw, cb, nb_total, seq):
    n = x2d.shape[0]
    if seq >= FFN_ROWS:
        tm, nb, l = FFN_ROWS, 1, FFN_ROWS
        tiles_per_batch = seq // tm
    else:
        tm, nb, l = n, nb_total, seq
        tiles_per_batch = 1
    row = pl.BlockSpec((tm, D_MODEL), lambda i: (i, 0))
    cstate = pl.BlockSpec((nb, CONV_W - 1, D_FF), lambda i: (i // tiles_per_batch, 0, 0))
    return pl.pallas_call(
        functools.partial(_ffn_kernel, nb=nb, l=l, tiles_per_batch=tiles_per_batch),
        grid=(n // tm,),
        in_specs=[row, row, row, row, row, row, cstate,
                  _resident((D_MODEL, D_MODEL)), _resident((D_MODEL, 2 * D_FF)),
                  _resident((D_FF, D_MODEL)), _resident((1, D_MODEL)), _resident((1, D_MODEL)),
                  _resident((1, D_MODEL)), _resident((CONV_W, D_FF)), _resident((1, D_FF))],
        out_specs=[row, cstate],
        out_shape=[jax.ShapeDtypeStruct((n, D_MODEL), F32),
                   jax.ShapeDtypeStruct((nb_total, CONV_W - 1, D_FF), F32)],
        scratch_shapes=[pltpu.VMEM((nb, l + 8, D_FF), F32)],
        compiler_params=pltpu.CompilerParams(
            dimension_semantics=("arbitrary",), vmem_limit_bytes=V7X_VMEM_LIMIT),
        name="merge_ffn",
    )(x2d, attn, hm, om, ga, gm, conv0, wout, wup, wdn, gpm, gpf, gpo, cw, cb)


def _layer(x, caches, c0, n0, m0, conv0, wts, lam_init):
    (g_pre_mix, w_cat, gate_bias, lq1, lk1, lq2, lk2, g_attn_head, g_mlstm_head, wout, g_post_mix,
     g_pre_ffn, wup, conv_w, conv_b, wdn, g_post_ffn) = wts
    b, l, _ = x.shape
    x2d = x.reshape(b * l, D_MODEL)
    (qa, kf, kb, vf, vb, qm, km, vm, om, ga, gm, gz) = _inproj(x2d, g_pre_mix, w_cat, b, l)

    if caches is None:
        attn = _attn_prompt(qa, kb, vb, lq1, lk1, lq2, lk2, g_attn_head, lam_init)
    else:
        cache_k, cache_v, b_off = caches
        attn = _attn_sample(qa, kb, vb, cache_k, cache_v, b_off,
                            lq1, lk1, lq2, lk2, g_attn_head, lam_init)

    seq3 = lambda a: a.reshape(b, l, a.shape[-1])
    hm, c1, n1, m1 = _mlstm(seq3(qm), seq3(km), seq3(vm), seq3(gz), gate_bias, g_mlstm_head,
                            c0, n0, m0.reshape(b, 1, H_M))

    y, conv1 = _ffn(x2d, attn.reshape(b * l, D_MODEL), hm.reshape(b * l, D_MODEL), om, ga, gm,
                    conv0, wout, wup, wdn, g_post_mix, g_pre_ffn, g_post_ffn, conv_w, conv_b, b, l)
    return (y.reshape(b, l, D_MODEL), kf.reshape(b, l, H_A, DV_A), vf.reshape(b, l, H_A, DV_A),
            c1, n1, m1.reshape(b, H_M), conv1)


def kernel(x_prompt, x_sample, cache_k, cache_v, state_C, state_n, state_m, state_conv, g_pre_mix, w_in, b_gates, lam_q1, lam_k1, lam_q2, lam_k2, g_attn_head, g_mlstm_head, w_out, g_post_mix, g_pre_ffn, w_up, conv_w, conv_b, w_down, g_post_ffn):
    depth = w_in.shape[0]
    bp = x_prompt.shape[0]
    yp, ys = x_prompt, x_sample
    outs_p, outs_s = [], []
    n_main = 6 * D_MODEL
    for li in range(depth):
        lam_init = 0.8 - 0.6 * math.exp(-0.3 * li)
        w = w_in[li]
        w_cat = jnp.concatenate(
            [w[:, :n_main], w[:, n_main + 2 * H_M:], w[:, n_main:n_main + 2 * H_M],
             jnp.zeros((D_MODEL, GATE_LANES - 2 * H_M), w.dtype)], axis=1).astype(BF16)
        gate_bias = jnp.concatenate(
            [b_gates[li], jnp.zeros((GATE_LANES - 2 * H_M,), F32)]).reshape(1, GATE_LANES)
        row = lambda a: a.reshape(1, -1)
        wts = (row(g_pre_mix[li]), w_cat, gate_bias, row(lam_q1[li]), row(lam_k1[li]),
               row(lam_q2[li]), row(lam_k2[li]), row(g_attn_head[li]), g_mlstm_head[li],
               w_out[li].astype(BF16), row(g_post_mix[li]), row(g_pre_ffn[li]),
               w_up[li].astype(BF16), conv_w[li], row(conv_b[li]), w_down[li].astype(BF16),
               row(g_post_ffn[li]))
        c0 = jnp.zeros((bp, H_M, DH_M, DH_M), F32)
        n0 = jnp.zeros((bp, H_M, DH_M), F32)
        m0 = jnp.zeros((bp, H_M), F32)
        conv0 = jnp.zeros((bp, CONV_W - 1, D_FF), F32)
        yp, *sp = _layer(yp, None, c0, n0, m0, conv0, wts, lam_init)
        bs, past = cache_k.shape[1], cache_k.shape[2]
        caches = (cache_k.reshape(depth * bs, past * H_A, DV_A),
                  cache_v.reshape(depth * bs, past * H_A, DV_A), li * bs)
        ys, *ss = _layer(ys, caches, state_C[li], state_n[li], state_m[li],
                         state_conv[li], wts, lam_init)
        outs_p.append(sp)
        outs_s.append(ss)
    k_p, v_p, c_p, n_p, m_p, conv_p = [jnp.stack([o[i] for o in outs_p]) for i in range(6)]
    k_s, v_s, c_s, n_s, m_s, conv_s = [jnp.stack([o[i] for o in outs_s]) for i in range(6)]
    return (yp, ys, k_p, v_p, c_p, n_p, m_p, conv_p, k_s, v_s, c_s, n_s, m_s, conv_s)
```

```python
import functools
import math

import jax
import jax.numpy as jnp
import numpy as np
from jax import lax
from jax.experimental import pallas as pl
from jax.experimental.pallas import tpu as pltpu

F32 = jnp.float32
BF16 = jnp.bfloat16

D_MODEL = 1024
CHUNK = 64
H_A = 8
DH_A = 64
DV_A = 2 * DH_A
H_M = 4
DH_M = D_MODEL // H_M
D_FF = 2816
CONV_W = 3
EPS = 1e-6
N_SEG = 9
GATE_LANES = 128
W_CAT = N_SEG * D_MODEL + GATE_LANES
QK_SCALE = DH_A ** -0.5 * math.log2(math.e)

V7X_VMEM_LIMIT = 56 * 1024 * 1024

PROJ_ROWS = 256
ATTN_TILE = 512
MLSTM_CHUNK = 256
FFN_ROWS = 256

NT_DIMS = (((1,), (1,)), ((), ()))
TN_DIMS = (((0,), (0,)), ((), ()))


def _rms(x, g):
    return x * lax.rsqrt(jnp.mean(x * x, axis=-1, keepdims=True) + EPS) * g


def _resident(shape):
    nd = len(shape)
    return pl.BlockSpec(shape, lambda *_: (0,) * nd, pipeline_mode=pl.Buffered(1))


def _inproj_kernel(x_ref, g_ref, w_ref, qa_ref, kf_ref, kb_ref, vf_ref, vb_ref,
                   qm_ref, km_ref, vm_ref, om_ref, ga_ref, gm_ref, gz_ref, *, nbt, lt, transposed):
    hb = _rms(x_ref[...], g_ref[...]).astype(BF16)

    def seg(j, width=D_MODEL):
        return jnp.dot(hb, w_ref[:, j * D_MODEL:j * D_MODEL + width],
                       preferred_element_type=F32)

    def put_heads(ref, val, transpose):
        for h in range(H_A):
            piece = val[:, h * DV_A:(h + 1) * DV_A]
            if transpose:
                ref[0, h] = piece.T.astype(BF16)
            else:
                ref[:, h] = piece.astype(BF16).reshape(nbt, lt, DV_A)

    put_heads(qa_ref, seg(0) * QK_SCALE, transposed)
    k = seg(1)
    kf_ref[...] = k
    put_heads(kb_ref, k, False)
    v = seg(2)
    vf_ref[...] = v
    put_heads(vb_ref, v, transposed)
    qm_ref[...] = seg(3).astype(BF16)
    km_ref[...] = seg(4).astype(BF16)
    vm_ref[...] = seg(5).astype(BF16)
    om_ref[...] = jax.nn.sigmoid(seg(6)).astype(BF16)
    ga_ref[...] = jax.nn.sigmoid(seg(7)).astype(BF16)
    gm_ref[...] = jax.nn.sigmoid(seg(8)).astype(BF16)
    gz_ref[...] = seg(9, GATE_LANES)


def _inproj(x2d, g, w_cat, nb, seq):
    n = x2d.shape[0]
    tm = PROJ_ROWS
    transposed = seq >= tm
    if transposed:
        nbt, lt = 1, tm
        per_b = seq // tm
        head = pl.BlockSpec((1, H_A, lt, DV_A), lambda i: (i // per_b, 0, i % per_b, 0))
        head_t = pl.BlockSpec((1, H_A, DV_A, lt), lambda i: (i // per_b, 0, 0, i % per_b))
        hm16_t = jax.ShapeDtypeStruct((nb, H_A, DV_A, seq), BF16)
    else:
        nbt, lt = tm // seq, seq
        head = head_t = pl.BlockSpec((nbt, H_A, lt, DV_A), lambda i: (i, 0, 0, 0))
        hm16_t = jax.ShapeDtypeStruct((nb, H_A, seq, DV_A), BF16)
    row = pl.BlockSpec((tm, D_MODEL), lambda i: (i, 0))
    full32 = jax.ShapeDtypeStruct((n, D_MODEL), F32)
    full16 = jax.ShapeDtypeStruct((n, D_MODEL), BF16)
    hm16 = jax.ShapeDtypeStruct((nb, H_A, seq, DV_A), BF16)
    return pl.pallas_call(
        functools.partial(_inproj_kernel, nbt=nbt, lt=lt, transposed=transposed),
        grid=(n // tm,),
        in_specs=[row, _resident((1, D_MODEL)), _resident((D_MODEL, W_CAT))],
        out_specs=[head_t, row, head, row, head_t, row, row, row, row, row, row,
                   pl.BlockSpec((tm, GATE_LANES), lambda i: (i, 0))],
        out_shape=[hm16_t, full32, hm16, full32, hm16_t, full16, full16, full16,
                   full16, full16, full16, jax.ShapeDtypeStruct((n, GATE_LANES), F32)],
        compiler_params=pltpu.CompilerParams(
            dimension_semantics=("arbitrary",), vmem_limit_bytes=V7X_VMEM_LIMIT),
        name="inproj",
    )(x2d, g, w_cat)


def _lam(lq1, lk1, lq2, lk2, lam_init):
    return (jnp.exp(jnp.sum(lq1 * lk1, axis=-1, keepdims=True))
            - jnp.exp(jnp.sum(lq2 * lk2, axis=-1, keepdims=True)) + lam_init)


def _split_maps(q, axis):
    idx = lax.broadcasted_iota(jnp.int32, q.shape, axis)
    zero = jnp.zeros_like(q)
    return jnp.where(idx < DH_A, q, zero), jnp.where(idx >= DH_A, q, zero)


def _attn_prompt_kernel(qi_tab, kj_tab, qt_ref, k_ref, vt_ref, lq1_ref, lk1_ref, lq2_ref, lk2_ref,
                        gcol_ref, o_ref, qz_s, m_s, l_s, acc_s, *, lam_init):
    p = pl.program_id(1)
    qi = qi_tab[p]
    kj = kj_tab[p]
    t = ATTN_TILE

    @pl.when(kj == 0)
    def _init():
        for h in range(H_A):
            q1, q2 = _split_maps(qt_ref[h], 0)
            qz_s[0, h] = q1
            qz_s[1, h] = q2
        m_s[...] = jnp.full(m_s.shape, -jnp.inf, F32)
        l_s[...] = jnp.zeros(l_s.shape, F32)
        acc_s[...] = jnp.zeros(acc_s.shape, F32)

    def sweep(masked):
        if masked:
            kr = lax.broadcasted_iota(jnp.int32, (t, t), 0) // CHUNK
            qc = lax.broadcasted_iota(jnp.int32, (t, t), 1) // CHUNK
            visible = kr <= qc

        def scores(h, mp):
            s = jnp.dot(k_ref[h], qz_s[mp, h], preferred_element_type=F32)
            return jnp.where(visible, s, -jnp.inf) if masked else s

        pairs = [(h, mp) for h in range(H_A) for mp in range(2)]
        s_next = scores(*pairs[0])
        for i, (h, mp) in enumerate(pairs):
            s = s_next
            m_old = m_s[mp, h]
            m_new = jnp.maximum(m_old, jnp.max(s, axis=0, keepdims=True))
            if i + 1 < len(pairs):
                s_next = scores(*pairs[i + 1])
            alpha = jnp.exp2(m_old - m_new)
            pr = jnp.exp2(s - m_new)
            l_s[mp, h] = alpha * l_s[mp, h] + jnp.sum(pr, axis=0, keepdims=True)
            acc_s[mp, h] = alpha * acc_s[mp, h] + jnp.dot(
                vt_ref[h], pr.astype(BF16), preferred_element_type=F32)
            m_s[mp, h] = m_new

    @pl.when(kj < qi)
    def _full():
        sweep(False)

    @pl.when(kj == qi)
    def _diag():
        sweep(True)
        lam = _lam(lq1_ref[...], lk1_ref[...], lq2_ref[...], lk2_ref[...], lam_init)
        for h in range(H_A):
            ot = acc_s[0, h] / l_s[0, h] - lam * (acc_s[1, h] / l_s[1, h])
            ms = jnp.mean(ot * ot, axis=0, keepdims=True)
            ot = ot * lax.rsqrt(ms + EPS) * gcol_ref[...] * (1.0 - lam_init)
            o_ref[:, h * DV_A:(h + 1) * DV_A] = ot.T.astype(BF16)


def _attn_prompt(qt, k, vt, lq1, lk1, lq2, lk2, gh, lam_init):
    b, _, s, _ = k.shape
    t = ATTN_TILE
    nq = s // t
    pairs = [(i, j) for i in range(nq) for j in range(i + 1)]
    qi_tab = jnp.asarray(np.array([a for a, _ in pairs], np.int32))
    kj_tab = jnp.asarray(np.array([c for _, c in pairs], np.int32))
    qspec = pl.BlockSpec((None, H_A, DV_A, t), lambda bb, p, qi, kj: (bb, 0, 0, qi[p]))
    kspec = pl.BlockSpec((None, H_A, t, DV_A), lambda bb, p, qi, kj: (bb, 0, kj[p], 0))
    vspec = pl.BlockSpec((None, H_A, DV_A, t), lambda bb, p, qi, kj: (bb, 0, 0, kj[p]))
    small = lambda shape: pl.BlockSpec(shape, lambda bb, p, qi, kj: (0, 0))
    grid_spec = pltpu.PrefetchScalarGridSpec(
        num_scalar_prefetch=2,
        grid=(b, len(pairs)),
        in_specs=[qspec, kspec, vspec, small((1, DH_A)), small((1, DH_A)), small((1, DH_A)),
                  small((1, DH_A)), small((DV_A, 1))],
        out_specs=pl.BlockSpec((None, t, H_A * DV_A), lambda bb, p, qi, kj: (bb, qi[p], 0)),
        scratch_shapes=[pltpu.VMEM((2, H_A, DV_A, t), BF16),
                        pltpu.VMEM((2, H_A, 1, t), F32),
                        pltpu.VMEM((2, H_A, 1, t), F32),
                        pltpu.VMEM((2, H_A, DV_A, t), F32)],
    )
    return pl.pallas_call(
        functools.partial(_attn_prompt_kernel, lam_init=lam_init),
        grid_spec=grid_spec,
        out_shape=jax.ShapeDtypeStruct((b, s, H_A * DV_A), BF16),
        compiler_params=pltpu.CompilerParams(
            dimension_semantics=("arbitrary", "arbitrary"), vmem_limit_bytes=V7X_VMEM_LIMIT),
        name="attn_prompt",
    )(qi_tab, kj_tab, qt, k, vt, lq1, lk1, lq2, lk2, gh.reshape(DV_A, 1))


def _attn_sample_kernel(q_ref, kn_ref, vn_ref, ck_ref, cv_ref, lq1_ref, lk1_ref, lq2_ref, lk2_ref,
                        gh_ref, o_ref, *, lam_init):
    lam = _lam(lq1_ref[...], lk1_ref[...], lq2_ref[...], lk2_ref[...], lam_init)
    nq = q_ref.shape[1]
    past = ck_ref.shape[0] // H_A
    for h in range(H_A):
        sl = slice(h * DV_A, (h + 1) * DV_A)
        rows = pl.ds(h, past, stride=H_A)
        q1, q2 = _split_maps(q_ref[h], 1)
        qz = jnp.concatenate([q1, q2], axis=0)
        kc = ck_ref[rows, :].astype(BF16)
        sc = lax.dot_general(qz, kc, NT_DIMS, preferred_element_type=F32)
        sn = lax.dot_general(qz, kn_ref[h], NT_DIMS, preferred_element_type=F32)
        m = jnp.maximum(jnp.max(sc, axis=-1, keepdims=True), jnp.max(sn, axis=-1, keepdims=True))
        pc = jnp.exp2(sc - m)
        pn = jnp.exp2(sn - m)
        inv = 1.0 / (jnp.sum(pc, axis=-1, keepdims=True) + jnp.sum(pn, axis=-1, keepdims=True))
        pc = pc * inv
        pn = pn * inv
        ac = pc[:nq] - lam * pc[nq:]
        an = pn[:nq] - lam * pn[nq:]
        o = (jnp.dot(ac.astype(BF16), cv_ref[rows, :].astype(BF16), preferred_element_type=F32)
             + jnp.dot(an.astype(BF16), vn_ref[h], preferred_element_type=F32))
        o_ref[:, sl] = (_rms(o, gh_ref[...]) * (1.0 - lam_init)).astype(BF16)


def _attn_sample(q, kn, vn, cache_k, cache_v, b_off, lq1, lk1, lq2, lk2, gh, lam_init):
    b, _, l, _ = q.shape
    hspec = pl.BlockSpec((None, H_A, l, DV_A), lambda i: (i, 0, 0, 0))
    cspec = pl.BlockSpec((None,) + cache_k.shape[1:], lambda i: (b_off + i, 0, 0))
    small = lambda shape: pl.BlockSpec(shape, lambda i: (0, 0))
    return pl.pallas_call(
        functools.partial(_attn_sample_kernel, lam_init=lam_init),
        grid=(b,),
        in_specs=[hspec, hspec, hspec, cspec, cspec, small((1, DH_A)), small((1, DH_A)),
                  small((1, DH_A)), small((1, DH_A)), small((1, DV_A))],
        out_specs=pl.BlockSpec((None, l, H_A * DV_A), lambda i: (i, 0, 0)),
        out_shape=jax.ShapeDtypeStruct((b, l, H_A * DV_A), BF16),
        compiler_params=pltpu.CompilerParams(
            dimension_semantics=("arbitrary",), vmem_limit_bytes=V7X_VMEM_LIMIT),
        name="attn_sample",
    )(q, kn, vn, cache_k, cache_v, lq1, lk1, lq2, lk2, gh)


def _split3(x):
    hi = x.astype(BF16)
    r1 = x - hi.astype(F32)
    mid = r1.astype(BF16)
    lo = (r1 - mid.astype(F32)).astype(BF16)
    return hi, mid, lo


def _mlstm_kernel(q_ref, k_ref, v_ref, og_ref, gz_ref, bias_ref, gh_ref, c0_ref, n0_ref, m0_ref,
                  h_ref, c_ref, n_ref, m_ref, c_s, n_s, m_s, *, t):
    c = pl.program_id(1)

    @pl.when(c == 0)
    def _load_state():
        c_s[...] = c0_ref[...]
        n_s[...] = n0_ref[...]
        m_s[...] = m0_ref[...]

    gz = gz_ref[...] + bias_ref[...]
    lane = lax.broadcasted_iota(jnp.int32, gz.shape, 1)
    lf = jnp.minimum(gz, 0.0) - jnp.log1p(jnp.exp(-jnp.abs(gz)))
    lf = jnp.where((lane >= H_M) & (lane < 2 * H_M), lf, 0.0)
    row = lax.broadcasted_iota(jnp.int32, (t, t), 0)
    col = lax.broadcasted_iota(jnp.int32, (t, t), 1)
    causal = col <= row
    tril = jnp.where(causal, 1.0, 0.0).astype(BF16)
    bcum = sum(jnp.dot(tril, piece, preferred_element_type=F32) for piece in _split3(lf))
    gc = jnp.where(lane < H_M, gz, bcum)
    sel = jnp.where(lax.broadcasted_iota(jnp.int32, (8, GATE_LANES), 0)
                    == lax.broadcasted_iota(jnp.int32, (8, GATE_LANES), 1), 1.0, 0.0).astype(BF16)
    gr = sum(lax.dot_general(sel, piece, NT_DIMS, preferred_element_type=F32)
             for piece in _split3(gc))

    for h in range(H_M):
        sl = slice(h * DH_M, (h + 1) * DH_M)
        ig_c, b_c = gc[:, h:h + 1], gc[:, H_M + h:H_M + h + 1]
        ig_r, b_r = gr[h:h + 1, :], gr[H_M + h:H_M + h + 1, :]
        m_prev = m_s[:, h:h + 1]
        c_prev = c_s[h]
        n_prev = n_s[h:h + 1, :]
        q = q_ref[:, sl] * (DH_M ** -0.5)
        k = k_ref[:, sl]
        v = v_ref[:, sl]

        d = jnp.where(causal, b_c - b_r + ig_r, -jnp.inf)
        inter = b_c + m_prev
        m_t = jnp.maximum(inter, jnp.max(d, axis=-1, keepdims=True))
        w_intra = jnp.exp(d - m_t)
        w_inter = jnp.exp(inter - m_t)
        s = lax.dot_general(q, k, NT_DIMS, preferred_element_type=F32) * w_intra
        num = (jnp.dot(s.astype(BF16), v, preferred_element_type=F32)
               + w_inter * lax.dot_general(q, c_prev.astype(BF16), NT_DIMS,
                                           preferred_element_type=F32))
        den = (jnp.sum(s, axis=-1, keepdims=True)
               + w_inter * jnp.sum(q.astype(F32) * n_prev, axis=-1, keepdims=True))
        hh = num / jnp.maximum(jnp.abs(den), jnp.exp(-m_t))
        h_ref[:, sl] = (_rms(hh, gh_ref[h:h + 1, :]) * og_ref[:, sl].astype(F32)).astype(BF16)

        g_last = b_c[t - 1:t, :]
        logw = g_last - b_c + ig_c
        m_new = jnp.maximum(g_last + m_prev, jnp.max(logw, axis=0, keepdims=True))
        ws = jnp.exp(logw - m_new)
        wc = jnp.exp(g_last + m_prev - m_new)
        vw = (v.astype(F32) * ws).astype(BF16)
        c_s[h] = wc * c_prev + lax.dot_general(vw, k, TN_DIMS, preferred_element_type=F32)
        n_s[h:h + 1, :] = wc * n_prev + jnp.sum(k.astype(F32) * ws, axis=0, keepdims=True)
        m_s[:, h:h + 1] = m_new

    @pl.when(c == pl.num_programs(1) - 1)
    def _store_state():
        c_ref[...] = c_s[...]
        n_ref[...] = n_s[...]
        m_ref[...] = m_s[...]


def _mlstm(q, k, v, og, gz, bias, gh, c0, n0, m0):
    b, l, _ = q.shape
    t = min(MLSTM_CHUNK, l)
    nc = l // t
    seq = lambda width: pl.BlockSpec((None, t, width), lambda i, j: (i, j, 0))
    const = lambda shape: pl.BlockSpec(shape, lambda i, j: (0,) * len(shape))
    cspec = pl.BlockSpec((None, H_M, DH_M, DH_M), lambda i, j: (i, 0, 0, 0))
    nspec = pl.BlockSpec((None, H_M, DH_M), lambda i, j: (i, 0, 0))
    mspec = pl.BlockSpec((None, 1, H_M), lambda i, j: (i, 0, 0))
    return pl.pallas_call(
        functools.partial(_mlstm_kernel, t=t),
        grid=(b, nc),
        in_specs=[seq(D_MODEL), seq(D_MODEL), seq(D_MODEL), seq(D_MODEL), seq(GATE_LANES),
                  const((1, GATE_LANES)), const((H_M, DH_M)), cspec, nspec, mspec],
        out_specs=[seq(D_MODEL), cspec, nspec, mspec],
        out_shape=[jax.ShapeDtypeStruct((b, l, D_MODEL), BF16),
                   jax.ShapeDtypeStruct((b, H_M, DH_M, DH_M), F32),
                   jax.ShapeDtypeStruct((b, H_M, DH_M), F32),
                   jax.ShapeDtypeStruct((b, 1, H_M), F32)],
        scratch_shapes=[pltpu.VMEM((H_M, DH_M, DH_M), F32),
                        pltpu.VMEM((H_M, DH_M), F32),
                        pltpu.VMEM((1, H_M), F32)],
        compiler_params=pltpu.CompilerParams(
            dimension_semantics=("arbitrary", "arbitrary"), vmem_limit_bytes=V7X_VMEM_LIMIT),
        name="mlstm",
    )(q, k, v, og, gz, bias, gh, c0, n0, m0)


def _gated_gelu(x, u):
    c = -2.0 * math.sqrt(2.0 / math.pi)
    return (x * u) / (1.0 + jnp.exp(x * (c + (c * 0.044715) * (x * x))))


def _ffn_kernel(x_ref, attn_ref, hg_ref, ga_ref, gm_ref, conv0_ref, wout_ref, wup_ref, wdn_ref,
                gpm_ref, gpf_ref, gpo_ref, cw_ref, cb_ref, y_ref, cs_ref, g_s,
                *, halves, l, tiles_per_batch):
    i = pl.program_id(0)

    @pl.when(i % tiles_per_batch == 0)
    def _from_state():
        g_s[:, 6:8, :] = conv0_ref[...]

    @pl.when(i % tiles_per_batch != 0)
    def _from_prev_tile():
        g_s[:, 6:8, :] = g_s[:, l + 6:l + 8, :]

    def pre(rows):
        merged = ga_ref[rows, :] * attn_ref[rows, :] + gm_ref[rows, :] * hg_ref[rows, :]
        x1 = x_ref[rows, :] + _rms(jnp.dot(merged, wout_ref[...], preferred_element_type=F32),
                                   gpm_ref[...])
        return x1, _rms(x1, gpf_ref[...]).astype(BF16)

    def up(h2):
        return jnp.dot(h2, wup_ref[...], preferred_element_type=F32)

    def act(ug, bsl, r0, lh):
        nbh = bsl.stop - bsl.start
        g3 = ug[:, D_FF:].reshape(nbh, lh, D_FF)
        g_s[bsl, 8 + r0:8 + r0 + lh, :] = g3
        gconv = (cb_ref[...] + cw_ref[0:1, :] * g_s[bsl, 6 + r0:6 + r0 + lh, :]
                 + cw_ref[1:2, :] * g_s[bsl, 7 + r0:7 + r0 + lh, :] + cw_ref[2:3, :] * g3)
        return _gated_gelu(gconv.reshape(nbh * lh, D_FF), ug[:, :D_FF]).astype(BF16)

    def post(x1, a, rows):
        ff = jnp.dot(a, wdn_ref[...], preferred_element_type=F32)
        y_ref[rows, :] = x1 + _rms(ff, gpo_ref[...])

    (rows_a, *conv_a), (rows_b, *conv_b) = halves
    x1_a, h2_a = pre(rows_a)
    x1_b, h2_b = pre(rows_b)
    ug_a = up(h2_a)
    ug_b = up(h2_b)
    act_a = act(ug_a, *conv_a)
    post(x1_a, act_a, rows_a)
    act_b = act(ug_b, *conv_b)
    post(x1_b, act_b, rows_b)
    cs_ref[...] = g_s[:, l + 6:l + 8, :]


def _ffn(x2d, attn, hg, ga, gm, conv0, wout, wup, wdn, gpm, gpf, gpo, cw, cb, nb_total, seq):
    n = x2d.shape[0]
    if seq >= FFN_ROWS:
        tm, nb, l = FFN_ROWS, 1, FFN_ROWS
        tiles_per_batch = seq // tm
        hl = l // 2
        halves = ((slice(0, hl), slice(0, 1), 0, hl), (slice(hl, l), slice(0, 1), hl, hl))
    else:
        tm, nb, l = n, nb_total, seq
        tiles_per_batch = 1
        hb = nb // 2
        halves = ((slice(0, hb * l), slice(0, hb), 0, l), (slice(hb * l, tm), slice(hb, nb), 0, l))
    row = pl.BlockSpec((tm, D_MODEL), lambda i: (i, 0))
    cstate = pl.BlockSpec((nb, CONV_W - 1, D_FF), lambda i: (i // tiles_per_batch, 0, 0))
    return pl.pallas_call(
        functools.partial(_ffn_kernel, halves=halves, l=l, tiles_per_batch=tiles_per_batch),
        grid=(n // tm,),
        in_specs=[row, row, row, row, row, cstate,
                  _resident((D_MODEL, D_MODEL)), _resident((D_MODEL, 2 * D_FF)),
                  _resident((D_FF, D_MODEL)), _resident((1, D_MODEL)), _resident((1, D_MODEL)),
                  _resident((1, D_MODEL)), _resident((CONV_W, D_FF)), _resident((1, D_FF))],
        out_specs=[row, cstate],
        out_shape=[jax.ShapeDtypeStruct((n, D_MODEL), F32),
                   jax.ShapeDtypeStruct((nb_total, CONV_W - 1, D_FF), F32)],
        scratch_shapes=[pltpu.VMEM((nb, l + 8, D_FF), F32)],
        compiler_params=pltpu.CompilerParams(
            dimension_semantics=("arbitrary",), vmem_limit_bytes=V7X_VMEM_LIMIT),
        name="merge_ffn",
    )(x2d, attn, hg, ga, gm, conv0, wout, wup, wdn, gpm, gpf, gpo, cw, cb)


def _layer(x, caches, c0, n0, m0, conv0, wts, lam_init):
    (g_pre_mix, w_cat, gate_bias, lq1, lk1, lq2, lk2, g_attn_head, g_mlstm_head, wout, g_post_mix,
     g_pre_ffn, wup, conv_w, conv_b, wdn, g_post_ffn) = wts
    b, l, _ = x.shape
    x2d = x.reshape(b * l, D_MODEL)
    (qa, kf, kb, vf, vb, qm, km, vm, om, ga, gm, gz) = _inproj(x2d, g_pre_mix, w_cat, b, l)

    if caches is None:
        attn = _attn_prompt(qa, kb, vb, lq1, lk1, lq2, lk2, g_attn_head, lam_init)
    else:
        cache_k, cache_v, b_off = caches
        attn = _attn_sample(qa, kb, vb, cache_k, cache_v, b_off,
                            lq1, lk1, lq2, lk2, g_attn_head, lam_init)

    seq3 = lambda a: a.reshape(b, l, a.shape[-1])
    hm, c1, n1, m1 = _mlstm(seq3(qm), seq3(km), seq3(vm), seq3(om), seq3(gz), gate_bias, g_mlstm_head,
                            c0, n0, m0.reshape(b, 1, H_M))

    y, conv1 = _ffn(x2d, attn.reshape(b * l, D_MODEL), hm.reshape(b * l, D_MODEL), ga, gm,
                    conv0, wout, wup, wdn, g_post_mix, g_pre_ffn, g_post_ffn, conv_w, conv_b, b, l)
    return (y.reshape(b, l, D_MODEL), kf.reshape(b, l, H_A, DV_A), vf.reshape(b, l, H_A, DV_A),
            c1, n1, m1.reshape(b, H_M), conv1)


def kernel(x_prompt, x_sample, cache_k, cache_v, state_C, state_n, state_m, state_conv, g_pre_mix, w_in, b_gates, lam_q1, lam_k1, lam_q2, lam_k2, g_attn_head, g_mlstm_head, w_out, g_post_mix, g_pre_ffn, w_up, conv_w, conv_b, w_down, g_post_ffn):
    depth = w_in.shape[0]
    bp = x_prompt.shape[0]
    yp, ys = x_prompt, x_sample
    outs_p, outs_s = [], []
    n_main = 6 * D_MODEL
    for li in range(depth):
        lam_init = 0.8 - 0.6 * math.exp(-0.3 * li)
        w = w_in[li]
        w_cat = jnp.concatenate(
            [w[:, :n_main], w[:, n_main + 2 * H_M:], w[:, n_main:n_main + 2 * H_M],
             jnp.zeros((D_MODEL, GATE_LANES - 2 * H_M), w.dtype)], axis=1).astype(BF16)
        gate_bias = jnp.concatenate(
            [b_gates[li], jnp.zeros((GATE_LANES - 2 * H_M,), F32)]).reshape(1, GATE_LANES)
        row = lambda a: a.reshape(1, -1)
        wts = (row(g_pre_mix[li]), w_cat, gate_bias, row(lam_q1[li]), row(lam_k1[li]),
               row(lam_q2[li]), row(lam_k2[li]), row(g_attn_head[li]), g_mlstm_head[li],
               w_out[li].astype(BF16), row(g_post_mix[li]), row(g_pre_ffn[li]),
               w_up[li].astype(BF16), conv_w[li], row(conv_b[li]), w_down[li].astype(BF16),
               row(g_post_ffn[li]))
        c0 = jnp.zeros((bp, H_M, DH_M, DH_M), F32)
        n0 = jnp.zeros((bp, H_M, DH_M), F32)
        m0 = jnp.zeros((bp, H_M), F32)
        conv0 = jnp.zeros((bp, CONV_W - 1, D_FF), F32)
        yp, *sp = _layer(yp, None, c0, n0, m0, conv0, wts, lam_init)
        bs, past = cache_k.shape[1], cache_k.shape[2]
        caches = (cache_k.reshape(depth * bs, past * H_A, DV_A),
                  cache_v.reshape(depth * bs, past * H_A, DV_A), li * bs)
        ys, *ss = _layer(ys, caches, state_C[li], state_n[li], state_m[li],
                         state_conv[li], wts, lam_init)
        outs_p.append(sp)
        outs_s.append(ss)
    k_p, v_p, c_p, n_p, m_p, conv_p = [jnp.stack([o[i] for o in outs_p]) for i in range(6)]
    k_s, v_s, c_s, n_s, m_s, conv_s = [jnp.stack([o[i] for o in outs_s]) for i in range(6)]
    return (yp, ys, k_p, v_p, c_p, n_p, m_p, conv_p, k_s, v_s, c_s, n_s, m_s, conv_s)
```

```python
import functools
import math

import jax
import jax.numpy as jnp
import numpy as np
from jax import lax
from jax.experimental import pallas as pl
from jax.experimental.pallas import tpu as pltpu

F32 = jnp.float32
BF16 = jnp.bfloat16

D_MODEL = 1024
CHUNK = 64
H_A = 8
DH_A = 64
DV_A = 2 * DH_A
H_M = 4
DH_M = D_MODEL // H_M
D_FF = 2816
CONV_W = 3
EPS = 1e-6
N_SEG = 9
GATE_LANES = 128
W_CAT = N_SEG * D_MODEL + GATE_LANES
QK_SCALE = DH_A ** -0.5 * math.log2(math.e)

V7X_VMEM_LIMIT = 56 * 1024 * 1024

PROJ_ROWS = 256
ATTN_TILE = 512
GROWTH_LIMIT = 64.0
MLSTM_CHUNK = 256
FFN_ROWS = 256

NT_DIMS = (((1,), (1,)), ((), ()))
TN_DIMS = (((0,), (0,)), ((), ()))


def _rms(x, g):
    return x * lax.rsqrt(jnp.mean(x * x, axis=-1, keepdims=True) + EPS) * g


def _resident(shape):
    nd = len(shape)
    return pl.BlockSpec(shape, lambda *_: (0,) * nd, pipeline_mode=pl.Buffered(1))


def _inproj_kernel(x_ref, g_ref, w_ref, qa_ref, kf_ref, kb_ref, vf_ref, vb_ref,
                   qm_ref, km_ref, vm_ref, om_ref, ga_ref, gm_ref, gz_ref, *, nbt, lt, transposed):
    hb = _rms(x_ref[...], g_ref[...]).astype(BF16)

    def seg(j, width=D_MODEL):
        return jnp.dot(hb, w_ref[:, j * D_MODEL:j * D_MODEL + width],
                       preferred_element_type=F32)

    def put_heads(ref, val, transpose):
        for h in range(H_A):
            piece = val[:, h * DV_A:(h + 1) * DV_A]
            if transpose:
                ref[0, h] = piece.T.astype(BF16)
            else:
                ref[:, h] = piece.astype(BF16).reshape(nbt, lt, DV_A)

    put_heads(qa_ref, seg(0) * QK_SCALE, transposed)
    k = seg(1)
    kf_ref[...] = k
    put_heads(kb_ref, k, False)
    v = seg(2)
    vf_ref[...] = v
    put_heads(vb_ref, v, transposed)
    qm_ref[...] = seg(3).astype(BF16)
    km_ref[...] = seg(4).astype(BF16)
    vm_ref[...] = seg(5).astype(BF16)
    om_ref[...] = jax.nn.sigmoid(seg(6)).astype(BF16)
    ga_ref[...] = jax.nn.sigmoid(seg(7)).astype(BF16)
    gm_ref[...] = jax.nn.sigmoid(seg(8)).astype(BF16)
    gz_ref[...] = seg(9, GATE_LANES)


def _inproj(x2d, g, w_cat, nb, seq):
    n = x2d.shape[0]
    tm = PROJ_ROWS
    transposed = seq >= tm
    if transposed:
        nbt, lt = 1, tm
        per_b = seq // tm
        head = pl.BlockSpec((1, H_A, lt, DV_A), lambda i: (i // per_b, 0, i % per_b, 0))
        head_t = pl.BlockSpec((1, H_A, DV_A, lt), lambda i: (i // per_b, 0, 0, i % per_b))
        hm16_t = jax.ShapeDtypeStruct((nb, H_A, DV_A, seq), BF16)
    else:
        nbt, lt = tm // seq, seq
        head = head_t = pl.BlockSpec((nbt, H_A, lt, DV_A), lambda i: (i, 0, 0, 0))
        hm16_t = jax.ShapeDtypeStruct((nb, H_A, seq, DV_A), BF16)
    row = pl.BlockSpec((tm, D_MODEL), lambda i: (i, 0))
    full32 = jax.ShapeDtypeStruct((n, D_MODEL), F32)
    full16 = jax.ShapeDtypeStruct((n, D_MODEL), BF16)
    hm16 = jax.ShapeDtypeStruct((nb, H_A, seq, DV_A), BF16)
    return pl.pallas_call(
        functools.partial(_inproj_kernel, nbt=nbt, lt=lt, transposed=transposed),
        grid=(n // tm,),
        in_specs=[row, _resident((1, D_MODEL)), _resident((D_MODEL, W_CAT))],
        out_specs=[head_t, row, head, row, head_t, row, row, row, row, row, row,
                   pl.BlockSpec((tm, GATE_LANES), lambda i: (i, 0))],
        out_shape=[hm16_t, full32, hm16, full32, hm16_t, full16, full16, full16,
                   full16, full16, full16, jax.ShapeDtypeStruct((n, GATE_LANES), F32)],
        compiler_params=pltpu.CompilerParams(
            dimension_semantics=("arbitrary",), vmem_limit_bytes=V7X_VMEM_LIMIT),
        name="inproj",
    )(x2d, g, w_cat)


def _lam(lq1, lk1, lq2, lk2, lam_init):
    return (jnp.exp(jnp.sum(lq1 * lk1, axis=-1, keepdims=True))
            - jnp.exp(jnp.sum(lq2 * lk2, axis=-1, keepdims=True)) + lam_init)


def _split_maps(q, axis):
    idx = lax.broadcasted_iota(jnp.int32, q.shape, axis)
    zero = jnp.zeros_like(q)
    return jnp.where(idx < DH_A, q, zero), jnp.where(idx >= DH_A, q, zero)


def _attn_prompt_kernel(qi_tab, kj_tab, qt_ref, k_ref, vt_ref, lq1_ref, lk1_ref, lq2_ref, lk2_ref,
                        gcol_ref, o_ref, qz_s, m_s, l_s, acc_s, mt_s, lt_s, at_s, redo_s, *, lam_init):
    p = pl.program_id(1)
    qi = qi_tab[p]
    kj = kj_tab[p]
    t = ATTN_TILE
    first = kj == 0
    diag = kj == qi
    pairs = [(h, mp) for h in range(H_A) for mp in range(2)]
    n = len(pairs)

    @pl.when(first)
    def _init():
        for h in range(H_A):
            q1, q2 = _split_maps(qt_ref[h], 0)
            qz_s[0, h] = q1
            qz_s[1, h] = q2
        m_s[...] = jnp.full(m_s.shape, -jnp.inf, F32)
        l_s[...] = jnp.zeros(l_s.shape, F32)
        acc_s[...] = jnp.zeros(acc_s.shape, F32)
        redo_s[0] = 1

    def chunk_mask():
        kr = lax.broadcasted_iota(jnp.int32, (t, t), 0) // CHUNK
        qc = lax.broadcasted_iota(jnp.int32, (t, t), 1) // CHUNK
        return kr <= qc

    def scores(h, mp, visible):
        s = jnp.dot(k_ref[h], qz_s[mp, h], preferred_element_type=F32)
        return s if visible is None else jnp.where(visible, s, -jnp.inf)

    def two_pass(visible):
        s_next = scores(*pairs[0], visible)
        for i, (h, mp) in enumerate(pairs):
            s = s_next
            m_old = m_s[mp, h]
            m_new = jnp.maximum(m_old, jnp.max(s, axis=0, keepdims=True))
            if i + 1 < n:
                s_next = scores(*pairs[i + 1], visible)
            alpha = jnp.exp2(m_old - m_new)
            pr = jnp.exp2(s - m_new)
            l_s[mp, h] = alpha * l_s[mp, h] + jnp.sum(pr, axis=0, keepdims=True)
            acc_s[mp, h] = alpha * acc_s[mp, h] + jnp.dot(
                vt_ref[h], pr.astype(BF16), preferred_element_type=F32)
            m_s[mp, h] = m_new

    def one_pass(visible):
        s_next = scores(*pairs[0], visible)
        growth = None
        for i, (h, mp) in enumerate(pairs):
            s = s_next
            if i + 1 < n:
                s_next = scores(*pairs[i + 1], visible)
            m_ref = m_s[mp, h]
            pr = jnp.exp2(s - m_ref)
            mt = jnp.max(s, axis=0, keepdims=True)
            mt_s[mp, h] = mt
            lt_s[mp, h] = jnp.sum(pr, axis=0, keepdims=True)
            at_s[mp, h] = jnp.dot(vt_ref[h], pr.astype(BF16), preferred_element_type=F32)
            growth = mt - m_ref if growth is None else jnp.maximum(growth, mt - m_ref)
        redo_s[0] = jnp.where(jnp.max(growth) <= GROWTH_LIMIT, 0, 1)

    def commit():
        for h, mp in pairs:
            m_ref = m_s[mp, h]
            m_new = jnp.maximum(m_ref, mt_s[mp, h])
            alpha = jnp.exp2(m_ref - m_new)
            l_s[mp, h] = (l_s[mp, h] + lt_s[mp, h]) * alpha
            acc_s[mp, h] = (acc_s[mp, h] + at_s[mp, h]) * alpha
            m_s[mp, h] = m_new

    @pl.when(jnp.logical_and(jnp.logical_not(first), jnp.logical_not(diag)))
    def _fold_full():
        one_pass(None)

    @pl.when(jnp.logical_and(jnp.logical_not(first), diag))
    def _fold_diag():
        one_pass(chunk_mask())

    redo = redo_s[0] == 1

    @pl.when(jnp.logical_not(redo))
    def _commit():
        commit()

    @pl.when(jnp.logical_and(redo, jnp.logical_not(diag)))
    def _redo_full():
        two_pass(None)

    @pl.when(jnp.logical_and(redo, diag))
    def _redo_diag():
        two_pass(chunk_mask())

    @pl.when(diag)
    def _finish():
        lam = _lam(lq1_ref[...], lk1_ref[...], lq2_ref[...], lk2_ref[...], lam_init)
        for h in range(H_A):
            ot = acc_s[0, h] / l_s[0, h] - lam * (acc_s[1, h] / l_s[1, h])
            ms = jnp.mean(ot * ot, axis=0, keepdims=True)
            ot = ot * lax.rsqrt(ms + EPS) * gcol_ref[...] * (1.0 - lam_init)
            o_ref[:, h * DV_A:(h + 1) * DV_A] = ot.T.astype(BF16)


def _attn_prompt(qt, k, vt, lq1, lk1, lq2, lk2, gh, lam_init):
    b, _, s, _ = k.shape
    t = ATTN_TILE
    nq = s // t
    pairs = [(i, j) for i in range(nq) for j in range(i + 1)]
    qi_tab = jnp.asarray(np.array([a for a, _ in pairs], np.int32))
    kj_tab = jnp.asarray(np.array([c for _, c in pairs], np.int32))
    qspec = pl.BlockSpec((None, H_A, DV_A, t), lambda bb, p, qi, kj: (bb, 0, 0, qi[p]))
    kspec = pl.BlockSpec((None, H_A, t, DV_A), lambda bb, p, qi, kj: (bb, 0, kj[p], 0))
    vspec = pl.BlockSpec((None, H_A, DV_A, t), lambda bb, p, qi, kj: (bb, 0, 0, kj[p]))
    small = lambda shape: pl.BlockSpec(shape, lambda bb, p, qi, kj: (0, 0))
    grid_spec = pltpu.PrefetchScalarGridSpec(
        num_scalar_prefetch=2,
        grid=(b, len(pairs)),
        in_specs=[qspec, kspec, vspec, small((1, DH_A)), small((1, DH_A)), small((1, DH_A)),
                  small((1, DH_A)), small((DV_A, 1))],
        out_specs=pl.BlockSpec((None, t, H_A * DV_A), lambda bb, p, qi, kj: (bb, qi[p], 0)),
        scratch_shapes=[pltpu.VMEM((2, H_A, DV_A, t), BF16),
                        pltpu.VMEM((2, H_A, 1, t), F32),
                        pltpu.VMEM((2, H_A, 1, t), F32),
                        pltpu.VMEM((2, H_A, DV_A, t), F32),
                        pltpu.VMEM((2, H_A, 1, t), F32),
                        pltpu.VMEM((2, H_A, 1, t), F32),
                        pltpu.VMEM((2, H_A, DV_A, t), F32),
                        pltpu.SMEM((1,), jnp.int32)],
    )
    return pl.pallas_call(
        functools.partial(_attn_prompt_kernel, lam_init=lam_init),
        grid_spec=grid_spec,
        out_shape=jax.ShapeDtypeStruct((b, s, H_A * DV_A), BF16),
        compiler_params=pltpu.CompilerParams(
            dimension_semantics=("arbitrary", "arbitrary"), vmem_limit_bytes=V7X_VMEM_LIMIT),
        name="attn_prompt",
    )(qi_tab, kj_tab, qt, k, vt, lq1, lk1, lq2, lk2, gh.reshape(DV_A, 1))


def _attn_sample_kernel(q_ref, kn_ref, vn_ref, ck_ref, cv_ref, lq1_ref, lk1_ref, lq2_ref, lk2_ref,
                        gh_ref, o_ref, *, lam_init):
    lam = _lam(lq1_ref[...], lk1_ref[...], lq2_ref[...], lk2_ref[...], lam_init)
    nq = q_ref.shape[1]
    past = ck_ref.shape[0] // H_A
    for h in range(H_A):
        sl = slice(h * DV_A, (h + 1) * DV_A)
        rows = pl.ds(h, past, stride=H_A)
        q1, q2 = _split_maps(q_ref[h], 1)
        qz = jnp.concatenate([q1, q2], axis=0)
        kc = ck_ref[rows, :].astype(BF16)
        sc = lax.dot_general(qz, kc, NT_DIMS, preferred_element_type=F32)
        sn = lax.dot_general(qz, kn_ref[h], NT_DIMS, preferred_element_type=F32)
        m = jnp.maximum(jnp.max(sc, axis=-1, keepdims=True), jnp.max(sn, axis=-1, keepdims=True))
        pc = jnp.exp2(sc - m)
        pn = jnp.exp2(sn - m)
        inv = 1.0 / (jnp.sum(pc, axis=-1, keepdims=True) + jnp.sum(pn, axis=-1, keepdims=True))
        pc = pc * inv
        pn = pn * inv
        ac = pc[:nq] - lam * pc[nq:]
        an = pn[:nq] - lam * pn[nq:]
        o = (jnp.dot(ac.astype(BF16), cv_ref[rows, :].astype(BF16), preferred_element_type=F32)
             + jnp.dot(an.astype(BF16), vn_ref[h], preferred_element_type=F32))
        o_ref[:, sl] = (_rms(o, gh_ref[...]) * (1.0 - lam_init)).astype(BF16)


def _attn_sample(q, kn, vn, cache_k, cache_v, b_off, lq1, lk1, lq2, lk2, gh, lam_init):
    b, _, l, _ = q.shape
    hspec = pl.BlockSpec((None, H_A, l, DV_A), lambda i: (i, 0, 0, 0))
    cspec = pl.BlockSpec((None,) + cache_k.shape[1:], lambda i: (b_off + i, 0, 0))
    small = lambda shape: pl.BlockSpec(shape, lambda i: (0, 0))
    return pl.pallas_call(
        functools.partial(_attn_sample_kernel, lam_init=lam_init),
        grid=(b,),
        in_specs=[hspec, hspec, hspec, cspec, cspec, small((1, DH_A)), small((1, DH_A)),
                  small((1, DH_A)), small((1, DH_A)), small((1, DV_A))],
        out_specs=pl.BlockSpec((None, l, H_A * DV_A), lambda i: (i, 0, 0)),
        out_shape=jax.ShapeDtypeStruct((b, l, H_A * DV_A), BF16),
        compiler_params=pltpu.CompilerParams(
            dimension_semantics=("arbitrary",), vmem_limit_bytes=V7X_VMEM_LIMIT),
        name="attn_sample",
    )(q, kn, vn, cache_k, cache_v, lq1, lk1, lq2, lk2, gh)


def _split3(x):
    hi = x.astype(BF16)
    r1 = x - hi.astype(F32)
    mid = r1.astype(BF16)
    lo = (r1 - mid.astype(F32)).astype(BF16)
    return hi, mid, lo


def _mlstm_kernel(q_ref, k_ref, v_ref, og_ref, gz_ref, bias_ref, gh_ref, c0_ref, n0_ref, m0_ref,
                  h_ref, c_ref, n_ref, m_ref, c_s, n_s, m_s, *, t):
    c = pl.program_id(1)

    @pl.when(c == 0)
    def _load_state():
        c_s[...] = c0_ref[...]
        n_s[...] = n0_ref[...]
        m_s[...] = m0_ref[...]

    gz = gz_ref[...] + bias_ref[...]
    lane = lax.broadcasted_iota(jnp.int32, gz.shape, 1)
    lf = jnp.minimum(gz, 0.0) - jnp.log1p(jnp.exp(-jnp.abs(gz)))
    lf = jnp.where((lane >= H_M) & (lane < 2 * H_M), lf, 0.0)
    row = lax.broadcasted_iota(jnp.int32, (t, t), 0)
    col = lax.broadcasted_iota(jnp.int32, (t, t), 1)
    causal = col <= row
    tril = jnp.where(causal, 1.0, 0.0).astype(BF16)
    bcum = sum(jnp.dot(tril, piece, preferred_element_type=F32) for piece in _split3(lf))
    gc = jnp.where(lane < H_M, gz, bcum)
    sel = jnp.where(lax.broadcasted_iota(jnp.int32, (8, GATE_LANES), 0)
                    == lax.broadcasted_iota(jnp.int32, (8, GATE_LANES), 1), 1.0, 0.0).astype(BF16)
    gr = sum(lax.dot_general(sel, piece, NT_DIMS, preferred_element_type=F32)
             for piece in _split3(gc))

    for h in range(H_M):
        sl = slice(h * DH_M, (h + 1) * DH_M)
        ig_c, b_c = gc[:, h:h + 1], gc[:, H_M + h:H_M + h + 1]
        ig_r, b_r = gr[h:h + 1, :], gr[H_M + h:H_M + h + 1, :]
        m_prev = m_s[:, h:h + 1]
        c_prev = c_s[h]
        n_prev = n_s[h:h + 1, :]
        q = q_ref[:, sl] * (DH_M ** -0.5)
        k = k_ref[:, sl]
        v = v_ref[:, sl]

        d = jnp.where(causal, b_c - b_r + ig_r, -jnp.inf)
        inter = b_c + m_prev
        m_t = jnp.maximum(inter, jnp.max(d, axis=-1, keepdims=True))
        w_intra = jnp.exp(d - m_t)
        w_inter = jnp.exp(inter - m_t)
        s = lax.dot_general(q, k, NT_DIMS, preferred_element_type=F32) * w_intra
        num = (jnp.dot(s.astype(BF16), v, preferred_element_type=F32)
               + w_inter * lax.dot_general(q, c_prev.astype(BF16), NT_DIMS,
                                           preferred_element_type=F32))
        den = (jnp.sum(s, axis=-1, keepdims=True)
               + w_inter * jnp.sum(q.astype(F32) * n_prev, axis=-1, keepdims=True))
        hh = num / jnp.maximum(jnp.abs(den), jnp.exp(-m_t))
        h_ref[:, sl] = (_rms(hh, gh_ref[h:h + 1, :]) * og_ref[:, sl].astype(F32)).astype(BF16)

        g_last = b_c[t - 1:t, :]
        logw = g_last - b_c + ig_c
        m_new = jnp.maximum(g_last + m_prev, jnp.max(logw, axis=0, keepdims=True))
        ws = jnp.exp(logw - m_new)
        wc = jnp.exp(g_last + m_prev - m_new)
        vw = (v.astype(F32) * ws).astype(BF16)
        c_s[h] = wc * c_prev + lax.dot_general(vw, k, TN_DIMS, preferred_element_type=F32)
        n_s[h:h + 1, :] = wc * n_prev + jnp.sum(k.astype(F32) * ws, axis=0, keepdims=True)
        m_s[:, h:h + 1] = m_new

    @pl.when(c == pl.num_programs(1) - 1)
    def _store_state():
        c_ref[...] = c_s[...]
        n_ref[...] = n_s[...]
        m_ref[...] = m_s[...]


def _mlstm(q, k, v, og, gz, bias, gh, c0, n0, m0):
    b, l, _ = q.shape
    t = min(MLSTM_CHUNK, l)
    nc = l // t
    seq = lambda width: pl.BlockSpec((None, t, width), lambda i, j: (i, j, 0))
    const = lambda shape: pl.BlockSpec(shape, lambda i, j: (0,) * len(shape))
    cspec = pl.BlockSpec((None, H_M, DH_M, DH_M), lambda i, j: (i, 0, 0, 0))
    nspec = pl.BlockSpec((None, H_M, DH_M), lambda i, j: (i, 0, 0))
    mspec = pl.BlockSpec((None, 1, H_M), lambda i, j: (i, 0, 0))
    return pl.pallas_call(
        functools.partial(_mlstm_kernel, t=t),
        grid=(b, nc),
        in_specs=[seq(D_MODEL), seq(D_MODEL), seq(D_MODEL), seq(D_MODEL), seq(GATE_LANES),
                  const((1, GATE_LANES)), const((H_M, DH_M)), cspec, nspec, mspec],
        out_specs=[seq(D_MODEL), cspec, nspec, mspec],
        out_shape=[jax.ShapeDtypeStruct((b, l, D_MODEL), BF16),
                   jax.ShapeDtypeStruct((b, H_M, DH_M, DH_M), F32),
                   jax.ShapeDtypeStruct((b, H_M, DH_M), F32),
                   jax.ShapeDtypeStruct((b, 1, H_M), F32)],
        scratch_shapes=[pltpu.VMEM((H_M, DH_M, DH_M), F32),
                        pltpu.VMEM((H_M, DH_M), F32),
                        pltpu.VMEM((1, H_M), F32)],
        compiler_params=pltpu.CompilerParams(
            dimension_semantics=("arbitrary", "arbitrary"), vmem_limit_bytes=V7X_VMEM_LIMIT),
        name="mlstm",
    )(q, k, v, og, gz, bias, gh, c0, n0, m0)


def _gated_gelu(x, u):
    c = -2.0 * math.sqrt(2.0 / math.pi)
    return (x * u) / (1.0 + jnp.exp(x * (c + (c * 0.044715) * (x * x))))


def _ffn_kernel(x_ref, attn_ref, hg_ref, ga_ref, gm_ref, conv0_ref, wout_ref, wup_ref, wdn_ref,
                gpm_ref, gpf_ref, gpo_ref, cw_ref, cb_ref, y_ref, cs_ref, g_s,
                *, halves, l, tiles_per_batch):
    i = pl.program_id(0)

    @pl.when(i % tiles_per_batch == 0)
    def _from_state():
        g_s[:, 6:8, :] = conv0_ref[...]

    @pl.when(i % tiles_per_batch != 0)
    def _from_prev_tile():
        g_s[:, 6:8, :] = g_s[:, l + 6:l + 8, :]

    def pre(rows):
        merged = ga_ref[rows, :] * attn_ref[rows, :] + gm_ref[rows, :] * hg_ref[rows, :]
        x1 = x_ref[rows, :] + _rms(jnp.dot(merged, wout_ref[...], preferred_element_type=F32),
                                   gpm_ref[...])
        return x1, _rms(x1, gpf_ref[...]).astype(BF16)

    def up(h2):
        return jnp.dot(h2, wup_ref[...], preferred_element_type=F32)

    def act(ug, bsl, r0, lh):
        nbh = bsl.stop - bsl.start
        g3 = ug[:, D_FF:].reshape(nbh, lh, D_FF)
        g_s[bsl, 8 + r0:8 + r0 + lh, :] = g3
        gconv = (cb_ref[...] + cw_ref[0:1, :] * g_s[bsl, 6 + r0:6 + r0 + lh, :]
                 + cw_ref[1:2, :] * g_s[bsl, 7 + r0:7 + r0 + lh, :] + cw_ref[2:3, :] * g3)
        return _gated_gelu(gconv.reshape(nbh * lh, D_FF), ug[:, :D_FF]).astype(BF16)

    def post(x1, a, rows):
        ff = jnp.dot(a, wdn_ref[...], preferred_element_type=F32)
        y_ref[rows, :] = x1 + _rms(ff, gpo_ref[...])

    (rows_a, *conv_a), (rows_b, *conv_b) = halves
    x1_a, h2_a = pre(rows_a)
    x1_b, h2_b = pre(rows_b)
    ug_a = up(h2_a)
    ug_b = up(h2_b)
    act_a = act(ug_a, *conv_a)
    post(x1_a, act_a, rows_a)
    act_b = act(ug_b, *conv_b)
    post(x1_b, act_b, rows_b)
    cs_ref[...] = g_s[:, l + 6:l + 8, :]


def _ffn(x2d, attn, hg, ga, gm, conv0, wout, wup, wdn, gpm, gpf, gpo, cw, cb, nb_total, seq):
    n = x2d.shape[0]
    if seq >= FFN_ROWS:
        tm, nb, l = FFN_ROWS, 1, FFN_ROWS
        tiles_per_batch = seq // tm
        hl = l // 2
        halves = ((slice(0, hl), slice(0, 1), 0, hl), (slice(hl, l), slice(0, 1), hl, hl))
    else:
        tm, nb, l = n, nb_total, seq
        tiles_per_batch = 1
        hb = nb // 2
        halves = ((slice(0, hb * l), slice(0, hb), 0, l), (slice(hb * l, tm), slice(hb, nb), 0, l))
    row = pl.BlockSpec((tm, D_MODEL), lambda i: (i, 0))
    cstate = pl.BlockSpec((nb, CONV_W - 1, D_FF), lambda i: (i // tiles_per_batch, 0, 0))
    return pl.pallas_call(
        functools.partial(_ffn_kernel, halves=halves, l=l, tiles_per_batch=tiles_per_batch),
        grid=(n // tm,),
        in_specs=[row, row, row, row, row, cstate,
                  _resident((D_MODEL, D_MODEL)), _resident((D_MODEL, 2 * D_FF)),
                  _resident((D_FF, D_MODEL)), _resident((1, D_MODEL)), _resident((1, D_MODEL)),
                  _resident((1, D_MODEL)), _resident((CONV_W, D_FF)), _resident((1, D_FF))],
        out_specs=[row, cstate],
        out_shape=[jax.ShapeDtypeStruct((n, D_MODEL), F32),
                   jax.ShapeDtypeStruct((nb_total, CONV_W - 1, D_FF), F32)],
        scratch_shapes=[pltpu.VMEM((nb, l + 8, D_FF), F32)],
        compiler_params=pltpu.CompilerParams(
            dimension_semantics=("arbitrary",), vmem_limit_bytes=V7X_VMEM_LIMIT),
        name="merge_ffn",
    )(x2d, attn, hg, ga, gm, conv0, wout, wup, wdn, gpm, gpf, gpo, cw, cb)


def _layer(x, caches, c0, n0, m0, conv0, wts, lam_init):
    (g_pre_mix, w_cat, gate_bias, lq1, lk1, lq2, lk2, g_attn_head, g_mlstm_head, wout, g_post_mix,
     g_pre_ffn, wup, conv_w, conv_b, wdn, g_post_ffn) = wts
    b, l, _ = x.shape
    x2d = x.reshape(b * l, D_MODEL)
    (qa, kf, kb, vf, vb, qm, km, vm, om, ga, gm, gz) = _inproj(x2d, g_pre_mix, w_cat, b, l)

    if caches is None:
        attn = _attn_prompt(qa, kb, vb, lq1, lk1, lq2, lk2, g_attn_head, lam_init)
    else:
        cache_k, cache_v, b_off = caches
        attn = _attn_sample(qa, kb, vb, cache_k, cache_v, b_off,
                            lq1, lk1, lq2, lk2, g_attn_head, lam_init)

    seq3 = lambda a: a.reshape(b, l, a.shape[-1])
    hm, c1, n1, m1 = _mlstm(seq3(qm), seq3(km), seq3(vm), seq3(om), seq3(gz), gate_bias, g_mlstm_head,
                            c0, n0, m0.reshape(b, 1, H_M))

    y, conv1 = _ffn(x2d, attn.reshape(b * l, D_MODEL), hm.reshape(b * l, D_MODEL), ga, gm,
                    conv0, wout, wup, wdn, g_post_mix, g_pre_ffn, g_post_ffn, conv_w, conv_b, b, l)
    return (y.reshape(b, l, D_MODEL), kf.reshape(b, l, H_A, DV_A), vf.reshape(b, l, H_A, DV_A),
            c1, n1, m1.reshape(b, H_M), conv1)


def kernel(x_prompt, x_sample, cache_k, cache_v, state_C, state_n, state_m, state_conv, g_pre_mix, w_in, b_gates, lam_q1, lam_k1, lam_q2, lam_k2, g_attn_head, g_mlstm_head, w_out, g_post_mix, g_pre_ffn, w_up, conv_w, conv_b, w_down, g_post_ffn):
    depth = w_in.shape[0]
    bp = x_prompt.shape[0]
    yp, ys = x_prompt, x_sample
    outs_p, outs_s = [], []
    n_main = 6 * D_MODEL
    for li in range(depth):
        lam_init = 0.8 - 0.6 * math.exp(-0.3 * li)
        w = w_in[li]
        w_cat = jnp.concatenate(
            [w[:, :n_main], w[:, n_main + 2 * H_M:], w[:, n_main:n_main + 2 * H_M],
             jnp.zeros((D_MODEL, GATE_LANES - 2 * H_M), w.dtype)], axis=1).astype(BF16)
        gate_bias = jnp.concatenate(
            [b_gates[li], jnp.zeros((GATE_LANES - 2 * H_M,), F32)]).reshape(1, GATE_LANES)
        row = lambda a: a.reshape(1, -1)
        wts = (row(g_pre_mix[li]), w_cat, gate_bias, row(lam_q1[li]), row(lam_k1[li]),
               row(lam_q2[li]), row(lam_k2[li]), row(g_attn_head[li]), g_mlstm_head[li],
               w_out[li].astype(BF16), row(g_post_mix[li]), row(g_pre_ffn[li]),
               w_up[li].astype(BF16), conv_w[li], row(conv_b[li]), w_down[li].astype(BF16),
               row(g_post_ffn[li]))
        c0 = jnp.zeros((bp, H_M, DH_M, DH_M), F32)
        n0 = jnp.zeros((bp, H_M, DH_M), F32)
        m0 = jnp.zeros((bp, H_M), F32)
        conv0 = jnp.zeros((bp, CONV_W - 1, D_FF), F32)
        yp, *sp = _layer(yp, None, c0, n0, m0, conv0, wts, lam_init)
        bs, past = cache_k.shape[1], cache_k.shape[2]
        caches = (cache_k.reshape(depth * bs, past * H_A, DV_A),
                  cache_v.reshape(depth * bs, past * H_A, DV_A), li * bs)
        ys, *ss = _layer(ys, caches, state_C[li], state_n[li], state_m[li],
                         state_conv[li], wts, lam_init)
        outs_p.append(sp)
        outs_s.append(ss)
    k_p, v_p, c_p, n_p, m_p, conv_p = [jnp.stack([o[i] for o in outs_p]) for i in range(6)]
    k_s, v_s, c_s, n_s, m_s, conv_s = [jnp.stack([o[i] for o in outs_s]) for i in range(6)]
    return (yp, ys, k_p, v_p, c_p, n_p, m_p, conv_p, k_s, v_s, c_s, n_s, m_s, conv_s)
```

```python
import functools
import math

import jax
import jax.numpy as jnp
import numpy as np
from jax import lax
from jax.experimental import pallas as pl
from jax.experimental.pallas import tpu as pltpu

F32 = jnp.float32
BF16 = jnp.bfloat16

D_MODEL = 1024
CHUNK = 64
H_A = 8
DH_A = 64
DV_A = 2 * DH_A
H_M = 4
DH_M = D_MODEL // H_M
D_FF = 2816
CONV_W = 3
EPS = 1e-6
N_SEG_A = 6
GATE_LANES = 128
QK_SCALE = DH_A ** -0.5 * math.log2(math.e)

V7X_VMEM_LIMIT = 56 * 1024 * 1024

PROJ_ROWS = 256
ATTN_TILE = 512
GROWTH_LIMIT = 64.0
MLSTM_CHUNK = 256
FFN_ROWS = 256

NT_DIMS = (((1,), (1,)), ((), ()))
TN_DIMS = (((0,), (0,)), ((), ()))


def _rms(x, g):
    return x * lax.rsqrt(jnp.mean(x * x, axis=-1, keepdims=True) + EPS) * g


def _resident(shape):
    nd = len(shape)
    return pl.BlockSpec(shape, lambda *_: (0,) * nd, pipeline_mode=pl.Buffered(1))


def _inproj_kernel(x_ref, g_ref, wa_ref, wb_ref, wg_ref, qa_ref, kf_ref, kb_ref, vf_ref, vb_ref,
                   qm_ref, km_ref, vm_ref, om_ref, ga_ref, gm_ref, gz_ref, *, nbt, lt, transposed):
    hb = _rms(x_ref[...], g_ref[...]).astype(BF16)

    def seg(j):
        w_ref, jj = (wa_ref, j) if j < N_SEG_A else (wb_ref, j - N_SEG_A)
        return jnp.dot(hb, w_ref[:, jj * D_MODEL:(jj + 1) * D_MODEL], preferred_element_type=F32)

    def put_heads(ref, val, transpose):
        for h in range(H_A):
            piece = val[:, h * DV_A:(h + 1) * DV_A]
            if transpose:
                ref[0, h] = piece.T.astype(BF16)
            else:
                ref[:, h] = piece.astype(BF16).reshape(nbt, lt, DV_A)

    put_heads(qa_ref, seg(0) * QK_SCALE, transposed)
    k = seg(1)
    kf_ref[...] = k
    put_heads(kb_ref, k, False)
    v = seg(2)
    vf_ref[...] = v
    put_heads(vb_ref, v, transposed)
    qm_ref[...] = seg(3).astype(BF16)
    km_ref[...] = seg(4).astype(BF16)
    vm_ref[...] = seg(5).astype(BF16)
    om_ref[...] = jax.nn.sigmoid(seg(6)).astype(BF16)
    ga_ref[...] = jax.nn.sigmoid(seg(7)).astype(BF16)
    gm_ref[...] = jax.nn.sigmoid(seg(8)).astype(BF16)
    gz_ref[...] = jnp.dot(hb, wg_ref[...], preferred_element_type=F32)


def _inproj(x2d, g, wa, wb, wg, nb, seq):
    n = x2d.shape[0]
    tm = PROJ_ROWS
    transposed = seq >= tm
    if transposed:
        nbt, lt = 1, tm
        per_b = seq // tm
        head = pl.BlockSpec((1, H_A, lt, DV_A), lambda i: (i // per_b, 0, i % per_b, 0))
        head_t = pl.BlockSpec((1, H_A, DV_A, lt), lambda i: (i // per_b, 0, 0, i % per_b))
        hm16_t = jax.ShapeDtypeStruct((nb, H_A, DV_A, seq), BF16)
    else:
        nbt, lt = tm // seq, seq
        head = head_t = pl.BlockSpec((nbt, H_A, lt, DV_A), lambda i: (i, 0, 0, 0))
        hm16_t = jax.ShapeDtypeStruct((nb, H_A, seq, DV_A), BF16)
    row = pl.BlockSpec((tm, D_MODEL), lambda i: (i, 0))
    full32 = jax.ShapeDtypeStruct((n, D_MODEL), F32)
    full16 = jax.ShapeDtypeStruct((n, D_MODEL), BF16)
    hm16 = jax.ShapeDtypeStruct((nb, H_A, seq, DV_A), BF16)
    return pl.pallas_call(
        functools.partial(_inproj_kernel, nbt=nbt, lt=lt, transposed=transposed),
        grid=(n // tm,),
        in_specs=[row, _resident((1, D_MODEL)), _resident(wa.shape), _resident(wb.shape),
                  _resident(wg.shape)],
        out_specs=[head_t, row, head, row, head_t, row, row, row, row, row, row,
                   pl.BlockSpec((tm, GATE_LANES), lambda i: (i, 0))],
        out_shape=[hm16_t, full32, hm16, full32, hm16_t, full16, full16, full16,
                   full16, full16, full16, jax.ShapeDtypeStruct((n, GATE_LANES), F32)],
        compiler_params=pltpu.CompilerParams(
            dimension_semantics=("arbitrary",), vmem_limit_bytes=V7X_VMEM_LIMIT),
        name="inproj",
    )(x2d, g, wa, wb, wg)


def _lam(lq1, lk1, lq2, lk2, lam_init):
    return (jnp.exp(jnp.sum(lq1 * lk1, axis=-1, keepdims=True))
            - jnp.exp(jnp.sum(lq2 * lk2, axis=-1, keepdims=True)) + lam_init)


def _split_maps(q, axis):
    idx = lax.broadcasted_iota(jnp.int32, q.shape, axis)
    zero = jnp.zeros_like(q)
    return jnp.where(idx < DH_A, q, zero), jnp.where(idx >= DH_A, q, zero)


def _attn_prompt_kernel(qi_tab, kj_tab, qt_ref, k_ref, vt_ref, lq1_ref, lk1_ref, lq2_ref, lk2_ref,
                        gcol_ref, o_ref, qz_s, m_s, l_s, acc_s, mt_s, lt_s, at_s, redo_s, *, lam_init):
    p = pl.program_id(1)
    qi = qi_tab[p]
    kj = kj_tab[p]
    t = ATTN_TILE
    first = kj == 0
    diag = kj == qi
    pairs = [(h, mp) for h in range(H_A) for mp in range(2)]
    n = len(pairs)

    @pl.when(first)
    def _init():
        for h in range(H_A):
            q1, q2 = _split_maps(qt_ref[h], 0)
            qz_s[0, h] = q1
            qz_s[1, h] = q2
        m_s[...] = jnp.full(m_s.shape, -jnp.inf, F32)
        l_s[...] = jnp.zeros(l_s.shape, F32)
        acc_s[...] = jnp.zeros(acc_s.shape, F32)

    def chunk_mask():
        kr = lax.broadcasted_iota(jnp.int32, (t, t), 0) // CHUNK
        qc = lax.broadcasted_iota(jnp.int32, (t, t), 1) // CHUNK
        return kr <= qc

    def scores(h, mp, visible):
        s = jnp.dot(k_ref[h], qz_s[mp, h], preferred_element_type=F32)
        return s if visible is None else jnp.where(visible, s, -jnp.inf)

    def two_pass(visible):
        s_next = scores(*pairs[0], visible)
        for i, (h, mp) in enumerate(pairs):
            s = s_next
            m_old = m_s[mp, h]
            m_new = jnp.maximum(m_old, jnp.max(s, axis=0, keepdims=True))
            if i + 1 < n:
                s_next = scores(*pairs[i + 1], visible)
            alpha = jnp.exp2(m_old - m_new)
            pr = jnp.exp2(s - m_new)
            l_s[mp, h] = alpha * l_s[mp, h] + jnp.sum(pr, axis=0, keepdims=True)
            acc_s[mp, h] = alpha * acc_s[mp, h] + jnp.dot(
                vt_ref[h], pr.astype(BF16), preferred_element_type=F32)
            m_s[mp, h] = m_new

    def one_pass(visible):
        s_next = scores(*pairs[0], visible)
        growth = None
        for i, (h, mp) in enumerate(pairs):
            s = s_next
            if i + 1 < n:
                s_next = scores(*pairs[i + 1], visible)
            m_ref = jnp.where(first, s[0:1, :], m_s[mp, h])
            m_s[mp, h] = m_ref
            pr = jnp.exp2(s - m_ref)
            mt = jnp.max(s, axis=0, keepdims=True)
            mt_s[mp, h] = mt
            lt_s[mp, h] = jnp.sum(pr, axis=0, keepdims=True)
            at_s[mp, h] = jnp.dot(vt_ref[h], pr.astype(BF16), preferred_element_type=F32)
            growth = mt - m_ref if growth is None else jnp.maximum(growth, mt - m_ref)
        redo_s[0] = jnp.where(jnp.max(growth) <= GROWTH_LIMIT, 0, 1)

    def commit():
        for h, mp in pairs:
            m_ref = m_s[mp, h]
            m_new = jnp.maximum(m_ref, mt_s[mp, h])
            alpha = jnp.exp2(m_ref - m_new)
            l_s[mp, h] = (l_s[mp, h] + lt_s[mp, h]) * alpha
            acc_s[mp, h] = (acc_s[mp, h] + at_s[mp, h]) * alpha
            m_s[mp, h] = m_new

    @pl.when(jnp.logical_not(diag))
    def _fold_full():
        one_pass(None)

    @pl.when(diag)
    def _fold_diag():
        one_pass(chunk_mask())

    redo = redo_s[0] == 1

    @pl.when(jnp.logical_not(redo))
    def _commit():
        commit()

    @pl.when(jnp.logical_and(redo, jnp.logical_not(diag)))
    def _redo_full():
        two_pass(None)

    @pl.when(jnp.logical_and(redo, diag))
    def _redo_diag():
        two_pass(chunk_mask())

    @pl.when(diag)
    def _finish():
        lam = _lam(lq1_ref[...], lk1_ref[...], lq2_ref[...], lk2_ref[...], lam_init)
        for h in range(H_A):
            ot = acc_s[0, h] / l_s[0, h] - lam * (acc_s[1, h] / l_s[1, h])
            ms = jnp.mean(ot * ot, axis=0, keepdims=True)
            ot = ot * lax.rsqrt(ms + EPS) * gcol_ref[...] * (1.0 - lam_init)
            o_ref[:, h * DV_A:(h + 1) * DV_A] = ot.T.astype(BF16)


def _attn_prompt(qt, k, vt, lq1, lk1, lq2, lk2, gh, lam_init):
    b, _, s, _ = k.shape
    t = ATTN_TILE
    nq = s // t
    pairs = [(i, j) for i in range(nq) for j in range(i + 1)]
    qi_tab = jnp.asarray(np.array([a for a, _ in pairs], np.int32))
    kj_tab = jnp.asarray(np.array([c for _, c in pairs], np.int32))
    qspec = pl.BlockSpec((None, H_A, DV_A, t), lambda bb, p, qi, kj: (bb, 0, 0, qi[p]))
    kspec = pl.BlockSpec((None, H_A, t, DV_A), lambda bb, p, qi, kj: (bb, 0, kj[p], 0))
    vspec = pl.BlockSpec((None, H_A, DV_A, t), lambda bb, p, qi, kj: (bb, 0, 0, kj[p]))
    small = lambda shape: pl.BlockSpec(shape, lambda bb, p, qi, kj: (0, 0))
    grid_spec = pltpu.PrefetchScalarGridSpec(
        num_scalar_prefetch=2,
        grid=(b, len(pairs)),
        in_specs=[qspec, kspec, vspec, small((1, DH_A)), small((1, DH_A)), small((1, DH_A)),
                  small((1, DH_A)), small((DV_A, 1))],
        out_specs=pl.BlockSpec((None, t, H_A * DV_A), lambda bb, p, qi, kj: (bb, qi[p], 0)),
        scratch_shapes=[pltpu.VMEM((2, H_A, DV_A, t), BF16),
                        pltpu.VMEM((2, H_A, 1, t), F32),
                        pltpu.VMEM((2, H_A, 1, t), F32),
                        pltpu.VMEM((2, H_A, DV_A, t), F32),
                        pltpu.VMEM((2, H_A, 1, t), F32),
                        pltpu.VMEM((2, H_A, 1, t), F32),
                        pltpu.VMEM((2, H_A, DV_A, t), F32),
                        pltpu.SMEM((1,), jnp.int32)],
    )
    return pl.pallas_call(
        functools.partial(_attn_prompt_kernel, lam_init=lam_init),
        grid_spec=grid_spec,
        out_shape=jax.ShapeDtypeStruct((b, s, H_A * DV_A), BF16),
        compiler_params=pltpu.CompilerParams(
            dimension_semantics=("arbitrary", "arbitrary"), vmem_limit_bytes=V7X_VMEM_LIMIT),
        name="attn_prompt",
    )(qi_tab, kj_tab, qt, k, vt, lq1, lk1, lq2, lk2, gh.reshape(DV_A, 1))


def _attn_sample_kernel(q_ref, kn_ref, vn_ref, ck_ref, cv_ref, lq1_ref, lk1_ref, lq2_ref, lk2_ref,
                        gh_ref, o_ref, *, lam_init):
    lam = _lam(lq1_ref[...], lk1_ref[...], lq2_ref[...], lk2_ref[...], lam_init)
    nq = q_ref.shape[1]
    past = ck_ref.shape[0] // H_A
    for h in range(H_A):
        sl = slice(h * DV_A, (h + 1) * DV_A)
        rows = pl.ds(h, past, stride=H_A)
        q1, q2 = _split_maps(q_ref[h], 1)
        qz = jnp.concatenate([q1, q2], axis=0)
        kc = ck_ref[rows, :].astype(BF16)
        sc = lax.dot_general(qz, kc, NT_DIMS, preferred_element_type=F32)
        sn = lax.dot_general(qz, kn_ref[h], NT_DIMS, preferred_element_type=F32)
        m = jnp.maximum(jnp.max(sc, axis=-1, keepdims=True), jnp.max(sn, axis=-1, keepdims=True))
        pc = jnp.exp2(sc - m)
        pn = jnp.exp2(sn - m)
        inv = 1.0 / (jnp.sum(pc, axis=-1, keepdims=True) + jnp.sum(pn, axis=-1, keepdims=True))
        pc = pc * inv
        pn = pn * inv
        ac = pc[:nq] - lam * pc[nq:]
        an = pn[:nq] - lam * pn[nq:]
        o = (jnp.dot(ac.astype(BF16), cv_ref[rows, :].astype(BF16), preferred_element_type=F32)
             + jnp.dot(an.astype(BF16), vn_ref[h], preferred_element_type=F32))
        o_ref[:, sl] = (_rms(o, gh_ref[...]) * (1.0 - lam_init)).astype(BF16)


def _attn_sample(q, kn, vn, cache_k, cache_v, b_off, lq1, lk1, lq2, lk2, gh, lam_init):
    b, _, l, _ = q.shape
    hspec = pl.BlockSpec((None, H_A, l, DV_A), lambda i: (i, 0, 0, 0))
    cspec = pl.BlockSpec((None,) + cache_k.shape[1:], lambda i: (b_off + i, 0, 0))
    small = lambda shape: pl.BlockSpec(shape, lambda i: (0, 0))
    return pl.pallas_call(
        functools.partial(_attn_sample_kernel, lam_init=lam_init),
        grid=(b,),
        in_specs=[hspec, hspec, hspec, cspec, cspec, small((1, DH_A)), small((1, DH_A)),
                  small((1, DH_A)), small((1, DH_A)), small((1, DV_A))],
        out_specs=pl.BlockSpec((None, l, H_A * DV_A), lambda i: (i, 0, 0)),
        out_shape=jax.ShapeDtypeStruct((b, l, H_A * DV_A), BF16),
        compiler_params=pltpu.CompilerParams(
            dimension_semantics=("arbitrary",), vmem_limit_bytes=V7X_VMEM_LIMIT),
        name="attn_sample",
    )(q, kn, vn, cache_k, cache_v, lq1, lk1, lq2, lk2, gh)


def _split3(x):
    hi = x.astype(BF16)
    r1 = x - hi.astype(F32)
    mid = r1.astype(BF16)
    lo = (r1 - mid.astype(F32)).astype(BF16)
    return hi, mid, lo


def _mlstm_kernel(q_ref, k_ref, v_ref, og_ref, gz_ref, bias_ref, gh_ref, c0_ref, n0_ref, m0_ref,
                  h_ref, c_ref, n_ref, m_ref, c_s, n_s, m_s, *, t):
    c = pl.program_id(1)

    @pl.when(c == 0)
    def _load_state():
        c_s[...] = c0_ref[...]
        n_s[...] = n0_ref[...]
        m_s[...] = m0_ref[...]

    gz = gz_ref[...] + bias_ref[...]
    lane = lax.broadcasted_iota(jnp.int32, gz.shape, 1)
    lf = jnp.minimum(gz, 0.0) - jnp.log1p(jnp.exp(-jnp.abs(gz)))
    lf = jnp.where((lane >= H_M) & (lane < 2 * H_M), lf, 0.0)
    row = lax.broadcasted_iota(jnp.int32, (t, t), 0)
    col = lax.broadcasted_iota(jnp.int32, (t, t), 1)
    causal = col <= row
    tril = jnp.where(causal, 1.0, 0.0).astype(BF16)
    bcum = sum(jnp.dot(tril, piece, preferred_element_type=F32) for piece in _split3(lf))
    gc = jnp.where(lane < H_M, gz, bcum)
    sel = jnp.where(lax.broadcasted_iota(jnp.int32, (8, GATE_LANES), 0)
                    == lax.broadcasted_iota(jnp.int32, (8, GATE_LANES), 1), 1.0, 0.0).astype(BF16)
    gr = sum(lax.dot_general(sel, piece, NT_DIMS, preferred_element_type=F32)
             for piece in _split3(gc))

    for h in range(H_M):
        sl = slice(h * DH_M, (h + 1) * DH_M)
        ig_c, b_c = gc[:, h:h + 1], gc[:, H_M + h:H_M + h + 1]
        ig_r, b_r = gr[h:h + 1, :], gr[H_M + h:H_M + h + 1, :]
        m_prev = m_s[:, h:h + 1]
        c_prev = c_s[h]
        n_prev = n_s[h:h + 1, :]
        q = q_ref[:, sl] * (DH_M ** -0.5)
        k = k_ref[:, sl]
        v = v_ref[:, sl]

        d = jnp.where(causal, b_c - b_r + ig_r, -jnp.inf)
        inter = b_c + m_prev
        m_t = jnp.maximum(inter, jnp.max(d, axis=-1, keepdims=True))
        w_intra = jnp.exp(d - m_t)
        w_inter = jnp.exp(inter - m_t)
        s = lax.dot_general(q, k, NT_DIMS, preferred_element_type=F32) * w_intra
        num = (jnp.dot(s.astype(BF16), v, preferred_element_type=F32)
               + w_inter * lax.dot_general(q, c_prev.astype(BF16), NT_DIMS,
                                           preferred_element_type=F32))
        den = (jnp.sum(s, axis=-1, keepdims=True)
               + w_inter * jnp.sum(q.astype(F32) * n_prev, axis=-1, keepdims=True))
        hh = num / jnp.maximum(jnp.abs(den), jnp.exp(-m_t))
        h_ref[:, sl] = (_rms(hh, gh_ref[h:h + 1, :]) * og_ref[:, sl].astype(F32)).astype(BF16)

        g_last = b_c[t - 1:t, :]
        logw = g_last - b_c + ig_c
        m_new = jnp.maximum(g_last + m_prev, jnp.max(logw, axis=0, keepdims=True))
        ws = jnp.exp(logw - m_new)
        wc = jnp.exp(g_last + m_prev - m_new)
        vw = (v.astype(F32) * ws).astype(BF16)
        c_s[h] = wc * c_prev + lax.dot_general(vw, k, TN_DIMS, preferred_element_type=F32)
        n_s[h:h + 1, :] = wc * n_prev + jnp.sum(k.astype(F32) * ws, axis=0, keepdims=True)
        m_s[:, h:h + 1] = m_new

    @pl.when(c == pl.num_programs(1) - 1)
    def _store_state():
        c_ref[...] = c_s[...]
        n_ref[...] = n_s[...]
        m_ref[...] = m_s[...]


def _mlstm(q, k, v, og, gz, bias, gh, c0, n0, m0):
    b, l, _ = q.shape
    t = min(MLSTM_CHUNK, l)
    nc = l // t
    seq = lambda width: pl.BlockSpec((None, t, width), lambda i, j: (i, j, 0))
    const = lambda shape: pl.BlockSpec(shape, lambda i, j: (0,) * len(shape))
    cspec = pl.BlockSpec((None, H_M, DH_M, DH_M), lambda i, j: (i, 0, 0, 0))
    nspec = pl.BlockSpec((None, H_M, DH_M), lambda i, j: (i, 0, 0))
    mspec = pl.BlockSpec((None, 1, H_M), lambda i, j: (i, 0, 0))
    return pl.pallas_call(
        functools.partial(_mlstm_kernel, t=t),
        grid=(b, nc),
        in_specs=[seq(D_MODEL), seq(D_MODEL), seq(D_MODEL), seq(D_MODEL), seq(GATE_LANES),
                  const((1, GATE_LANES)), const((H_M, DH_M)), cspec, nspec, mspec],
        out_specs=[seq(D_MODEL), cspec, nspec, mspec],
        out_shape=[jax.ShapeDtypeStruct((b, l, D_MODEL), BF16),
                   jax.ShapeDtypeStruct((b, H_M, DH_M, DH_M), F32),
                   jax.ShapeDtypeStruct((b, H_M, DH_M), F32),
                   jax.ShapeDtypeStruct((b, 1, H_M), F32)],
        scratch_shapes=[pltpu.VMEM((H_M, DH_M, DH_M), F32),
                        pltpu.VMEM((H_M, DH_M), F32),
                        pltpu.VMEM((1, H_M), F32)],
        compiler_params=pltpu.CompilerParams(
            dimension_semantics=("arbitrary", "arbitrary"), vmem_limit_bytes=V7X_VMEM_LIMIT),
        name="mlstm",
    )(q, k, v, og, gz, bias, gh, c0, n0, m0)


def _gated_gelu(x, u):
    c = -2.0 * math.sqrt(2.0 / math.pi)
    return (x * u) / (1.0 + jnp.exp(x * (c + (c * 0.044715) * (x * x))))


def _ffn_kernel(x_ref, attn_ref, hg_ref, ga_ref, gm_ref, conv0_ref, wout_ref, wup_ref, wdn_ref,
                gpm_ref, gpf_ref, gpo_ref, cw_ref, cb_ref, y_ref, cs_ref, g_s,
                *, halves, l, tiles_per_batch):
    i = pl.program_id(0)

    @pl.when(i % tiles_per_batch == 0)
    def _from_state():
        g_s[:, 6:8, :] = conv0_ref[...]

    @pl.when(i % tiles_per_batch != 0)
    def _from_prev_tile():
        g_s[:, 6:8, :] = g_s[:, l + 6:l + 8, :]

    def pre(rows):
        merged = ga_ref[rows, :] * attn_ref[rows, :] + gm_ref[rows, :] * hg_ref[rows, :]
        x1 = x_ref[rows, :] + _rms(jnp.dot(merged, wout_ref[...], preferred_element_type=F32),
                                   gpm_ref[...])
        return x1, _rms(x1, gpf_ref[...]).astype(BF16)

    def up(h2):
        return jnp.dot(h2, wup_ref[...], preferred_element_type=F32)

    def act(ug, bsl, r0, lh):
        nbh = bsl.stop - bsl.start
        g3 = ug[:, D_FF:].reshape(nbh, lh, D_FF)
        g_s[bsl, 8 + r0:8 + r0 + lh, :] = g3
        gconv = (cb_ref[...] + cw_ref[0:1, :] * g_s[bsl, 6 + r0:6 + r0 + lh, :]
                 + cw_ref[1:2, :] * g_s[bsl, 7 + r0:7 + r0 + lh, :] + cw_ref[2:3, :] * g3)
        return _gated_gelu(gconv.reshape(nbh * lh, D_FF), ug[:, :D_FF]).astype(BF16)

    def post(x1, a, rows):
        ff = jnp.dot(a, wdn_ref[...], preferred_element_type=F32)
        y_ref[rows, :] = x1 + _rms(ff, gpo_ref[...])

    (rows_a, *conv_a), (rows_b, *conv_b) = halves
    x1_a, h2_a = pre(rows_a)
    x1_b, h2_b = pre(rows_b)
    ug_a = up(h2_a)
    ug_b = up(h2_b)
    act_a = act(ug_a, *conv_a)
    post(x1_a, act_a, rows_a)
    act_b = act(ug_b, *conv_b)
    post(x1_b, act_b, rows_b)
    cs_ref[...] = g_s[:, l + 6:l + 8, :]


def _ffn(x2d, attn, hg, ga, gm, conv0, wout, wup, wdn, gpm, gpf, gpo, cw, cb, nb_total, seq):
    n = x2d.shape[0]
    if seq >= FFN_ROWS:
        tm, nb, l = FFN_ROWS, 1, FFN_ROWS
        tiles_per_batch = seq // tm
        hl = l // 2
        halves = ((slice(0, hl), slice(0, 1), 0, hl), (slice(hl, l), slice(0, 1), hl, hl))
    else:
        tm, nb, l = n, nb_total, seq
        tiles_per_batch = 1
        hb = nb // 2
        halves = ((slice(0, hb * l), slice(0, hb), 0, l), (slice(hb * l, tm), slice(hb, nb), 0, l))
    row = pl.BlockSpec((tm, D_MODEL), lambda i: (i, 0))
    cstate = pl.BlockSpec((nb, CONV_W - 1, D_FF), lambda i: (i // tiles_per_batch, 0, 0))
    return pl.pallas_call(
        functools.partial(_ffn_kernel, halves=halves, l=l, tiles_per_batch=tiles_per_batch),
        grid=(n // tm,),
        in_specs=[row, row, row, row, row, cstate,
                  _resident((D_MODEL, D_MODEL)), _resident((D_MODEL, 2 * D_FF)),
                  _resident((D_FF, D_MODEL)), _resident((1, D_MODEL)), _resident((1, D_MODEL)),
                  _resident((1, D_MODEL)), _resident((CONV_W, D_FF)), _resident((1, D_FF))],
        out_specs=[row, cstate],
        out_shape=[jax.ShapeDtypeStruct((n, D_MODEL), F32),
                   jax.ShapeDtypeStruct((nb_total, CONV_W - 1, D_FF), F32)],
        scratch_shapes=[pltpu.VMEM((nb, l + 8, D_FF), F32)],
        compiler_params=pltpu.CompilerParams(
            dimension_semantics=("arbitrary",), vmem_limit_bytes=V7X_VMEM_LIMIT),
        name="merge_ffn",
    )(x2d, attn, hg, ga, gm, conv0, wout, wup, wdn, gpm, gpf, gpo, cw, cb)


def _layer(x, caches, c0, n0, m0, conv0, wts, lam_init):
    (g_pre_mix, wa, wb, wg, gate_bias, lq1, lk1, lq2, lk2, g_attn_head, g_mlstm_head, wout, g_post_mix,
     g_pre_ffn, wup, conv_w, conv_b, wdn, g_post_ffn) = wts
    b, l, _ = x.shape
    x2d = x.reshape(b * l, D_MODEL)
    (qa, kf, kb, vf, vb, qm, km, vm, om, ga, gm, gz) = _inproj(x2d, g_pre_mix, wa, wb, wg, b, l)

    if caches is None:
        attn = _attn_prompt(qa, kb, vb, lq1, lk1, lq2, lk2, g_attn_head, lam_init)
    else:
        cache_k, cache_v, b_off = caches
        attn = _attn_sample(qa, kb, vb, cache_k, cache_v, b_off,
                            lq1, lk1, lq2, lk2, g_attn_head, lam_init)

    seq3 = lambda a: a.reshape(b, l, a.shape[-1])
    hm, c1, n1, m1 = _mlstm(seq3(qm), seq3(km), seq3(vm), seq3(om), seq3(gz), gate_bias, g_mlstm_head,
                            c0, n0, m0.reshape(b, 1, H_M))

    y, conv1 = _ffn(x2d, attn.reshape(b * l, D_MODEL), hm.reshape(b * l, D_MODEL), ga, gm,
                    conv0, wout, wup, wdn, g_post_mix, g_pre_ffn, g_post_ffn, conv_w, conv_b, b, l)
    return (y.reshape(b, l, D_MODEL), kf.reshape(b, l, H_A, DV_A), vf.reshape(b, l, H_A, DV_A),
            c1, n1, m1.reshape(b, H_M), conv1)


def kernel(x_prompt, x_sample, cache_k, cache_v, state_C, state_n, state_m, state_conv, g_pre_mix, w_in, b_gates, lam_q1, lam_k1, lam_q2, lam_k2, g_attn_head, g_mlstm_head, w_out, g_post_mix, g_pre_ffn, w_up, conv_w, conv_b, w_down, g_post_ffn):
    depth = w_in.shape[0]
    bp = x_prompt.shape[0]
    yp, ys = x_prompt, x_sample
    outs_p, outs_s = [], []
    n_main = 6 * D_MODEL
    for li in range(depth):
        lam_init = 0.8 - 0.6 * math.exp(-0.3 * li)
        w = w_in[li]
        wa = w[:, :n_main].astype(BF16)
        wb = w[:, n_main + 2 * H_M:].astype(BF16)
        wg = jnp.pad(w[:, n_main:n_main + 2 * H_M], ((0, 0), (0, GATE_LANES - 2 * H_M))).astype(BF16)
        gate_bias = jnp.concatenate(
            [b_gates[li], jnp.zeros((GATE_LANES - 2 * H_M,), F32)]).reshape(1, GATE_LANES)
        row = lambda a: a.reshape(1, -1)
        wts = (row(g_pre_mix[li]), wa, wb, wg, gate_bias, row(lam_q1[li]), row(lam_k1[li]),
               row(lam_q2[li]), row(lam_k2[li]), row(g_attn_head[li]), g_mlstm_head[li],
               w_out[li].astype(BF16), row(g_post_mix[li]), row(g_pre_ffn[li]),
               w_up[li].astype(BF16), conv_w[li], row(conv_b[li]), w_down[li].astype(BF16),
               row(g_post_ffn[li]))
        c0 = jnp.zeros((bp, H_M, DH_M, DH_M), F32)
        n0 = jnp.zeros((bp, H_M, DH_M), F32)
        m0 = jnp.zeros((bp, H_M), F32)
        conv0 = jnp.zeros((bp, CONV_W - 1, D_FF), F32)
        yp, *sp = _layer(yp, None, c0, n0, m0, conv0, wts, lam_init)
        bs, past = cache_k.shape[1], cache_k.shape[2]
        caches = (cache_k.reshape(depth * bs, past * H_A, DV_A),
                  cache_v.reshape(depth * bs, past * H_A, DV_A), li * bs)
        ys, *ss = _layer(ys, caches, state_C[li], state_n[li], state_m[li],
                         state_conv[li], wts, lam_init)
        outs_p.append(sp)
        outs_s.append(ss)
    k_p, v_p, c_p, n_p, m_p, conv_p = [jnp.stack([o[i] for o in outs_p]) for i in range(6)]
    k_s, v_s, c_s, n_s, m_s, conv_s = [jnp.stack([o[i] for o in outs_s]) for i in range(6)]
    return (yp, ys, k_p, v_p, c_p, n_p, m_p, conv_p, k_s, v_s, c_s, n_s, m_s, conv_s)
```

```python
import functools
import math

import jax
import jax.numpy as jnp
import numpy as np
from jax import lax
from jax.experimental import pallas as pl
from jax.experimental.pallas import tpu as pltpu

F32 = jnp.float32
BF16 = jnp.bfloat16

D_MODEL = 1024
CHUNK = 64
H_A = 8
DH_A = 64
DV_A = 2 * DH_A
H_M = 4
DH_M = D_MODEL // H_M
D_FF = 2816
CONV_W = 3
EPS = 1e-6
N_SEG_A = 6
GATE_LANES = 128
QK_SCALE = DH_A ** -0.5 * math.log2(math.e)

V7X_VMEM_LIMIT = 56 * 1024 * 1024

PROJ_ROWS = 256
ATTN_TILE = 512
SCORE_BOUND = 60.0
MLSTM_CHUNK = 256
FFN_ROWS = 256

NT_DIMS = (((1,), (1,)), ((), ()))
TN_DIMS = (((0,), (0,)), ((), ()))


def _rms(x, g):
    return x * lax.rsqrt(jnp.mean(x * x, axis=-1, keepdims=True) + EPS) * g


def _resident(shape):
    nd = len(shape)
    return pl.BlockSpec(shape, lambda *_: (0,) * nd, pipeline_mode=pl.Buffered(1))


def _inproj_kernel(x_ref, g_ref, wa_ref, wb_ref, wg_ref, e_ref, qa_ref, kf_ref, kb_ref, vf_ref, vb_ref,
                   qm_ref, km_ref, vm_ref, om_ref, ga_ref, gm_ref, gz_ref, qn_ref, kn_ref,
                   *, nbt, lt, transposed):
    hb = _rms(x_ref[...], g_ref[...]).astype(BF16)

    def seg(j):
        w_ref, jj = (wa_ref, j) if j < N_SEG_A else (wb_ref, j - N_SEG_A)
        return jnp.dot(hb, w_ref[:, jj * D_MODEL:(jj + 1) * D_MODEL], preferred_element_type=F32)

    def put_heads(ref, val, transpose):
        for h in range(H_A):
            piece = val[:, h * DV_A:(h + 1) * DV_A]
            if transpose:
                ref[0, h] = piece.T.astype(BF16)
            else:
                ref[:, h] = piece.astype(BF16).reshape(nbt, lt, DV_A)

    def max_sq_norm(val):
        sq = jnp.dot((val * val).astype(BF16), e_ref[...], preferred_element_type=F32)
        return jnp.max(sq, axis=0, keepdims=True)

    q = seg(0) * QK_SCALE
    put_heads(qa_ref, q, transposed)
    qn_ref[0] = max_sq_norm(q)
    k = seg(1)
    kf_ref[...] = k
    put_heads(kb_ref, k, False)
    kn_ref[0] = max_sq_norm(k)
    v = seg(2)
    vf_ref[...] = v
    put_heads(vb_ref, v, transposed)
    qm_ref[...] = seg(3).astype(BF16)
    km_ref[...] = seg(4).astype(BF16)
    vm_ref[...] = seg(5).astype(BF16)
    om_ref[...] = jax.nn.sigmoid(seg(6)).astype(BF16)
    ga_ref[...] = jax.nn.sigmoid(seg(7)).astype(BF16)
    gm_ref[...] = jax.nn.sigmoid(seg(8)).astype(BF16)
    gz_ref[...] = jnp.dot(hb, wg_ref[...], preferred_element_type=F32)


def _inproj(x2d, g, wa, wb, wg, nb, seq):
    n = x2d.shape[0]
    tm = PROJ_ROWS
    transposed = seq >= tm
    if transposed:
        nbt, lt = 1, tm
        per_b = seq // tm
        head = pl.BlockSpec((1, H_A, lt, DV_A), lambda i: (i // per_b, 0, i % per_b, 0))
        head_t = pl.BlockSpec((1, H_A, DV_A, lt), lambda i: (i // per_b, 0, 0, i % per_b))
        hm16_t = jax.ShapeDtypeStruct((nb, H_A, DV_A, seq), BF16)
    else:
        nbt, lt = tm // seq, seq
        head = head_t = pl.BlockSpec((nbt, H_A, lt, DV_A), lambda i: (i, 0, 0, 0))
        hm16_t = jax.ShapeDtypeStruct((nb, H_A, seq, DV_A), BF16)
    row = pl.BlockSpec((tm, D_MODEL), lambda i: (i, 0))
    norm = pl.BlockSpec((1, 1, GATE_LANES), lambda i: (i, 0, 0))
    norm_shape = jax.ShapeDtypeStruct((n // tm, 1, GATE_LANES), F32)
    group = (np.arange(D_MODEL)[:, None] // DH_A == np.arange(GATE_LANES)[None, :])
    full32 = jax.ShapeDtypeStruct((n, D_MODEL), F32)
    full16 = jax.ShapeDtypeStruct((n, D_MODEL), BF16)
    hm16 = jax.ShapeDtypeStruct((nb, H_A, seq, DV_A), BF16)
    return pl.pallas_call(
        functools.partial(_inproj_kernel, nbt=nbt, lt=lt, transposed=transposed),
        grid=(n // tm,),
        in_specs=[row, _resident((1, D_MODEL)), _resident(wa.shape), _resident(wb.shape),
                  _resident(wg.shape), _resident((D_MODEL, GATE_LANES))],
        out_specs=[head_t, row, head, row, head_t, row, row, row, row, row, row,
                   pl.BlockSpec((tm, GATE_LANES), lambda i: (i, 0)), norm, norm],
        out_shape=[hm16_t, full32, hm16, full32, hm16_t, full16, full16, full16,
                   full16, full16, full16, jax.ShapeDtypeStruct((n, GATE_LANES), F32),
                   norm_shape, norm_shape],
        compiler_params=pltpu.CompilerParams(
            dimension_semantics=("arbitrary",), vmem_limit_bytes=V7X_VMEM_LIMIT),
        name="inproj",
    )(x2d, g, wa, wb, wg, jnp.asarray(group, BF16))


def _lam(lq1, lk1, lq2, lk2, lam_init):
    return (jnp.exp(jnp.sum(lq1 * lk1, axis=-1, keepdims=True))
            - jnp.exp(jnp.sum(lq2 * lk2, axis=-1, keepdims=True)) + lam_init)


def _split_maps(q, axis):
    idx = lax.broadcasted_iota(jnp.int32, q.shape, axis)
    zero = jnp.zeros_like(q)
    return jnp.where(idx < DH_A, q, zero), jnp.where(idx >= DH_A, q, zero)


def _attn_prompt_kernel(qi_tab, kj_tab, qt_ref, k_ref, vt_ref, qn_ref, kn_ref, lq1_ref, lk1_ref, lq2_ref,
                        lk2_ref, gcol_ref, o_ref, qz_s, m_s, l_s, acc_s, mode_s, *, lam_init):
    p = pl.program_id(1)
    qi = qi_tab[p]
    kj = kj_tab[p]
    t = ATTN_TILE
    diag = kj == qi
    pairs = [(h, mp) for h in range(H_A) for mp in range(2)]
    n = len(pairs)

    @pl.when(kj == 0)
    def _init():
        for h in range(H_A):
            q1, q2 = _split_maps(qt_ref[h], 0)
            qz_s[0, h] = q1
            qz_s[1, h] = q2
        m_s[...] = jnp.full(m_s.shape, -jnp.inf, F32)
        l_s[...] = jnp.zeros(l_s.shape, F32)
        acc_s[...] = jnp.zeros(acc_s.shape, F32)
        bound_sq = (jnp.max(qn_ref[...], axis=0, keepdims=True)
                    * jnp.max(kn_ref[...], axis=0, keepdims=True))
        mode_s[0] = jnp.where(jnp.max(bound_sq) <= SCORE_BOUND * SCORE_BOUND, 1, 0)

    def chunk_mask():
        kr = lax.broadcasted_iota(jnp.int32, (t, t), 0) // CHUNK
        qc = lax.broadcasted_iota(jnp.int32, (t, t), 1) // CHUNK
        return kr <= qc

    def scores(h, mp, visible):
        s = jnp.dot(k_ref[h], qz_s[mp, h], preferred_element_type=F32)
        return s if visible is None else jnp.where(visible, s, -jnp.inf)

    def bounded_pass(visible):
        s_next = scores(*pairs[0], visible)
        for i, (h, mp) in enumerate(pairs):
            s = s_next
            if i + 1 < n:
                s_next = scores(*pairs[i + 1], visible)
            pr = jnp.exp2(s)
            l_s[mp, h] = l_s[mp, h] + jnp.sum(pr, axis=0, keepdims=True)
            acc_s[mp, h] = acc_s[mp, h] + jnp.dot(
                vt_ref[h], pr.astype(BF16), preferred_element_type=F32)

    def online_pass(visible):
        s_next = scores(*pairs[0], visible)
        for i, (h, mp) in enumerate(pairs):
            s = s_next
            m_old = m_s[mp, h]
            m_new = jnp.maximum(m_old, jnp.max(s, axis=0, keepdims=True))
            if i + 1 < n:
                s_next = scores(*pairs[i + 1], visible)
            alpha = jnp.exp2(m_old - m_new)
            pr = jnp.exp2(s - m_new)
            l_s[mp, h] = alpha * l_s[mp, h] + jnp.sum(pr, axis=0, keepdims=True)
            acc_s[mp, h] = alpha * acc_s[mp, h] + jnp.dot(
                vt_ref[h], pr.astype(BF16), preferred_element_type=F32)
            m_s[mp, h] = m_new

    bounded = mode_s[0] == 1
    off_diag = jnp.logical_not(diag)

    @pl.when(jnp.logical_and(bounded, off_diag))
    def _bounded_full():
        bounded_pass(None)

    @pl.when(jnp.logical_and(bounded, diag))
    def _bounded_diag():
        bounded_pass(chunk_mask())

    @pl.when(jnp.logical_and(jnp.logical_not(bounded), off_diag))
    def _online_full():
        online_pass(None)

    @pl.when(jnp.logical_and(jnp.logical_not(bounded), diag))
    def _online_diag():
        online_pass(chunk_mask())

    @pl.when(diag)
    def _finish():
        lam = _lam(lq1_ref[...], lk1_ref[...], lq2_ref[...], lk2_ref[...], lam_init)
        for h in range(H_A):
            ot = acc_s[0, h] / l_s[0, h] - lam * (acc_s[1, h] / l_s[1, h])
            ms = jnp.mean(ot * ot, axis=0, keepdims=True)
            ot = ot * lax.rsqrt(ms + EPS) * gcol_ref[...] * (1.0 - lam_init)
            o_ref[:, h * DV_A:(h + 1) * DV_A] = ot.T.astype(BF16)


def _attn_prompt(qt, k, vt, qn, kn, lq1, lk1, lq2, lk2, gh, lam_init):
    b, _, s, _ = k.shape
    t = ATTN_TILE
    nq = s // t
    pairs = [(i, j) for i in range(nq) for j in range(i + 1)]
    qi_tab = jnp.asarray(np.array([a for a, _ in pairs], np.int32))
    kj_tab = jnp.asarray(np.array([c for _, c in pairs], np.int32))
    qspec = pl.BlockSpec((None, H_A, DV_A, t), lambda bb, p, qi, kj: (bb, 0, 0, qi[p]))
    kspec = pl.BlockSpec((None, H_A, t, DV_A), lambda bb, p, qi, kj: (bb, 0, kj[p], 0))
    vspec = pl.BlockSpec((None, H_A, DV_A, t), lambda bb, p, qi, kj: (bb, 0, 0, kj[p]))
    small = lambda shape: pl.BlockSpec(shape, lambda bb, p, qi, kj: (0, 0))
    nspec = pl.BlockSpec((None,) + qn.shape[1:], lambda bb, p, qi, kj: (bb, 0, 0))
    grid_spec = pltpu.PrefetchScalarGridSpec(
        num_scalar_prefetch=2,
        grid=(b, len(pairs)),
        in_specs=[qspec, kspec, vspec, nspec, nspec, small((1, DH_A)), small((1, DH_A)), small((1, DH_A)),
                  small((1, DH_A)), small((DV_A, 1))],
        out_specs=pl.BlockSpec((None, t, H_A * DV_A), lambda bb, p, qi, kj: (bb, qi[p], 0)),
        scratch_shapes=[pltpu.VMEM((2, H_A, DV_A, t), BF16),
                        pltpu.VMEM((2, H_A, 1, t), F32),
                        pltpu.VMEM((2, H_A, 1, t), F32),
                        pltpu.VMEM((2, H_A, DV_A, t), F32),
                        pltpu.SMEM((1,), jnp.int32)],
    )
    return pl.pallas_call(
        functools.partial(_attn_prompt_kernel, lam_init=lam_init),
        grid_spec=grid_spec,
        out_shape=jax.ShapeDtypeStruct((b, s, H_A * DV_A), BF16),
        compiler_params=pltpu.CompilerParams(
            dimension_semantics=("arbitrary", "arbitrary"), vmem_limit_bytes=V7X_VMEM_LIMIT),
        name="attn_prompt",
    )(qi_tab, kj_tab, qt, k, vt, qn, kn, lq1, lk1, lq2, lk2, gh.reshape(DV_A, 1))


def _attn_sample_kernel(q_ref, kn_ref, vn_ref, ck_ref, cv_ref, lq1_ref, lk1_ref, lq2_ref, lk2_ref,
                        gh_ref, o_ref, *, lam_init):
    lam = _lam(lq1_ref[...], lk1_ref[...], lq2_ref[...], lk2_ref[...], lam_init)
    nq = q_ref.shape[1]
    past = ck_ref.shape[0] // H_A
    for h in range(H_A):
        sl = slice(h * DV_A, (h + 1) * DV_A)
        rows = pl.ds(h, past, stride=H_A)
        q1, q2 = _split_maps(q_ref[h], 1)
        qz = jnp.concatenate([q1, q2], axis=0)
        kc = ck_ref[rows, :].astype(BF16)
        sc = lax.dot_general(qz, kc, NT_DIMS, preferred_element_type=F32)
        sn = lax.dot_general(qz, kn_ref[h], NT_DIMS, preferred_element_type=F32)
        m = jnp.maximum(jnp.max(sc, axis=-1, keepdims=True), jnp.max(sn, axis=-1, keepdims=True))
        pc = jnp.exp2(sc - m)
        pn = jnp.exp2(sn - m)
        inv = 1.0 / (jnp.sum(pc, axis=-1, keepdims=True) + jnp.sum(pn, axis=-1, keepdims=True))
        pc = pc * inv
        pn = pn * inv
        ac = pc[:nq] - lam * pc[nq:]
        an = pn[:nq] - lam * pn[nq:]
        o = (jnp.dot(ac.astype(BF16), cv_ref[rows, :].astype(BF16), preferred_element_type=F32)
             + jnp.dot(an.astype(BF16), vn_ref[h], preferred_element_type=F32))
        o_ref[:, sl] = (_rms(o, gh_ref[...]) * (1.0 - lam_init)).astype(BF16)


def _attn_sample(q, kn, vn, cache_k, cache_v, b_off, lq1, lk1, lq2, lk2, gh, lam_init):
    b, _, l, _ = q.shape
    hspec = pl.BlockSpec((None, H_A, l, DV_A), lambda i: (i, 0, 0, 0))
    cspec = pl.BlockSpec((None,) + cache_k.shape[1:], lambda i: (b_off + i, 0, 0))
    small = lambda shape: pl.BlockSpec(shape, lambda i: (0, 0))
    return pl.pallas_call(
        functools.partial(_attn_sample_kernel, lam_init=lam_init),
        grid=(b,),
        in_specs=[hspec, hspec, hspec, cspec, cspec, small((1, DH_A)), small((1, DH_A)),
                  small((1, DH_A)), small((1, DH_A)), small((1, DV_A))],
        out_specs=pl.BlockSpec((None, l, H_A * DV_A), lambda i: (i, 0, 0)),
        out_shape=jax.ShapeDtypeStruct((b, l, H_A * DV_A), BF16),
        compiler_params=pltpu.CompilerParams(
            dimension_semantics=("arbitrary",), vmem_limit_bytes=V7X_VMEM_LIMIT),
        name="attn_sample",
    )(q, kn, vn, cache_k, cache_v, lq1, lk1, lq2, lk2, gh)


def _split3(x):
    hi = x.astype(BF16)
    r1 = x - hi.astype(F32)
    mid = r1.astype(BF16)
    lo = (r1 - mid.astype(F32)).astype(BF16)
    return hi, mid, lo


def _mlstm_kernel(q_ref, k_ref, v_ref, og_ref, gz_ref, bias_ref, gh_ref, c0_ref, n0_ref, m0_ref,
                  h_ref, c_ref, n_ref, m_ref, c_s, n_s, m_s, *, t):
    c = pl.program_id(1)

    @pl.when(c == 0)
    def _load_state():
        c_s[...] = c0_ref[...]
        n_s[...] = n0_ref[...]
        m_s[...] = m0_ref[...]

    gz = gz_ref[...] + bias_ref[...]
    lane = lax.broadcasted_iota(jnp.int32, gz.shape, 1)
    lf = jnp.minimum(gz, 0.0) - jnp.log1p(jnp.exp(-jnp.abs(gz)))
    lf = jnp.where((lane >= H_M) & (lane < 2 * H_M), lf, 0.0)
    row = lax.broadcasted_iota(jnp.int32, (t, t), 0)
    col = lax.broadcasted_iota(jnp.int32, (t, t), 1)
    causal = col <= row
    tril = jnp.where(causal, 1.0, 0.0).astype(BF16)
    bcum = sum(jnp.dot(tril, piece, preferred_element_type=F32) for piece in _split3(lf))
    gc = jnp.where(lane < H_M, gz, bcum)
    sel = jnp.where(lax.broadcasted_iota(jnp.int32, (8, GATE_LANES), 0)
                    == lax.broadcasted_iota(jnp.int32, (8, GATE_LANES), 1), 1.0, 0.0).astype(BF16)
    gr = sum(lax.dot_general(sel, piece, NT_DIMS, preferred_element_type=F32)
             for piece in _split3(gc))

    for h in range(H_M):
        sl = slice(h * DH_M, (h + 1) * DH_M)
        ig_c, b_c = gc[:, h:h + 1], gc[:, H_M + h:H_M + h + 1]
        ig_r, b_r = gr[h:h + 1, :], gr[H_M + h:H_M + h + 1, :]
        m_prev = m_s[:, h:h + 1]
        c_prev = c_s[h]
        n_prev = n_s[h:h + 1, :]
        q = q_ref[:, sl] * (DH_M ** -0.5)
        k = k_ref[:, sl]
        v = v_ref[:, sl]

        d = jnp.where(causal, b_c - b_r + ig_r, -jnp.inf)
        inter = b_c + m_prev
        m_t = jnp.maximum(inter, jnp.max(d, axis=-1, keepdims=True))
        w_intra = jnp.exp(d - m_t)
        w_inter = jnp.exp(inter - m_t)
        s = lax.dot_general(q, k, NT_DIMS, preferred_element_type=F32) * w_intra
        num = (jnp.dot(s.astype(BF16), v, preferred_element_type=F32)
               + w_inter * lax.dot_general(q, c_prev.astype(BF16), NT_DIMS,
                                           preferred_element_type=F32))
        den = (jnp.sum(s, axis=-1, keepdims=True)
               + w_inter * jnp.sum(q.astype(F32) * n_prev, axis=-1, keepdims=True))
        hh = num / jnp.maximum(jnp.abs(den), jnp.exp(-m_t))
        h_ref[:, sl] = (_rms(hh, gh_ref[h:h + 1, :]) * og_ref[:, sl].astype(F32)).astype(BF16)

        g_last = b_c[t - 1:t, :]
        logw = g_last - b_c + ig_c
        m_new = jnp.maximum(g_last + m_prev, jnp.max(logw, axis=0, keepdims=True))
        ws = jnp.exp(logw - m_new)
        wc = jnp.exp(g_last + m_prev - m_new)
        vw = (v.astype(F32) * ws).astype(BF16)
        c_s[h] = wc * c_prev + lax.dot_general(vw, k, TN_DIMS, preferred_element_type=F32)
        n_s[h:h + 1, :] = wc * n_prev + jnp.sum(k.astype(F32) * ws, axis=0, keepdims=True)
        m_s[:, h:h + 1] = m_new

    @pl.when(c == pl.num_programs(1) - 1)
    def _store_state():
        c_ref[...] = c_s[...]
        n_ref[...] = n_s[...]
        m_ref[...] = m_s[...]


def _mlstm(q, k, v, og, gz, bias, gh, c0, n0, m0):
    b, l, _ = q.shape
    t = min(MLSTM_CHUNK, l)
    nc = l // t
    seq = lambda width: pl.BlockSpec((None, t, width), lambda i, j: (i, j, 0))
    const = lambda shape: pl.BlockSpec(shape, lambda i, j: (0,) * len(shape))
    cspec = pl.BlockSpec((None, H_M, DH_M, DH_M), lambda i, j: (i, 0, 0, 0))
    nspec = pl.BlockSpec((None, H_M, DH_M), lambda i, j: (i, 0, 0))
    mspec = pl.BlockSpec((None, 1, H_M), lambda i, j: (i, 0, 0))
    return pl.pallas_call(
        functools.partial(_mlstm_kernel, t=t),
        grid=(b, nc),
        in_specs=[seq(D_MODEL), seq(D_MODEL), seq(D_MODEL), seq(D_MODEL), seq(GATE_LANES),
                  const((1, GATE_LANES)), const((H_M, DH_M)), cspec, nspec, mspec],
        out_specs=[seq(D_MODEL), cspec, nspec, mspec],
        out_shape=[jax.ShapeDtypeStruct((b, l, D_MODEL), BF16),
                   jax.ShapeDtypeStruct((b, H_M, DH_M, DH_M), F32),
                   jax.ShapeDtypeStruct((b, H_M, DH_M), F32),
                   jax.ShapeDtypeStruct((b, 1, H_M), F32)],
        scratch_shapes=[pltpu.VMEM((H_M, DH_M, DH_M), F32),
                        pltpu.VMEM((H_M, DH_M), F32),
                        pltpu.VMEM((1, H_M), F32)],
        compiler_params=pltpu.CompilerParams(
            dimension_semantics=("arbitrary", "arbitrary"), vmem_limit_bytes=V7X_VMEM_LIMIT),
        name="mlstm",
    )(q, k, v, og, gz, bias, gh, c0, n0, m0)


def _gated_gelu(x, u):
    c = -2.0 * math.sqrt(2.0 / math.pi)
    return (x * u) / (1.0 + jnp.exp(x * (c + (c * 0.044715) * (x * x))))


def _ffn_kernel(x_ref, attn_ref, hg_ref, ga_ref, gm_ref, conv0_ref, wout_ref, wup_ref, wdn_ref,
                gpm_ref, gpf_ref, gpo_ref, cw_ref, cb_ref, y_ref, cs_ref, g_s,
                *, halves, l, tiles_per_batch):
    i = pl.program_id(0)

    @pl.when(i % tiles_per_batch == 0)
    def _from_state():
        g_s[:, 6:8, :] = conv0_ref[...]

    @pl.when(i % tiles_per_batch != 0)
    def _from_prev_tile():
        g_s[:, 6:8, :] = g_s[:, l + 6:l + 8, :]

    def pre(rows):
        merged = ga_ref[rows, :] * attn_ref[rows, :] + gm_ref[rows, :] * hg_ref[rows, :]
        x1 = x_ref[rows, :] + _rms(jnp.dot(merged, wout_ref[...], preferred_element_type=F32),
                                   gpm_ref[...])
        return x1, _rms(x1, gpf_ref[...]).astype(BF16)

    def up(h2):
        return jnp.dot(h2, wup_ref[...], preferred_element_type=F32)

    def act(ug, bsl, r0, lh):
        nbh = bsl.stop - bsl.start
        g3 = ug[:, D_FF:].reshape(nbh, lh, D_FF)
        g_s[bsl, 8 + r0:8 + r0 + lh, :] = g3
        gconv = (cb_ref[...] + cw_ref[0:1, :] * g_s[bsl, 6 + r0:6 + r0 + lh, :]
                 + cw_ref[1:2, :] * g_s[bsl, 7 + r0:7 + r0 + lh, :] + cw_ref[2:3, :] * g3)
        return _gated_gelu(gconv.reshape(nbh * lh, D_FF), ug[:, :D_FF]).astype(BF16)

    def post(x1, a, rows):
        ff = jnp.dot(a, wdn_ref[...], preferred_element_type=F32)
        y_ref[rows, :] = x1 + _rms(ff, gpo_ref[...])

    (rows_a, *conv_a), (rows_b, *conv_b) = halves
    x1_a, h2_a = pre(rows_a)
    x1_b, h2_b = pre(rows_b)
    ug_a = up(h2_a)
    ug_b = up(h2_b)
    act_a = act(ug_a, *conv_a)
    post(x1_a, act_a, rows_a)
    act_b = act(ug_b, *conv_b)
    post(x1_b, act_b, rows_b)
    cs_ref[...] = g_s[:, l + 6:l + 8, :]


def _ffn(x2d, attn, hg, ga, gm, conv0, wout, wup, wdn, gpm, gpf, gpo, cw, cb, nb_total, seq):
    n = x2d.shape[0]
    if seq >= FFN_ROWS:
        tm, nb, l = FFN_ROWS, 1, FFN_ROWS
        tiles_per_batch = seq // tm
        hl = l // 2
        halves = ((slice(0, hl), slice(0, 1), 0, hl), (slice(hl, l), slice(0, 1), hl, hl))
    else:
        tm, nb, l = n, nb_total, seq
        tiles_per_batch = 1
        hb = nb // 2
        halves = ((slice(0, hb * l), slice(0, hb), 0, l), (slice(hb * l, tm), slice(hb, nb), 0, l))
    row = pl.BlockSpec((tm, D_MODEL), lambda i: (i, 0))
    cstate = pl.BlockSpec((nb, CONV_W - 1, D_FF), lambda i: (i // tiles_per_batch, 0, 0))
    return pl.pallas_call(
        functools.partial(_ffn_kernel, halves=halves, l=l, tiles_per_batch=tiles_per_batch),
        grid=(n // tm,),
        in_specs=[row, row, row, row, row, cstate,
                  _resident((D_MODEL, D_MODEL)), _resident((D_MODEL, 2 * D_FF)),
                  _resident((D_FF, D_MODEL)), _resident((1, D_MODEL)), _resident((1, D_MODEL)),
                  _resident((1, D_MODEL)), _resident((CONV_W, D_FF)), _resident((1, D_FF))],
        out_specs=[row, cstate],
        out_shape=[jax.ShapeDtypeStruct((n, D_MODEL), F32),
                   jax.ShapeDtypeStruct((nb_total, CONV_W - 1, D_FF), F32)],
        scratch_shapes=[pltpu.VMEM((nb, l + 8, D_FF), F32)],
        compiler_params=pltpu.CompilerParams(
            dimension_semantics=("arbitrary",), vmem_limit_bytes=V7X_VMEM_LIMIT),
        name="merge_ffn",
    )(x2d, attn, hg, ga, gm, conv0, wout, wup, wdn, gpm, gpf, gpo, cw, cb)


def _layer(x, caches, c0, n0, m0, conv0, wts, lam_init):
    (g_pre_mix, wa, wb, wg, gate_bias, lq1, lk1, lq2, lk2, g_attn_head, g_mlstm_head, wout, g_post_mix,
     g_pre_ffn, wup, conv_w, conv_b, wdn, g_post_ffn) = wts
    b, l, _ = x.shape
    x2d = x.reshape(b * l, D_MODEL)
    (qa, kf, kb, vf, vb, qm, km, vm, om, ga, gm, gz, qn, kn) = _inproj(
        x2d, g_pre_mix, wa, wb, wg, b, l)

    if caches is None:
        per_b = qn.shape[0] // b
        attn = _attn_prompt(qa, kb, vb, qn.reshape(b, per_b, GATE_LANES), kn.reshape(b, per_b, GATE_LANES),
                            lq1, lk1, lq2, lk2, g_attn_head, lam_init)
    else:
        cache_k, cache_v, b_off = caches
        attn = _attn_sample(qa, kb, vb, cache_k, cache_v, b_off,
                            lq1, lk1, lq2, lk2, g_attn_head, lam_init)

    seq3 = lambda a: a.reshape(b, l, a.shape[-1])
    hm, c1, n1, m1 = _mlstm(seq3(qm), seq3(km), seq3(vm), seq3(om), seq3(gz), gate_bias, g_mlstm_head,
                            c0, n0, m0.reshape(b, 1, H_M))

    y, conv1 = _ffn(x2d, attn.reshape(b * l, D_MODEL), hm.reshape(b * l, D_MODEL), ga, gm,
                    conv0, wout, wup, wdn, g_post_mix, g_pre_ffn, g_post_ffn, conv_w, conv_b, b, l)
    return (y.reshape(b, l, D_MODEL), kf.reshape(b, l, H_A, DV_A), vf.reshape(b, l, H_A, DV_A),
            c1, n1, m1.reshape(b, H_M), conv1)


def kernel(x_prompt, x_sample, cache_k, cache_v, state_C, state_n, state_m, state_conv, g_pre_mix, w_in, b_gates, lam_q1, lam_k1, lam_q2, lam_k2, g_attn_head, g_mlstm_head, w_out, g_post_mix, g_pre_ffn, w_up, conv_w, conv_b, w_down, g_post_ffn):
    depth = w_in.shape[0]
    bp = x_prompt.shape[0]
    yp, ys = x_prompt, x_sample
    outs_p, outs_s = [], []
    n_main = 6 * D_MODEL
    for li in range(depth):
        lam_init = 0.8 - 0.6 * math.exp(-0.3 * li)
        w = w_in[li]
        wa = w[:, :n_main].astype(BF16)
        wb = w[:, n_main + 2 * H_M:].astype(BF16)
        wg = jnp.pad(w[:, n_main:n_main + 2 * H_M], ((0, 0), (0, GATE_LANES - 2 * H_M))).astype(BF16)
        gate_bias = jnp.concatenate(
            [b_gates[li], jnp.zeros((GATE_LANES - 2 * H_M,), F32)]).reshape(1, GATE_LANES)
        row = lambda a: a.reshape(1, -1)
        wts = (row(g_pre_mix[li]), wa, wb, wg, gate_bias, row(lam_q1[li]), row(lam_k1[li]),
               row(lam_q2[li]), row(lam_k2[li]), row(g_attn_head[li]), g_mlstm_head[li],
               w_out[li].astype(BF16), row(g_post_mix[li]), row(g_pre_ffn[li]),
               w_up[li].astype(BF16), conv_w[li], row(conv_b[li]), w_down[li].astype(BF16),
               row(g_post_ffn[li]))
        c0 = jnp.zeros((bp, H_M, DH_M, DH_M), F32)
        n0 = jnp.zeros((bp, H_M, DH_M), F32)
        m0 = jnp.zeros((bp, H_M), F32)
        conv0 = jnp.zeros((bp, CONV_W - 1, D_FF), F32)
        yp, *sp = _layer(yp, None, c0, n0, m0, conv0, wts, lam_init)
        bs, past = cache_k.shape[1], cache_k.shape[2]
        caches = (cache_k.reshape(depth * bs, past * H_A, DV_A),
                  cache_v.reshape(depth * bs, past * H_A, DV_A), li * bs)
        ys, *ss = _layer(ys, caches, state_C[li], state_n[li], state_m[li],
                         state_conv[li], wts, lam_init)
        outs_p.append(sp)
        outs_s.append(ss)
    k_p, v_p, c_p, n_p, m_p, conv_p = [jnp.stack([o[i] for o in outs_p]) for i in range(6)]
    k_s, v_s, c_s, n_s, m_s, conv_s = [jnp.stack([o[i] for o in outs_s]) for i in range(6)]
    return (yp, ys, k_p, v_p, c_p, n_p, m_p, conv_p, k_s, v_s, c_s, n_s, m_s, conv_s)
```

```python
import functools
import math

import jax
import jax.numpy as jnp
import numpy as np
from jax import lax
from jax.experimental import pallas as pl
from jax.experimental.pallas import tpu as pltpu

F32 = jnp.float32
BF16 = jnp.bfloat16

D_MODEL = 1024
CHUNK = 64
H_A = 8
DH_A = 64
DV_A = 2 * DH_A
H_M = 4
DH_M = D_MODEL // H_M
D_FF = 2816
CONV_W = 3
EPS = 1e-6
N_SEG_A = 6
GATE_LANES = 128
QK_SCALE = DH_A ** -0.5 * math.log2(math.e)

V7X_VMEM_LIMIT = 56 * 1024 * 1024

PROJ_ROWS = 256
ATTN_TILE = 512
SCORE_BOUND = 60.0
MLSTM_CHUNK = 256
FFN_ROWS = 512
FFN_PARTS = 4

NT_DIMS = (((1,), (1,)), ((), ()))
TN_DIMS = (((0,), (0,)), ((), ()))


def _rms(x, g):
    return x * lax.rsqrt(jnp.mean(x * x, axis=-1, keepdims=True) + EPS) * g


def _resident(shape):
    nd = len(shape)
    return pl.BlockSpec(shape, lambda *_: (0,) * nd, pipeline_mode=pl.Buffered(1))


def _inproj_kernel(x_ref, g_ref, wa_ref, wb_ref, wg_ref, e_ref, qa_ref, kf_ref, kb_ref, vf_ref, vb_ref,
                   qm_ref, km_ref, vm_ref, om_ref, ga_ref, gm_ref, gz_ref, qn_ref, kn_ref,
                   *, nbt, lt, transposed):
    hb = _rms(x_ref[...], g_ref[...]).astype(BF16)

    def seg(j):
        w_ref, jj = (wa_ref, j) if j < N_SEG_A else (wb_ref, j - N_SEG_A)
        return jnp.dot(hb, w_ref[:, jj * D_MODEL:(jj + 1) * D_MODEL], preferred_element_type=F32)

    def put_heads(ref, val, transpose):
        for h in range(H_A):
            piece = val[:, h * DV_A:(h + 1) * DV_A]
            if transpose:
                ref[0, h] = piece.T.astype(BF16)
            else:
                ref[:, h] = piece.astype(BF16).reshape(nbt, lt, DV_A)

    def max_sq_norm(val):
        sq = jnp.dot((val * val).astype(BF16), e_ref[...], preferred_element_type=F32)
        return jnp.max(sq, axis=0, keepdims=True)

    q = seg(0) * QK_SCALE
    put_heads(qa_ref, q, transposed)
    qn_ref[0] = max_sq_norm(q)
    k = seg(1)
    kf_ref[...] = k
    put_heads(kb_ref, k, False)
    kn_ref[0] = max_sq_norm(k)
    v = seg(2)
    vf_ref[...] = v
    put_heads(vb_ref, v, transposed)
    qm_ref[...] = (seg(3) * DH_M ** -0.5).astype(BF16)
    km_ref[...] = seg(4).astype(BF16)
    vm_ref[...] = seg(5).astype(BF16)
    om_ref[...] = jax.nn.sigmoid(seg(6)).astype(BF16)
    ga_ref[...] = jax.nn.sigmoid(seg(7)).astype(BF16)
    gm_ref[...] = jax.nn.sigmoid(seg(8)).astype(BF16)
    gz_ref[...] = jnp.dot(hb, wg_ref[...], preferred_element_type=F32)


def _inproj(x2d, g, wa, wb, wg, nb, seq):
    n = x2d.shape[0]
    tm = PROJ_ROWS
    transposed = seq >= tm
    if transposed:
        nbt, lt = 1, tm
        per_b = seq // tm
        head = pl.BlockSpec((1, H_A, lt, DV_A), lambda i: (i // per_b, 0, i % per_b, 0))
        head_t = pl.BlockSpec((1, H_A, DV_A, lt), lambda i: (i // per_b, 0, 0, i % per_b))
        hm16_t = jax.ShapeDtypeStruct((nb, H_A, DV_A, seq), BF16)
    else:
        nbt, lt = tm // seq, seq
        head = head_t = pl.BlockSpec((nbt, H_A, lt, DV_A), lambda i: (i, 0, 0, 0))
        hm16_t = jax.ShapeDtypeStruct((nb, H_A, seq, DV_A), BF16)
    row = pl.BlockSpec((tm, D_MODEL), lambda i: (i, 0))
    norm = pl.BlockSpec((1, 1, GATE_LANES), lambda i: (i, 0, 0))
    norm_shape = jax.ShapeDtypeStruct((n // tm, 1, GATE_LANES), F32)
    group = (np.arange(D_MODEL)[:, None] // DH_A == np.arange(GATE_LANES)[None, :])
    full32 = jax.ShapeDtypeStruct((n, D_MODEL), F32)
    full16 = jax.ShapeDtypeStruct((n, D_MODEL), BF16)
    hm16 = jax.ShapeDtypeStruct((nb, H_A, seq, DV_A), BF16)
    return pl.pallas_call(
        functools.partial(_inproj_kernel, nbt=nbt, lt=lt, transposed=transposed),
        grid=(n // tm,),
        in_specs=[row, _resident((1, D_MODEL)), _resident(wa.shape), _resident(wb.shape),
                  _resident(wg.shape), _resident((D_MODEL, GATE_LANES))],
        out_specs=[head_t, row, head, row, head_t, row, row, row, row, row, row,
                   pl.BlockSpec((tm, GATE_LANES), lambda i: (i, 0)), norm, norm],
        out_shape=[hm16_t, full32, hm16, full32, hm16_t, full16, full16, full16,
                   full16, full16, full16, jax.ShapeDtypeStruct((n, GATE_LANES), F32),
                   norm_shape, norm_shape],
        compiler_params=pltpu.CompilerParams(
            dimension_semantics=("arbitrary",), vmem_limit_bytes=V7X_VMEM_LIMIT),
        name="inproj",
    )(x2d, g, wa, wb, wg, jnp.asarray(group, BF16))


def _lam(lq1, lk1, lq2, lk2, lam_init):
    return (jnp.exp(jnp.sum(lq1 * lk1, axis=-1, keepdims=True))
            - jnp.exp(jnp.sum(lq2 * lk2, axis=-1, keepdims=True)) + lam_init)


def _split_maps(q, axis):
    idx = lax.broadcasted_iota(jnp.int32, q.shape, axis)
    zero = jnp.zeros_like(q)
    return jnp.where(idx < DH_A, q, zero), jnp.where(idx >= DH_A, q, zero)


def _attn_prompt_kernel(qi_tab, kj_tab, qt_ref, k_ref, vt_ref, qn_ref, kn_ref, lq1_ref, lk1_ref, lq2_ref,
                        lk2_ref, gcol_ref, o_ref, qz_s, m_s, l_s, acc_s, mode_s, *, lam_init):
    p = pl.program_id(1)
    qi = qi_tab[p]
    kj = kj_tab[p]
    t = ATTN_TILE
    diag = kj == qi
    pairs = [(h, mp) for h in range(H_A) for mp in range(2)]
    n = len(pairs)

    @pl.when(kj == 0)
    def _init():
        for h in range(H_A):
            q1, q2 = _split_maps(qt_ref[h], 0)
            qz_s[0, h] = q1
            qz_s[1, h] = q2
        m_s[...] = jnp.full(m_s.shape, -jnp.inf, F32)
        l_s[...] = jnp.zeros(l_s.shape, F32)
        acc_s[...] = jnp.zeros(acc_s.shape, F32)
        bound_sq = (jnp.max(qn_ref[...], axis=0, keepdims=True)
                    * jnp.max(kn_ref[...], axis=0, keepdims=True))
        mode_s[0] = jnp.where(jnp.max(bound_sq) <= SCORE_BOUND * SCORE_BOUND, 1, 0)

    def chunk_mask():
        kr = lax.broadcasted_iota(jnp.int32, (t, t), 0) // CHUNK
        qc = lax.broadcasted_iota(jnp.int32, (t, t), 1) // CHUNK
        return kr <= qc

    def scores(h, mp, visible):
        s = jnp.dot(k_ref[h], qz_s[mp, h], preferred_element_type=F32)
        return s if visible is None else jnp.where(visible, s, -jnp.inf)

    def bounded_pass(visible):
        s_next = scores(*pairs[0], visible)
        for i, (h, mp) in enumerate(pairs):
            s = s_next
            if i + 1 < n:
                s_next = scores(*pairs[i + 1], visible)
            pr = jnp.exp2(s)
            l_s[mp, h] = l_s[mp, h] + jnp.sum(pr, axis=0, keepdims=True)
            acc_s[mp, h] = acc_s[mp, h] + jnp.dot(
                vt_ref[h], pr.astype(BF16), preferred_element_type=F32)

    def online_pass(visible):
        s_next = scores(*pairs[0], visible)
        for i, (h, mp) in enumerate(pairs):
            s = s_next
            m_old = m_s[mp, h]
            m_new = jnp.maximum(m_old, jnp.max(s, axis=0, keepdims=True))
            if i + 1 < n:
                s_next = scores(*pairs[i + 1], visible)
            alpha = jnp.exp2(m_old - m_new)
            pr = jnp.exp2(s - m_new)
            l_s[mp, h] = alpha * l_s[mp, h] + jnp.sum(pr, axis=0, keepdims=True)
            acc_s[mp, h] = alpha * acc_s[mp, h] + jnp.dot(
                vt_ref[h], pr.astype(BF16), preferred_element_type=F32)
            m_s[mp, h] = m_new

    bounded = mode_s[0] == 1
    off_diag = jnp.logical_not(diag)

    @pl.when(jnp.logical_and(bounded, off_diag))
    def _bounded_full():
        bounded_pass(None)

    @pl.when(jnp.logical_and(bounded, diag))
    def _bounded_diag():
        bounded_pass(chunk_mask())

    @pl.when(jnp.logical_and(jnp.logical_not(bounded), off_diag))
    def _online_full():
        online_pass(None)

    @pl.when(jnp.logical_and(jnp.logical_not(bounded), diag))
    def _online_diag():
        online_pass(chunk_mask())

    @pl.when(diag)
    def _finish():
        lam = _lam(lq1_ref[...], lk1_ref[...], lq2_ref[...], lk2_ref[...], lam_init)
        for h in range(H_A):
            ot = acc_s[0, h] / l_s[0, h] - lam * (acc_s[1, h] / l_s[1, h])
            ms = jnp.mean(ot * ot, axis=0, keepdims=True)
            ot = ot * lax.rsqrt(ms + EPS) * gcol_ref[...] * (1.0 - lam_init)
            o_ref[:, h * DV_A:(h + 1) * DV_A] = ot.T.astype(BF16)


def _attn_prompt(qt, k, vt, qn, kn, lq1, lk1, lq2, lk2, gh, lam_init):
    b, _, s, _ = k.shape
    t = ATTN_TILE
    nq = s // t
    pairs = [(i, j) for i in range(nq) for j in range(i + 1)]
    qi_tab = jnp.asarray(np.array([a for a, _ in pairs], np.int32))
    kj_tab = jnp.asarray(np.array([c for _, c in pairs], np.int32))
    qspec = pl.BlockSpec((None, H_A, DV_A, t), lambda bb, p, qi, kj: (bb, 0, 0, qi[p]))
    kspec = pl.BlockSpec((None, H_A, t, DV_A), lambda bb, p, qi, kj: (bb, 0, kj[p], 0))
    vspec = pl.BlockSpec((None, H_A, DV_A, t), lambda bb, p, qi, kj: (bb, 0, 0, kj[p]))
    small = lambda shape: pl.BlockSpec(shape, lambda bb, p, qi, kj: (0, 0))
    nspec = pl.BlockSpec((None,) + qn.shape[1:], lambda bb, p, qi, kj: (bb, 0, 0))
    grid_spec = pltpu.PrefetchScalarGridSpec(
        num_scalar_prefetch=2,
        grid=(b, len(pairs)),
        in_specs=[qspec, kspec, vspec, nspec, nspec, small((1, DH_A)), small((1, DH_A)), small((1, DH_A)),
                  small((1, DH_A)), small((DV_A, 1))],
        out_specs=pl.BlockSpec((None, t, H_A * DV_A), lambda bb, p, qi, kj: (bb, qi[p], 0)),
        scratch_shapes=[pltpu.VMEM((2, H_A, DV_A, t), BF16),
                        pltpu.VMEM((2, H_A, 1, t), F32),
                        pltpu.VMEM((2, H_A, 1, t), F32),
                        pltpu.VMEM((2, H_A, DV_A, t), F32),
                        pltpu.SMEM((1,), jnp.int32)],
    )
    return pl.pallas_call(
        functools.partial(_attn_prompt_kernel, lam_init=lam_init),
        grid_spec=grid_spec,
        out_shape=jax.ShapeDtypeStruct((b, s, H_A * DV_A), BF16),
        compiler_params=pltpu.CompilerParams(
            dimension_semantics=("arbitrary", "arbitrary"), vmem_limit_bytes=V7X_VMEM_LIMIT),
        name="attn_prompt",
    )(qi_tab, kj_tab, qt, k, vt, qn, kn, lq1, lk1, lq2, lk2, gh.reshape(DV_A, 1))


def _attn_sample_kernel(q_ref, kn_ref, vn_ref, ck_ref, cv_ref, lq1_ref, lk1_ref, lq2_ref, lk2_ref,
                        gh_ref, o_ref, *, lam_init):
    lam = _lam(lq1_ref[...], lk1_ref[...], lq2_ref[...], lk2_ref[...], lam_init)
    nq = q_ref.shape[1]
    past = ck_ref.shape[0] // H_A
    for h in range(H_A):
        sl = slice(h * DV_A, (h + 1) * DV_A)
        rows = pl.ds(h, past, stride=H_A)
        q1, q2 = _split_maps(q_ref[h], 1)
        qz = jnp.concatenate([q1, q2], axis=0)
        kc = ck_ref[rows, :].astype(BF16)
        sc = lax.dot_general(qz, kc, NT_DIMS, preferred_element_type=F32)
        sn = lax.dot_general(qz, kn_ref[h], NT_DIMS, preferred_element_type=F32)
        m = jnp.maximum(jnp.max(sc, axis=-1, keepdims=True), jnp.max(sn, axis=-1, keepdims=True))
        pc = jnp.exp2(sc - m)
        pn = jnp.exp2(sn - m)
        inv = 1.0 / (jnp.sum(pc, axis=-1, keepdims=True) + jnp.sum(pn, axis=-1, keepdims=True))
        pc = pc * inv
        pn = pn * inv
        ac = pc[:nq] - lam * pc[nq:]
        an = pn[:nq] - lam * pn[nq:]
        o = (jnp.dot(ac.astype(BF16), cv_ref[rows, :].astype(BF16), preferred_element_type=F32)
             + jnp.dot(an.astype(BF16), vn_ref[h], preferred_element_type=F32))
        o_ref[:, sl] = (_rms(o, gh_ref[...]) * (1.0 - lam_init)).astype(BF16)


def _attn_sample(q, kn, vn, cache_k, cache_v, b_off, lq1, lk1, lq2, lk2, gh, lam_init):
    b, _, l, _ = q.shape
    hspec = pl.BlockSpec((None, H_A, l, DV_A), lambda i: (i, 0, 0, 0))
    cspec = pl.BlockSpec((None,) + cache_k.shape[1:], lambda i: (b_off + i, 0, 0))
    small = lambda shape: pl.BlockSpec(shape, lambda i: (0, 0))
    return pl.pallas_call(
        functools.partial(_attn_sample_kernel, lam_init=lam_init),
        grid=(b,),
        in_specs=[hspec, hspec, hspec, cspec, cspec, small((1, DH_A)), small((1, DH_A)),
                  small((1, DH_A)), small((1, DH_A)), small((1, DV_A))],
        out_specs=pl.BlockSpec((None, l, H_A * DV_A), lambda i: (i, 0, 0)),
        out_shape=jax.ShapeDtypeStruct((b, l, H_A * DV_A), BF16),
        compiler_params=pltpu.CompilerParams(
            dimension_semantics=("arbitrary",), vmem_limit_bytes=V7X_VMEM_LIMIT),
        name="attn_sample",
    )(q, kn, vn, cache_k, cache_v, lq1, lk1, lq2, lk2, gh)


def _split3(x):
    hi = x.astype(BF16)
    r1 = x - hi.astype(F32)
    mid = r1.astype(BF16)
    lo = (r1 - mid.astype(F32)).astype(BF16)
    return hi, mid, lo


def _mlstm_kernel(q_ref, k_ref, v_ref, og_ref, gz_ref, bias_ref, gh_ref, c0_ref, n0_ref, m0_ref,
                  h_ref, c_ref, n_ref, m_ref, c_s, n_s, m_s, *, t):
    c = pl.program_id(1)

    @pl.when(c == 0)
    def _load_state():
        c_s[...] = c0_ref[...]
        n_s[...] = n0_ref[...]
        m_s[...] = m0_ref[...]

    gz = gz_ref[...] + bias_ref[...]
    lane = lax.broadcasted_iota(jnp.int32, gz.shape, 1)
    lf = jnp.minimum(gz, 0.0) - jnp.log1p(jnp.exp(-jnp.abs(gz)))
    lf = jnp.where((lane >= H_M) & (lane < 2 * H_M), lf, 0.0)
    row = lax.broadcasted_iota(jnp.int32, (t, t), 0)
    col = lax.broadcasted_iota(jnp.int32, (t, t), 1)
    causal = col <= row
    tril = jnp.where(causal, 1.0, 0.0).astype(BF16)
    bcum = sum(jnp.dot(tril, piece, preferred_element_type=F32) for piece in _split3(lf))
    gc = jnp.where(lane < H_M, gz, bcum)
    sel = jnp.where(lax.broadcasted_iota(jnp.int32, (8, GATE_LANES), 0)
                    == lax.broadcasted_iota(jnp.int32, (8, GATE_LANES), 1), 1.0, 0.0).astype(BF16)
    gr = sum(lax.dot_general(sel, piece, NT_DIMS, preferred_element_type=F32)
             for piece in _split3(gc))

    heads = range(H_M)
    sls = [slice(h * DH_M, (h + 1) * DH_M) for h in heads]
    ig_c = [gc[:, h:h + 1] for h in heads]
    b_c = [gc[:, H_M + h:H_M + h + 1] for h in heads]
    ig_r = [gr[h:h + 1, :] for h in heads]
    b_r = [gr[H_M + h:H_M + h + 1, :] for h in heads]
    m_prev = [m_s[:, h:h + 1] for h in heads]
    c_prev = [c_s[h] for h in heads]
    n_prev = [n_s[h:h + 1, :] for h in heads]
    q = [q_ref[:, sl] for sl in sls]
    k = [k_ref[:, sl] for sl in sls]
    v = [v_ref[:, sl] for sl in sls]

    qk = [lax.dot_general(q[h], k[h], NT_DIMS, preferred_element_type=F32) for h in heads]
    qc = [lax.dot_general(q[h], c_prev[h].astype(BF16), NT_DIMS, preferred_element_type=F32)
          for h in heads]
    d = [jnp.where(causal, b_c[h] - b_r[h] + ig_r[h], -jnp.inf) for h in heads]
    inter = [b_c[h] + m_prev[h] for h in heads]
    m_t = [jnp.maximum(inter[h], jnp.max(d[h], axis=-1, keepdims=True)) for h in heads]
    w_inter = [jnp.exp(inter[h] - m_t[h]) for h in heads]
    s = [qk[h] * jnp.exp(d[h] - m_t[h]) for h in heads]
    num = [jnp.dot(s[h].astype(BF16), v[h], preferred_element_type=F32) + w_inter[h] * qc[h]
           for h in heads]
    den = [jnp.sum(s[h], axis=-1, keepdims=True)
           + w_inter[h] * jnp.sum(q[h].astype(F32) * n_prev[h], axis=-1, keepdims=True) for h in heads]
    hh = [num[h] / jnp.maximum(jnp.abs(den[h]), jnp.exp(-m_t[h])) for h in heads]
    for h in heads:
        h_ref[:, sls[h]] = (_rms(hh[h], gh_ref[h:h + 1, :])
                            * og_ref[:, sls[h]].astype(F32)).astype(BF16)

    g_last = [b_c[h][t - 1:t, :] for h in heads]
    logw = [g_last[h] - b_c[h] + ig_c[h] for h in heads]
    m_new = [jnp.maximum(g_last[h] + m_prev[h], jnp.max(logw[h], axis=0, keepdims=True)) for h in heads]
    ws = [jnp.exp(logw[h] - m_new[h]) for h in heads]
    wc = [jnp.exp(g_last[h] + m_prev[h] - m_new[h]) for h in heads]
    vw = [(v[h].astype(F32) * ws[h]).astype(BF16) for h in heads]
    for h in heads:
        c_s[h] = wc[h] * c_prev[h] + lax.dot_general(vw[h], k[h], TN_DIMS, preferred_element_type=F32)
        n_s[h:h + 1, :] = wc[h] * n_prev[h] + jnp.sum(k[h].astype(F32) * ws[h], axis=0, keepdims=True)
        m_s[:, h:h + 1] = m_new[h]

    @pl.when(c == pl.num_programs(1) - 1)
    def _store_state():
        c_ref[...] = c_s[...]
        n_ref[...] = n_s[...]
        m_ref[...] = m_s[...]


def _mlstm(q, k, v, og, gz, bias, gh, c0, n0, m0):
    b, l, _ = q.shape
    t = min(MLSTM_CHUNK, l)
    nc = l // t
    seq = lambda width: pl.BlockSpec((None, t, width), lambda i, j: (i, j, 0))
    const = lambda shape: pl.BlockSpec(shape, lambda i, j: (0,) * len(shape))
    cspec = pl.BlockSpec((None, H_M, DH_M, DH_M), lambda i, j: (i, 0, 0, 0))
    nspec = pl.BlockSpec((None, H_M, DH_M), lambda i, j: (i, 0, 0))
    mspec = pl.BlockSpec((None, 1, H_M), lambda i, j: (i, 0, 0))
    return pl.pallas_call(
        functools.partial(_mlstm_kernel, t=t),
        grid=(b, nc),
        in_specs=[seq(D_MODEL), seq(D_MODEL), seq(D_MODEL), seq(D_MODEL), seq(GATE_LANES),
                  const((1, GATE_LANES)), const((H_M, DH_M)), cspec, nspec, mspec],
        out_specs=[seq(D_MODEL), cspec, nspec, mspec],
        out_shape=[jax.ShapeDtypeStruct((b, l, D_MODEL), BF16),
                   jax.ShapeDtypeStruct((b, H_M, DH_M, DH_M), F32),
                   jax.ShapeDtypeStruct((b, H_M, DH_M), F32),
                   jax.ShapeDtypeStruct((b, 1, H_M), F32)],
        scratch_shapes=[pltpu.VMEM((H_M, DH_M, DH_M), F32),
                        pltpu.VMEM((H_M, DH_M), F32),
                        pltpu.VMEM((1, H_M), F32)],
        compiler_params=pltpu.CompilerParams(
            dimension_semantics=("arbitrary", "arbitrary"), vmem_limit_bytes=V7X_VMEM_LIMIT),
        name="mlstm",
    )(q, k, v, og, gz, bias, gh, c0, n0, m0)


def _gated_gelu(x, u):
    c = -2.0 * math.sqrt(2.0 / math.pi)
    return (x * u) / (1.0 + jnp.exp(x * (c + (c * 0.044715) * (x * x))))


def _ffn_kernel(x_ref, attn_ref, hg_ref, ga_ref, gm_ref, conv0_ref, wout_ref, wup_ref, wdn_ref,
                gpm_ref, gpf_ref, gpo_ref, cw_ref, cb_ref, y_ref, cs_ref, g_s,
                *, parts, l, tiles_per_batch):
    i = pl.program_id(0)

    @pl.when(i % tiles_per_batch == 0)
    def _from_state():
        g_s[:, 6:8, :] = conv0_ref[...]

    @pl.when(i % tiles_per_batch != 0)
    def _from_prev_tile():
        g_s[:, 6:8, :] = g_s[:, l + 6:l + 8, :]

    def pre(rows):
        merged = ga_ref[rows, :] * attn_ref[rows, :] + gm_ref[rows, :] * hg_ref[rows, :]
        x1 = x_ref[rows, :] + _rms(jnp.dot(merged, wout_ref[...], preferred_element_type=F32),
                                   gpm_ref[...])
        return x1, _rms(x1, gpf_ref[...]).astype(BF16)

    def up(h2):
        return jnp.dot(h2, wup_ref[...], preferred_element_type=F32)

    def act(ug, bsl, r0, lh):
        nbh = bsl.stop - bsl.start
        g3 = ug[:, D_FF:].reshape(nbh, lh, D_FF)
        g_s[bsl, 8 + r0:8 + r0 + lh, :] = g3
        gconv = (cb_ref[...] + cw_ref[0:1, :] * g_s[bsl, 6 + r0:6 + r0 + lh, :]
                 + cw_ref[1:2, :] * g_s[bsl, 7 + r0:7 + r0 + lh, :] + cw_ref[2:3, :] * g3)
        return _gated_gelu(gconv.reshape(nbh * lh, D_FF), ug[:, :D_FF]).astype(BF16)

    def post(x1, a, rows):
        ff = jnp.dot(a, wdn_ref[...], preferred_element_type=F32)
        y_ref[rows, :] = x1 + _rms(ff, gpo_ref[...])

    n = len(parts)
    x1, h2, ug = [None] * n, [None] * n, [None] * n
    x1[0], h2[0] = pre(parts[0][0])
    for i in range(n + 1):
        if i + 1 < n:
            x1[i + 1], h2[i + 1] = pre(parts[i + 1][0])
        if i < n:
            ug[i] = up(h2[i])
        if i >= 1:
            rows, *conv = parts[i - 1]
            post(x1[i - 1], act(ug[i - 1], *conv), rows)
    cs_ref[...] = g_s[:, l + 6:l + 8, :]


def _ffn(x2d, attn, hg, ga, gm, conv0, wout, wup, wdn, gpm, gpf, gpo, cw, cb, nb_total, seq):
    n = x2d.shape[0]
    np_ = FFN_PARTS
    if seq >= FFN_ROWS:
        tm, nb, l = FFN_ROWS, 1, FFN_ROWS
        tiles_per_batch = seq // tm
        pl_ = l // np_
        parts = tuple((slice(j * pl_, (j + 1) * pl_), slice(0, 1), j * pl_, pl_) for j in range(np_))
    else:
        tm, nb, l = n, nb_total, seq
        tiles_per_batch = 1
        pb = nb // np_
        parts = tuple((slice(j * pb * l, (j + 1) * pb * l), slice(j * pb, (j + 1) * pb), 0, l)
                      for j in range(np_))
    row = pl.BlockSpec((tm, D_MODEL), lambda i: (i, 0))
    cstate = pl.BlockSpec((nb, CONV_W - 1, D_FF), lambda i: (i // tiles_per_batch, 0, 0))
    return pl.pallas_call(
        functools.partial(_ffn_kernel, parts=parts, l=l, tiles_per_batch=tiles_per_batch),
        grid=(n // tm,),
        in_specs=[row, row, row, row, row, cstate,
                  _resident((D_MODEL, D_MODEL)), _resident((D_MODEL, 2 * D_FF)),
                  _resident((D_FF, D_MODEL)), _resident((1, D_MODEL)), _resident((1, D_MODEL)),
                  _resident((1, D_MODEL)), _resident((CONV_W, D_FF)), _resident((1, D_FF))],
        out_specs=[row, cstate],
        out_shape=[jax.ShapeDtypeStruct((n, D_MODEL), F32),
                   jax.ShapeDtypeStruct((nb_total, CONV_W - 1, D_FF), F32)],
        scratch_shapes=[pltpu.VMEM((nb, l + 8, D_FF), F32)],
        compiler_params=pltpu.CompilerParams(
            dimension_semantics=("arbitrary",), vmem_limit_bytes=V7X_VMEM_LIMIT),
        name="merge_ffn",
    )(x2d, attn, hg, ga, gm, conv0, wout, wup, wdn, gpm, gpf, gpo, cw, cb)


def _layer(x, caches, c0, n0, m0, conv0, wts, lam_init):
    (g_pre_mix, wa, wb, wg, gate_bias, lq1, lk1, lq2, lk2, g_attn_head, g_mlstm_head, wout, g_post_mix,
     g_pre_ffn, wup, conv_w, conv_b, wdn, g_post_ffn) = wts
    b, l, _ = x.shape
    x2d = x.reshape(b * l, D_MODEL)
    (qa, kf, kb, vf, vb, qm, km, vm, om, ga, gm, gz, qn, kn) = _inproj(
        x2d, g_pre_mix, wa, wb, wg, b, l)

    if caches is None:
        per_b = qn.shape[0] // b
        attn = _attn_prompt(qa, kb, vb, qn.reshape(b, per_b, GATE_LANES), kn.reshape(b, per_b, GATE_LANES),
                            lq1, lk1, lq2, lk2, g_attn_head, lam_init)
    else:
        cache_k, cache_v, b_off = caches
        attn = _attn_sample(qa, kb, vb, cache_k, cache_v, b_off,
                            lq1, lk1, lq2, lk2, g_attn_head, lam_init)

    seq3 = lambda a: a.reshape(b, l, a.shape[-1])
    hm, c1, n1, m1 = _mlstm(seq3(qm), seq3(km), seq3(vm), seq3(om), seq3(gz), gate_bias, g_mlstm_head,
                            c0, n0, m0.reshape(b, 1, H_M))

    y, conv1 = _ffn(x2d, attn.reshape(b * l, D_MODEL), hm.reshape(b * l, D_MODEL), ga, gm,
                    conv0, wout, wup, wdn, g_post_mix, g_pre_ffn, g_post_ffn, conv_w, conv_b, b, l)
    return (y.reshape(b, l, D_MODEL), kf.reshape(b, l, H_A, DV_A), vf.reshape(b, l, H_A, DV_A),
            c1, n1, m1.reshape(b, H_M), conv1)


def kernel(x_prompt, x_sample, cache_k, cache_v, state_C, state_n, state_m, state_conv, g_pre_mix, w_in, b_gates, lam_q1, lam_k1, lam_q2, lam_k2, g_attn_head, g_mlstm_head, w_out, g_post_mix, g_pre_ffn, w_up, conv_w, conv_b, w_down, g_post_ffn):
    depth = w_in.shape[0]
    bp = x_prompt.shape[0]
    yp, ys = x_prompt, x_sample
    outs_p, outs_s = [], []
    n_main = 6 * D_MODEL
    for li in range(depth):
        lam_init = 0.8 - 0.6 * math.exp(-0.3 * li)
        w = w_in[li]
        wa = w[:, :n_main].astype(BF16)
        wb = w[:, n_main + 2 * H_M:].astype(BF16)
        wg = jnp.pad(w[:, n_main:n_main + 2 * H_M], ((0, 0), (0, GATE_LANES - 2 * H_M))).astype(BF16)
        gate_bias = jnp.concatenate(
            [b_gates[li], jnp.zeros((GATE_LANES - 2 * H_M,), F32)]).reshape(1, GATE_LANES)
        row = lambda a: a.reshape(1, -1)
        wts = (row(g_pre_mix[li]), wa, wb, wg, gate_bias, row(lam_q1[li]), row(lam_k1[li]),
               row(lam_q2[li]), row(lam_k2[li]), row(g_attn_head[li]), g_mlstm_head[li],
               w_out[li].astype(BF16), row(g_post_mix[li]), row(g_pre_ffn[li]),
               w_up[li].astype(BF16), conv_w[li], row(conv_b[li]), w_down[li].astype(BF16),
               row(g_post_ffn[li]))
        c0 = jnp.zeros((bp, H_M, DH_M, DH_M), F32)
        n0 = jnp.zeros((bp, H_M, DH_M), F32)
        m0 = jnp.zeros((bp, H_M), F32)
        conv0 = jnp.zeros((bp, CONV_W - 1, D_FF), F32)
        yp, *sp = _layer(yp, None, c0, n0, m0, conv0, wts, lam_init)
        bs, past = cache_k.shape[1], cache_k.shape[2]
        caches = (cache_k.reshape(depth * bs, past * H_A, DV_A),
                  cache_v.reshape(depth * bs, past * H_A, DV_A), li * bs)
        ys, *ss = _layer(ys, caches, state_C[li], state_n[li], state_m[li],
                         state_conv[li], wts, lam_init)
        outs_p.append(sp)
        outs_s.append(ss)
    k_p, v_p, c_p, n_p, m_p, conv_p = [jnp.stack([o[i] for o in outs_p]) for i in range(6)]
    k_s, v_s, c_s, n_s, m_s, conv_s = [jnp.stack([o[i] for o in outs_s]) for i in range(6)]
    return (yp, ys, k_p, v_p, c_p, n_p, m_p, conv_p, k_s, v_s, c_s, n_s, m_s, conv_s)
```

```python
import functools
import math

import jax
import jax.numpy as jnp
import numpy as np
from jax import lax
from jax.experimental import pallas as pl
from jax.experimental.pallas import tpu as pltpu

F32 = jnp.float32
BF16 = jnp.bfloat16

D_MODEL = 1024
CHUNK = 64
H_A = 8
DH_A = 64
DV_A = 2 * DH_A
H_M = 4
DH_M = D_MODEL // H_M
D_FF = 2816
CONV_W = 3
EPS = 1e-6
N_SEG_A = 6
GATE_LANES = 128
QK_SCALE = DH_A ** -0.5 * math.log2(math.e)

V7X_VMEM_LIMIT = 56 * 1024 * 1024

PROJ_ROWS = 256
ATTN_TILE = 512
SCORE_BOUND = 60.0
MLSTM_CHUNK = 256
MLSTM_STREAMS = 4
FFN_ROWS = 512
FFN_PARTS = 4

NT_DIMS = (((1,), (1,)), ((), ()))
TN_DIMS = (((0,), (0,)), ((), ()))


def _rms(x, g):
    return x * lax.rsqrt(jnp.mean(x * x, axis=-1, keepdims=True) + EPS) * g


def _resident(shape):
    nd = len(shape)
    return pl.BlockSpec(shape, lambda *_: (0,) * nd, pipeline_mode=pl.Buffered(1))


def _inproj_kernel(x_ref, g_ref, wa_ref, wb_ref, wg_ref, e_ref, qa_ref, kf_ref, kb_ref, vf_ref, vb_ref,
                   qm_ref, km_ref, vm_ref, om_ref, ga_ref, gm_ref, gz_ref, qn_ref, kn_ref,
                   *, nbt, lt, transposed):
    hb = _rms(x_ref[...], g_ref[...]).astype(BF16)

    def seg(j):
        w_ref, jj = (wa_ref, j) if j < N_SEG_A else (wb_ref, j - N_SEG_A)
        return jnp.dot(hb, w_ref[:, jj * D_MODEL:(jj + 1) * D_MODEL], preferred_element_type=F32)

    def put_heads(ref, val, transpose):
        for h in range(H_A):
            piece = val[:, h * DV_A:(h + 1) * DV_A]
            if transpose:
                ref[0, h] = piece.T.astype(BF16)
            else:
                ref[:, h] = piece.astype(BF16).reshape(nbt, lt, DV_A)

    def max_sq_norm(val):
        sq = jnp.dot((val * val).astype(BF16), e_ref[...], preferred_element_type=F32)
        return jnp.max(sq, axis=0, keepdims=True)

    q = seg(0) * QK_SCALE
    put_heads(qa_ref, q, transposed)
    qn_ref[0] = max_sq_norm(q)
    k = seg(1)
    kf_ref[...] = k
    put_heads(kb_ref, k, False)
    kn_ref[0] = max_sq_norm(k)
    v = seg(2)
    vf_ref[...] = v
    put_heads(vb_ref, v, transposed)
    qm_ref[...] = (seg(3) * DH_M ** -0.5).astype(BF16)
    km_ref[...] = seg(4).astype(BF16)
    vm_ref[...] = seg(5).astype(BF16)
    om_ref[...] = jax.nn.sigmoid(seg(6)).astype(BF16)
    ga_ref[...] = jax.nn.sigmoid(seg(7)).astype(BF16)
    gm_ref[...] = jax.nn.sigmoid(seg(8)).astype(BF16)
    gz_ref[...] = jnp.dot(hb, wg_ref[...], preferred_element_type=F32)


def _inproj(x2d, g, wa, wb, wg, nb, seq):
    n = x2d.shape[0]
    tm = PROJ_ROWS
    transposed = seq >= tm
    if transposed:
        nbt, lt = 1, tm
        per_b = seq // tm
        head = pl.BlockSpec((1, H_A, lt, DV_A), lambda i: (i // per_b, 0, i % per_b, 0))
        head_t = pl.BlockSpec((1, H_A, DV_A, lt), lambda i: (i // per_b, 0, 0, i % per_b))
        hm16_t = jax.ShapeDtypeStruct((nb, H_A, DV_A, seq), BF16)
    else:
        nbt, lt = tm // seq, seq
        head = head_t = pl.BlockSpec((nbt, H_A, lt, DV_A), lambda i: (i, 0, 0, 0))
        hm16_t = jax.ShapeDtypeStruct((nb, H_A, seq, DV_A), BF16)
    row = pl.BlockSpec((tm, D_MODEL), lambda i: (i, 0))
    norm = pl.BlockSpec((1, 1, GATE_LANES), lambda i: (i, 0, 0))
    norm_shape = jax.ShapeDtypeStruct((n // tm, 1, GATE_LANES), F32)
    group = (np.arange(D_MODEL)[:, None] // DH_A == np.arange(GATE_LANES)[None, :])
    full32 = jax.ShapeDtypeStruct((n, D_MODEL), F32)
    full16 = jax.ShapeDtypeStruct((n, D_MODEL), BF16)
    hm16 = jax.ShapeDtypeStruct((nb, H_A, seq, DV_A), BF16)
    return pl.pallas_call(
        functools.partial(_inproj_kernel, nbt=nbt, lt=lt, transposed=transposed),
        grid=(n // tm,),
        in_specs=[row, _resident((1, D_MODEL)), _resident(wa.shape), _resident(wb.shape),
                  _resident(wg.shape), _resident((D_MODEL, GATE_LANES))],
        out_specs=[head_t, row, head, row, head_t, row, row, row, row, row, row,
                   pl.BlockSpec((tm, GATE_LANES), lambda i: (i, 0)), norm, norm],
        out_shape=[hm16_t, full32, hm16, full32, hm16_t, full16, full16, full16,
                   full16, full16, full16, jax.ShapeDtypeStruct((n, GATE_LANES), F32),
                   norm_shape, norm_shape],
        compiler_params=pltpu.CompilerParams(
            dimension_semantics=("arbitrary",), vmem_limit_bytes=V7X_VMEM_LIMIT),
        name="inproj",
    )(x2d, g, wa, wb, wg, jnp.asarray(group, BF16))


def _lam(lq1, lk1, lq2, lk2, lam_init):
    return (jnp.exp(jnp.sum(lq1 * lk1, axis=-1, keepdims=True))
            - jnp.exp(jnp.sum(lq2 * lk2, axis=-1, keepdims=True)) + lam_init)


def _split_maps(q, axis):
    idx = lax.broadcasted_iota(jnp.int32, q.shape, axis)
    zero = jnp.zeros_like(q)
    return jnp.where(idx < DH_A, q, zero), jnp.where(idx >= DH_A, q, zero)


def _attn_prompt_kernel(qi_tab, kj_tab, qt_ref, k_ref, vt_ref, qn_ref, kn_ref, lq1_ref, lk1_ref, lq2_ref,
                        lk2_ref, gcol_ref, o_ref, qz_s, m_s, l_s, acc_s, mode_s, *, lam_init):
    p = pl.program_id(1)
    qi = qi_tab[p]
    kj = kj_tab[p]
    t = ATTN_TILE
    diag = kj == qi
    pairs = [(h, mp) for h in range(H_A) for mp in range(2)]
    n = len(pairs)

    @pl.when(kj == 0)
    def _init():
        for h in range(H_A):
            q1, q2 = _split_maps(qt_ref[h], 0)
            qz_s[0, h] = q1
            qz_s[1, h] = q2
        m_s[...] = jnp.full(m_s.shape, -jnp.inf, F32)
        l_s[...] = jnp.zeros(l_s.shape, F32)
        acc_s[...] = jnp.zeros(acc_s.shape, F32)
        bound_sq = (jnp.max(qn_ref[...], axis=0, keepdims=True)
                    * jnp.max(kn_ref[...], axis=0, keepdims=True))
        mode_s[0] = jnp.where(jnp.max(bound_sq) <= SCORE_BOUND * SCORE_BOUND, 1, 0)

    def chunk_mask():
        kr = lax.broadcasted_iota(jnp.int32, (t, t), 0) // CHUNK
        qc = lax.broadcasted_iota(jnp.int32, (t, t), 1) // CHUNK
        return kr <= qc

    def scores(h, mp, visible):
        s = jnp.dot(k_ref[h], qz_s[mp, h], preferred_element_type=F32)
        return s if visible is None else jnp.where(visible, s, -jnp.inf)

    def bounded_pass(visible):
        s_next = scores(*pairs[0], visible)
        for i, (h, mp) in enumerate(pairs):
            s = s_next
            if i + 1 < n:
                s_next = scores(*pairs[i + 1], visible)
            pr = jnp.exp2(s)
            l_s[mp, h] = l_s[mp, h] + jnp.sum(pr, axis=0, keepdims=True)
            acc_s[mp, h] = acc_s[mp, h] + jnp.dot(
                vt_ref[h], pr.astype(BF16), preferred_element_type=F32)

    def online_pass(visible):
        s_next = scores(*pairs[0], visible)
        for i, (h, mp) in enumerate(pairs):
            s = s_next
            m_old = m_s[mp, h]
            m_new = jnp.maximum(m_old, jnp.max(s, axis=0, keepdims=True))
            if i + 1 < n:
                s_next = scores(*pairs[i + 1], visible)
            alpha = jnp.exp2(m_old - m_new)
            pr = jnp.exp2(s - m_new)
            l_s[mp, h] = alpha * l_s[mp, h] + jnp.sum(pr, axis=0, keepdims=True)
            acc_s[mp, h] = alpha * acc_s[mp, h] + jnp.dot(
                vt_ref[h], pr.astype(BF16), preferred_element_type=F32)
            m_s[mp, h] = m_new

    bounded = mode_s[0] == 1
    off_diag = jnp.logical_not(diag)

    @pl.when(jnp.logical_and(bounded, off_diag))
    def _bounded_full():
        bounded_pass(None)

    @pl.when(jnp.logical_and(bounded, diag))
    def _bounded_diag():
        bounded_pass(chunk_mask())

    @pl.when(jnp.logical_and(jnp.logical_not(bounded), off_diag))
    def _online_full():
        online_pass(None)

    @pl.when(jnp.logical_and(jnp.logical_not(bounded), diag))
    def _online_diag():
        online_pass(chunk_mask())

    @pl.when(diag)
    def _finish():
        lam = _lam(lq1_ref[...], lk1_ref[...], lq2_ref[...], lk2_ref[...], lam_init)
        for h in range(H_A):
            ot = acc_s[0, h] / l_s[0, h] - lam * (acc_s[1, h] / l_s[1, h])
            ms = jnp.mean(ot * ot, axis=0, keepdims=True)
            ot = ot * lax.rsqrt(ms + EPS) * gcol_ref[...] * (1.0 - lam_init)
            o_ref[:, h * DV_A:(h + 1) * DV_A] = ot.T.astype(BF16)


def _attn_prompt(qt, k, vt, qn, kn, lq1, lk1, lq2, lk2, gh, lam_init):
    b, _, s, _ = k.shape
    t = ATTN_TILE
    nq = s // t
    pairs = [(i, j) for i in range(nq) for j in range(i + 1)]
    qi_tab = jnp.asarray(np.array([a for a, _ in pairs], np.int32))
    kj_tab = jnp.asarray(np.array([c for _, c in pairs], np.int32))
    qspec = pl.BlockSpec((None, H_A, DV_A, t), lambda bb, p, qi, kj: (bb, 0, 0, qi[p]))
    kspec = pl.BlockSpec((None, H_A, t, DV_A), lambda bb, p, qi, kj: (bb, 0, kj[p], 0))
    vspec = pl.BlockSpec((None, H_A, DV_A, t), lambda bb, p, qi, kj: (bb, 0, 0, kj[p]))
    small = lambda shape: pl.BlockSpec(shape, lambda bb, p, qi, kj: (0, 0))
    nspec = pl.BlockSpec((None,) + qn.shape[1:], lambda bb, p, qi, kj: (bb, 0, 0))
    grid_spec = pltpu.PrefetchScalarGridSpec(
        num_scalar_prefetch=2,
        grid=(b, len(pairs)),
        in_specs=[qspec, kspec, vspec, nspec, nspec, small((1, DH_A)), small((1, DH_A)), small((1, DH_A)),
                  small((1, DH_A)), small((DV_A, 1))],
        out_specs=pl.BlockSpec((None, t, H_A * DV_A), lambda bb, p, qi, kj: (bb, qi[p], 0)),
        scratch_shapes=[pltpu.VMEM((2, H_A, DV_A, t), BF16),
                        pltpu.VMEM((2, H_A, 1, t), F32),
                        pltpu.VMEM((2, H_A, 1, t), F32),
                        pltpu.VMEM((2, H_A, DV_A, t), F32),
                        pltpu.SMEM((1,), jnp.int32)],
    )
    return pl.pallas_call(
        functools.partial(_attn_prompt_kernel, lam_init=lam_init),
        grid_spec=grid_spec,
        out_shape=jax.ShapeDtypeStruct((b, s, H_A * DV_A), BF16),
        compiler_params=pltpu.CompilerParams(
            dimension_semantics=("arbitrary", "arbitrary"), vmem_limit_bytes=V7X_VMEM_LIMIT),
        name="attn_prompt",
    )(qi_tab, kj_tab, qt, k, vt, qn, kn, lq1, lk1, lq2, lk2, gh.reshape(DV_A, 1))


def _attn_sample_kernel(q_ref, kn_ref, vn_ref, ck_ref, cv_ref, lq1_ref, lk1_ref, lq2_ref, lk2_ref,
                        gh_ref, o_ref, *, lam_init):
    lam = _lam(lq1_ref[...], lk1_ref[...], lq2_ref[...], lk2_ref[...], lam_init)
    nq = q_ref.shape[1]
    past = ck_ref.shape[0] // H_A
    for h in range(H_A):
        sl = slice(h * DV_A, (h + 1) * DV_A)
        rows = pl.ds(h, past, stride=H_A)
        q1, q2 = _split_maps(q_ref[h], 1)
        qz = jnp.concatenate([q1, q2], axis=0)
        kc = ck_ref[rows, :].astype(BF16)
        sc = lax.dot_general(qz, kc, NT_DIMS, preferred_element_type=F32)
        sn = lax.dot_general(qz, kn_ref[h], NT_DIMS, preferred_element_type=F32)
        m = jnp.maximum(jnp.max(sc, axis=-1, keepdims=True), jnp.max(sn, axis=-1, keepdims=True))
        pc = jnp.exp2(sc - m)
        pn = jnp.exp2(sn - m)
        inv = 1.0 / (jnp.sum(pc, axis=-1, keepdims=True) + jnp.sum(pn, axis=-1, keepdims=True))
        pc = pc * inv
        pn = pn * inv
        ac = pc[:nq] - lam * pc[nq:]
        an = pn[:nq] - lam * pn[nq:]
        o = (jnp.dot(ac.astype(BF16), cv_ref[rows, :].astype(BF16), preferred_element_type=F32)
             + jnp.dot(an.astype(BF16), vn_ref[h], preferred_element_type=F32))
        o_ref[:, sl] = (_rms(o, gh_ref[...]) * (1.0 - lam_init)).astype(BF16)


def _attn_sample(q, kn, vn, cache_k, cache_v, b_off, lq1, lk1, lq2, lk2, gh, lam_init):
    b, _, l, _ = q.shape
    hspec = pl.BlockSpec((None, H_A, l, DV_A), lambda i: (i, 0, 0, 0))
    cspec = pl.BlockSpec((None,) + cache_k.shape[1:], lambda i: (b_off + i, 0, 0))
    small = lambda shape: pl.BlockSpec(shape, lambda i: (0, 0))
    return pl.pallas_call(
        functools.partial(_attn_sample_kernel, lam_init=lam_init),
        grid=(b,),
        in_specs=[hspec, hspec, hspec, cspec, cspec, small((1, DH_A)), small((1, DH_A)),
                  small((1, DH_A)), small((1, DH_A)), small((1, DV_A))],
        out_specs=pl.BlockSpec((None, l, H_A * DV_A), lambda i: (i, 0, 0)),
        out_shape=jax.ShapeDtypeStruct((b, l, H_A * DV_A), BF16),
        compiler_params=pltpu.CompilerParams(
            dimension_semantics=("arbitrary",), vmem_limit_bytes=V7X_VMEM_LIMIT),
        name="attn_sample",
    )(q, kn, vn, cache_k, cache_v, lq1, lk1, lq2, lk2, gh)


def _split3(x):
    hi = x.astype(BF16)
    r1 = x - hi.astype(F32)
    mid = r1.astype(BF16)
    lo = (r1 - mid.astype(F32)).astype(BF16)
    return hi, mid, lo


def _mlstm_kernel(q_ref, k_ref, v_ref, og_ref, gz_ref, bias_ref, gh_ref, c0_ref, n0_ref, m0_ref,
                  h_ref, c_ref, n_ref, m_ref, c_s, n_s, m_s, *, t, bt):
    c = pl.program_id(1)

    @pl.when(c == 0)
    def _load_state():
        c_s[...] = c0_ref[...]
        n_s[...] = n0_ref[...]
        m_s[...] = m0_ref[...]

    lane = lax.broadcasted_iota(jnp.int32, (t, GATE_LANES), 1)
    row = lax.broadcasted_iota(jnp.int32, (t, t), 0)
    col = lax.broadcasted_iota(jnp.int32, (t, t), 1)
    causal = col <= row
    tril = jnp.where(causal, 1.0, 0.0).astype(BF16)
    sel = jnp.where(lax.broadcasted_iota(jnp.int32, (8, GATE_LANES), 0)
                    == lax.broadcasted_iota(jnp.int32, (8, GATE_LANES), 1), 1.0, 0.0).astype(BF16)

    gz = [gz_ref[b] + bias_ref[...] for b in range(bt)]
    lf = [jnp.minimum(g, 0.0) - jnp.log1p(jnp.exp(-jnp.abs(g))) for g in gz]
    lf = [jnp.where((lane >= H_M) & (lane < 2 * H_M), x, 0.0) for x in lf]
    bcum = [sum(jnp.dot(tril, piece, preferred_element_type=F32) for piece in _split3(x)) for x in lf]
    gc = [jnp.where(lane < H_M, gz[b], bcum[b]) for b in range(bt)]
    gr = [sum(lax.dot_general(sel, piece, NT_DIMS, preferred_element_type=F32) for piece in _split3(x))
          for x in gc]

    chains = [(b, h) for b in range(bt) for h in range(H_M)]
    ids = range(len(chains))
    sls = [slice(h * DH_M, (h + 1) * DH_M) for _, h in chains]
    ig_c = [gc[b][:, h:h + 1] for b, h in chains]
    b_c = [gc[b][:, H_M + h:H_M + h + 1] for b, h in chains]
    ig_r = [gr[b][h:h + 1, :] for b, h in chains]
    b_r = [gr[b][H_M + h:H_M + h + 1, :] for b, h in chains]
    m_prev = [m_s[b, :, h:h + 1] for b, h in chains]
    c_prev = [c_s[b, h] for b, h in chains]
    n_prev = [n_s[b, h:h + 1, :] for b, h in chains]
    q = [q_ref[b, :, sls[i]] for i, (b, _) in enumerate(chains)]
    k = [k_ref[b, :, sls[i]] for i, (b, _) in enumerate(chains)]
    v = [v_ref[b, :, sls[i]] for i, (b, _) in enumerate(chains)]

    qk = [lax.dot_general(q[i], k[i], NT_DIMS, preferred_element_type=F32) for i in ids]
    qc = [lax.dot_general(q[i], c_prev[i].astype(BF16), NT_DIMS, preferred_element_type=F32)
          for i in ids]
    d = [jnp.where(causal, b_c[i] - b_r[i] + ig_r[i], -jnp.inf) for i in ids]
    inter = [b_c[i] + m_prev[i] for i in ids]
    m_t = [jnp.maximum(inter[i], jnp.max(d[i], axis=-1, keepdims=True)) for i in ids]
    w_inter = [jnp.exp(inter[i] - m_t[i]) for i in ids]
    s = [qk[i] * jnp.exp(d[i] - m_t[i]) for i in ids]
    num = [jnp.dot(s[i].astype(BF16), v[i], preferred_element_type=F32) + w_inter[i] * qc[i]
           for i in ids]
    den = [jnp.sum(s[i], axis=-1, keepdims=True)
           + w_inter[i] * jnp.sum(q[i].astype(F32) * n_prev[i], axis=-1, keepdims=True) for i in ids]
    hh = [num[i] / jnp.maximum(jnp.abs(den[i]), jnp.exp(-m_t[i])) for i in ids]
    for i, (b, h) in enumerate(chains):
        h_ref[b, :, sls[i]] = (_rms(hh[i], gh_ref[h:h + 1, :])
                               * og_ref[b, :, sls[i]].astype(F32)).astype(BF16)

    g_last = [b_c[i][t - 1:t, :] for i in ids]
    logw = [g_last[i] - b_c[i] + ig_c[i] for i in ids]
    m_new = [jnp.maximum(g_last[i] + m_prev[i], jnp.max(logw[i], axis=0, keepdims=True)) for i in ids]
    ws = [jnp.exp(logw[i] - m_new[i]) for i in ids]
    wc = [jnp.exp(g_last[i] + m_prev[i] - m_new[i]) for i in ids]
    vw = [(v[i].astype(F32) * ws[i]).astype(BF16) for i in ids]
    for i, (b, h) in enumerate(chains):
        c_s[b, h] = wc[i] * c_prev[i] + lax.dot_general(vw[i], k[i], TN_DIMS,
                                                        preferred_element_type=F32)
        n_s[b, h:h + 1, :] = wc[i] * n_prev[i] + jnp.sum(k[i].astype(F32) * ws[i], axis=0,
                                                          keepdims=True)
        m_s[b, :, h:h + 1] = m_new[i]

    @pl.when(c == pl.num_programs(1) - 1)
    def _store_state():
        c_ref[...] = c_s[...]
        n_ref[...] = n_s[...]
        m_ref[...] = m_s[...]


def _mlstm(q, k, v, og, gz, bias, gh, c0, n0, m0):
    b, l, _ = q.shape
    t = min(MLSTM_CHUNK, l)
    nc = l // t
    bt = min(b, MLSTM_STREAMS)
    seq = lambda width: pl.BlockSpec((bt, t, width), lambda i, j: (i, j, 0))
    const = lambda shape: pl.BlockSpec(shape, lambda i, j: (0,) * len(shape))
    cspec = pl.BlockSpec((bt, H_M, DH_M, DH_M), lambda i, j: (i, 0, 0, 0))
    nspec = pl.BlockSpec((bt, H_M, DH_M), lambda i, j: (i, 0, 0))
    mspec = pl.BlockSpec((bt, 1, H_M), lambda i, j: (i, 0, 0))
    return pl.pallas_call(
        functools.partial(_mlstm_kernel, t=t, bt=bt),
        grid=(b // bt, nc),
        in_specs=[seq(D_MODEL), seq(D_MODEL), seq(D_MODEL), seq(D_MODEL), seq(GATE_LANES),
                  const((1, GATE_LANES)), const((H_M, DH_M)), cspec, nspec, mspec],
        out_specs=[seq(D_MODEL), cspec, nspec, mspec],
        out_shape=[jax.ShapeDtypeStruct((b, l, D_MODEL), BF16),
                   jax.ShapeDtypeStruct((b, H_M, DH_M, DH_M), F32),
                   jax.ShapeDtypeStruct((b, H_M, DH_M), F32),
                   jax.ShapeDtypeStruct((b, 1, H_M), F32)],
        scratch_shapes=[pltpu.VMEM((bt, H_M, DH_M, DH_M), F32),
                        pltpu.VMEM((bt, H_M, DH_M), F32),
                        pltpu.VMEM((bt, 1, H_M), F32)],
        compiler_params=pltpu.CompilerParams(
            dimension_semantics=("arbitrary", "arbitrary"), vmem_limit_bytes=V7X_VMEM_LIMIT),
        name="mlstm",
    )(q, k, v, og, gz, bias, gh, c0, n0, m0)


def _gated_gelu(x, u):
    c = -2.0 * math.sqrt(2.0 / math.pi)
    return (x * u) / (1.0 + jnp.exp(x * (c + (c * 0.044715) * (x * x))))


def _ffn_kernel(x_ref, attn_ref, hg_ref, ga_ref, gm_ref, conv0_ref, wout_ref, wup_ref, wdn_ref,
                gpm_ref, gpf_ref, gpo_ref, cw_ref, cb_ref, y_ref, cs_ref, g_s,
                *, parts, l, tiles_per_batch):
    i = pl.program_id(0)

    @pl.when(i % tiles_per_batch == 0)
    def _from_state():
        g_s[:, 6:8, :] = conv0_ref[...]

    @pl.when(i % tiles_per_batch != 0)
    def _from_prev_tile():
        g_s[:, 6:8, :] = g_s[:, l + 6:l + 8, :]

    def pre(rows):
        merged = ga_ref[rows, :] * attn_ref[rows, :] + gm_ref[rows, :] * hg_ref[rows, :]
        x1 = x_ref[rows, :] + _rms(jnp.dot(merged, wout_ref[...], preferred_element_type=F32),
                                   gpm_ref[...])
        return x1, _rms(x1, gpf_ref[...]).astype(BF16)

    def up(h2):
        return jnp.dot(h2, wup_ref[...], preferred_element_type=F32)

    def act(ug, bsl, r0, lh):
        nbh = bsl.stop - bsl.start
        g3 = ug[:, D_FF:].reshape(nbh, lh, D_FF)
        g_s[bsl, 8 + r0:8 + r0 + lh, :] = g3
        gconv = (cb_ref[...] + cw_ref[0:1, :] * g_s[bsl, 6 + r0:6 + r0 + lh, :]
                 + cw_ref[1:2, :] * g_s[bsl, 7 + r0:7 + r0 + lh, :] + cw_ref[2:3, :] * g3)
        return _gated_gelu(gconv.reshape(nbh * lh, D_FF), ug[:, :D_FF]).astype(BF16)

    def post(x1, a, rows):
        ff = jnp.dot(a, wdn_ref[...], preferred_element_type=F32)
        y_ref[rows, :] = x1 + _rms(ff, gpo_ref[...])

    n = len(parts)
    x1, h2, ug = [None] * n, [None] * n, [None] * n
    x1[0], h2[0] = pre(parts[0][0])
    for i in range(n + 1):
        if i + 1 < n:
            x1[i + 1], h2[i + 1] = pre(parts[i + 1][0])
        if i < n:
            ug[i] = up(h2[i])
        if i >= 1:
            rows, *conv = parts[i - 1]
            post(x1[i - 1], act(ug[i - 1], *conv), rows)
    cs_ref[...] = g_s[:, l + 6:l + 8, :]


def _ffn(x2d, attn, hg, ga, gm, conv0, wout, wup, wdn, gpm, gpf, gpo, cw, cb, nb_total, seq):
    n = x2d.shape[0]
    np_ = FFN_PARTS
    if seq >= FFN_ROWS:
        tm, nb, l = FFN_ROWS, 1, FFN_ROWS
        tiles_per_batch = seq // tm
        pl_ = l // np_
        parts = tuple((slice(j * pl_, (j + 1) * pl_), slice(0, 1), j * pl_, pl_) for j in range(np_))
    else:
        tm, nb, l = n, nb_total, seq
        tiles_per_batch = 1
        pb = nb // np_
        parts = tuple((slice(j * pb * l, (j + 1) * pb * l), slice(j * pb, (j + 1) * pb), 0, l)
                      for j in range(np_))
    row = pl.BlockSpec((tm, D_MODEL), lambda i: (i, 0))
    cstate = pl.BlockSpec((nb, CONV_W - 1, D_FF), lambda i: (i // tiles_per_batch, 0, 0))
    return pl.pallas_call(
        functools.partial(_ffn_kernel, parts=parts, l=l, tiles_per_batch=tiles_per_batch),
        grid=(n // tm,),
        in_specs=[row, row, row, row, row, cstate,
                  _resident((D_MODEL, D_MODEL)), _resident((D_MODEL, 2 * D_FF)),
                  _resident((D_FF, D_MODEL)), _resident((1, D_MODEL)), _resident((1, D_MODEL)),
                  _resident((1, D_MODEL)), _resident((CONV_W, D_FF)), _resident((1, D_FF))],
        out_specs=[row, cstate],
        out_shape=[jax.ShapeDtypeStruct((n, D_MODEL), F32),
                   jax.ShapeDtypeStruct((nb_total, CONV_W - 1, D_FF), F32)],
        scratch_shapes=[pltpu.VMEM((nb, l + 8, D_FF), F32)],
        compiler_params=pltpu.CompilerParams(
            dimension_semantics=("arbitrary",), vmem_limit_bytes=V7X_VMEM_LIMIT),
        name="merge_ffn",
    )(x2d, attn, hg, ga, gm, conv0, wout, wup, wdn, gpm, gpf, gpo, cw, cb)


def _layer(x, caches, c0, n0, m0, conv0, wts, lam_init):
    (g_pre_mix, wa, wb, wg, gate_bias, lq1, lk1, lq2, lk2, g_attn_head, g_mlstm_head, wout, g_post_mix,
     g_pre_ffn, wup, conv_w, conv_b, wdn, g_post_ffn) = wts
    b, l, _ = x.shape
    x2d = x.reshape(b * l, D_MODEL)
    (qa, kf, kb, vf, vb, qm, km, vm, om, ga, gm, gz, qn, kn) = _inproj(
        x2d, g_pre_mix, wa, wb, wg, b, l)

    if caches is None:
        per_b = qn.shape[0] // b
        attn = _attn_prompt(qa, kb, vb, qn.reshape(b, per_b, GATE_LANES), kn.reshape(b, per_b, GATE_LANES),
                            lq1, lk1, lq2, lk2, g_attn_head, lam_init)
    else:
        cache_k, cache_v, b_off = caches
        attn = _attn_sample(qa, kb, vb, cache_k, cache_v, b_off,
                            lq1, lk1, lq2, lk2, g_attn_head, lam_init)

    seq3 = lambda a: a.reshape(b, l, a.shape[-1])
    hm, c1, n1, m1 = _mlstm(seq3(qm), seq3(km), seq3(vm), seq3(om), seq3(gz), gate_bias, g_mlstm_head,
                            c0, n0, m0.reshape(b, 1, H_M))

    y, conv1 = _ffn(x2d, attn.reshape(b * l, D_MODEL), hm.reshape(b * l, D_MODEL), ga, gm,
                    conv0, wout, wup, wdn, g_post_mix, g_pre_ffn, g_post_ffn, conv_w, conv_b, b, l)
    return (y.reshape(b, l, D_MODEL), kf.reshape(b, l, H_A, DV_A), vf.reshape(b, l, H_A, DV_A),
            c1, n1, m1.reshape(b, H_M), conv1)


def kernel(x_prompt, x_sample, cache_k, cache_v, state_C, state_n, state_m, state_conv, g_pre_mix, w_in, b_gates, lam_q1, lam_k1, lam_q2, lam_k2, g_attn_head, g_mlstm_head, w_out, g_post_mix, g_pre_ffn, w_up, conv_w, conv_b, w_down, g_post_ffn):
    depth = w_in.shape[0]
    bp = x_prompt.shape[0]
    yp, ys = x_prompt, x_sample
    outs_p, outs_s = [], []
    n_main = 6 * D_MODEL
    for li in range(depth):
        lam_init = 0.8 - 0.6 * math.exp(-0.3 * li)
        w = w_in[li]
        wa = w[:, :n_main].astype(BF16)
        wb = w[:, n_main + 2 * H_M:].astype(BF16)
        wg = jnp.pad(w[:, n_main:n_main + 2 * H_M], ((0, 0), (0, GATE_LANES - 2 * H_M))).astype(BF16)
        gate_bias = jnp.concatenate(
            [b_gates[li], jnp.zeros((GATE_LANES - 2 * H_M,), F32)]).reshape(1, GATE_LANES)
        row = lambda a: a.reshape(1, -1)
        wts = (row(g_pre_mix[li]), wa, wb, wg, gate_bias, row(lam_q1[li]), row(lam_k1[li]),
               row(lam_q2[li]), row(lam_k2[li]), row(g_attn_head[li]), g_mlstm_head[li],
               w_out[li].astype(BF16), row(g_post_mix[li]), row(g_pre_ffn[li]),
               w_up[li].astype(BF16), conv_w[li], row(conv_b[li]), w_down[li].astype(BF16),
               row(g_post_ffn[li]))
        c0 = jnp.zeros((bp, H_M, DH_M, DH_M), F32)
        n0 = jnp.zeros((bp, H_M, DH_M), F32)
        m0 = jnp.zeros((bp, H_M), F32)
        conv0 = jnp.zeros((bp, CONV_W - 1, D_FF), F32)
        yp, *sp = _layer(yp, None, c0, n0, m0, conv0, wts, lam_init)
        bs, past = cache_k.shape[1], cache_k.shape[2]
        caches = (cache_k.reshape(depth * bs, past * H_A, DV_A),
                  cache_v.reshape(depth * bs, past * H_A, DV_A), li * bs)
        ys, *ss = _layer(ys, caches, state_C[li], state_n[li], state_m[li],
                         state_conv[li], wts, lam_init)
        outs_p.append(sp)
        outs_s.append(ss)
    k_p, v_p, c_p, n_p, m_p, conv_p = [jnp.stack([o[i] for o in outs_p]) for i in range(6)]
    k_s, v_s, c_s, n_s, m_s, conv_s = [jnp.stack([o[i] for o in outs_s]) for i in range(6)]
    return (yp, ys, k_p, v_p, c_p, n_p, m_p, conv_p, k_s, v_s, c_s, n_s, m_s, conv_s)
```

```python
import functools
import math

import jax
import jax.numpy as jnp
import numpy as np
from jax import lax
from jax.experimental import pallas as pl
from jax.experimental.pallas import tpu as pltpu

F32 = jnp.float32
BF16 = jnp.bfloat16

D_MODEL = 1024
CHUNK = 64
H_A = 8
DH_A = 64
DV_A = 2 * DH_A
H_M = 4
DH_M = D_MODEL // H_M
D_FF = 2816
CONV_W = 3
EPS = 1e-6
N_SEG_A = 6
GATE_LANES = 128
QK_SCALE = DH_A ** -0.5 * math.log2(math.e)

V7X_VMEM_LIMIT = 56 * 1024 * 1024

PROJ_ROWS = 256
ATTN_TILE = 512
SCORE_BOUND = 80.0
MLSTM_CHUNK = 256
MLSTM_STREAMS = 4
FFN_ROWS = 512
FFN_PARTS = 4

NT_DIMS = (((1,), (1,)), ((), ()))
TN_DIMS = (((0,), (0,)), ((), ()))


def _rms(x, g):
    return x * lax.rsqrt(jnp.mean(x * x, axis=-1, keepdims=True) + EPS) * g


def _resident(shape):
    nd = len(shape)
    return pl.BlockSpec(shape, lambda *_: (0,) * nd, pipeline_mode=pl.Buffered(1))


def _inproj_kernel(x_ref, g_ref, wa_ref, wb_ref, wg_ref, qa_ref, kf_ref, kb_ref, vf_ref, vb_ref,
                   qm_ref, km_ref, vm_ref, om_ref, ga_ref, gm_ref, gz_ref, qn_ref, kn_ref,
                   *, nbt, lt, transposed):
    hb = _rms(x_ref[...], g_ref[...]).astype(BF16)

    def seg(j):
        w_ref, jj = (wa_ref, j) if j < N_SEG_A else (wb_ref, j - N_SEG_A)
        return jnp.dot(hb, w_ref[:, jj * D_MODEL:(jj + 1) * D_MODEL], preferred_element_type=F32)

    def put_heads(ref, val, transpose):
        for h in range(H_A):
            piece = val[:, h * DV_A:(h + 1) * DV_A]
            if transpose:
                ref[0, h] = piece.T.astype(BF16)
            else:
                ref[:, h] = piece.astype(BF16).reshape(nbt, lt, DV_A)

    def max_sq_norm(val):
        lane = lax.broadcasted_iota(jnp.int32, (1, GATE_LANES), 1)
        out = jnp.zeros((1, GATE_LANES), F32)
        for h in range(H_A):
            piece = val[:, h * DV_A:(h + 1) * DV_A]
            top = jnp.max(jnp.sum(piece * piece, axis=-1, keepdims=True), axis=0, keepdims=True)
            out = jnp.where(lane == h, top, out)
        return out

    q = seg(0) * QK_SCALE
    put_heads(qa_ref, q, transposed)
    qn_ref[0] = max_sq_norm(q)
    k = seg(1)
    kf_ref[...] = k
    put_heads(kb_ref, k, False)
    kn_ref[0] = max_sq_norm(k)
    v = seg(2)
    vf_ref[...] = v
    put_heads(vb_ref, v, transposed)
    qm_ref[...] = (seg(3) * DH_M ** -0.5).astype(BF16)
    km_ref[...] = seg(4).astype(BF16)
    vm_ref[...] = seg(5).astype(BF16)
    om_ref[...] = jax.nn.sigmoid(seg(6)).astype(BF16)
    ga_ref[...] = jax.nn.sigmoid(seg(7)).astype(BF16)
    gm_ref[...] = jax.nn.sigmoid(seg(8)).astype(BF16)
    gz_ref[...] = jnp.dot(hb, wg_ref[...], preferred_element_type=F32)


def _inproj(x2d, g, wa, wb, wg, nb, seq):
    n = x2d.shape[0]
    tm = PROJ_ROWS
    transposed = seq >= tm
    if transposed:
        nbt, lt = 1, tm
        per_b = seq // tm
        head = pl.BlockSpec((1, H_A, lt, DV_A), lambda i: (i // per_b, 0, i % per_b, 0))
        head_t = pl.BlockSpec((1, H_A, DV_A, lt), lambda i: (i // per_b, 0, 0, i % per_b))
        hm16_t = jax.ShapeDtypeStruct((nb, H_A, DV_A, seq), BF16)
    else:
        nbt, lt = tm // seq, seq
        head = head_t = pl.BlockSpec((nbt, H_A, lt, DV_A), lambda i: (i, 0, 0, 0))
        hm16_t = jax.ShapeDtypeStruct((nb, H_A, seq, DV_A), BF16)
    row = pl.BlockSpec((tm, D_MODEL), lambda i: (i, 0))
    norm = pl.BlockSpec((1, 1, GATE_LANES), lambda i: (i, 0, 0))
    norm_shape = jax.ShapeDtypeStruct((n // tm, 1, GATE_LANES), F32)
    full32 = jax.ShapeDtypeStruct((n, D_MODEL), F32)
    full16 = jax.ShapeDtypeStruct((n, D_MODEL), BF16)
    hm16 = jax.ShapeDtypeStruct((nb, H_A, seq, DV_A), BF16)
    return pl.pallas_call(
        functools.partial(_inproj_kernel, nbt=nbt, lt=lt, transposed=transposed),
        grid=(n // tm,),
        in_specs=[row, _resident((1, D_MODEL)), _resident(wa.shape), _resident(wb.shape),
                  _resident(wg.shape)],
        out_specs=[head_t, row, head, row, head_t, row, row, row, row, row, row,
                   pl.BlockSpec((tm, GATE_LANES), lambda i: (i, 0)), norm, norm],
        out_shape=[hm16_t, full32, hm16, full32, hm16_t, full16, full16, full16,
                   full16, full16, full16, jax.ShapeDtypeStruct((n, GATE_LANES), F32),
                   norm_shape, norm_shape],
        compiler_params=pltpu.CompilerParams(
            dimension_semantics=("arbitrary",), vmem_limit_bytes=V7X_VMEM_LIMIT),
        name="inproj",
    )(x2d, g, wa, wb, wg)


def _lam(lq1, lk1, lq2, lk2, lam_init):
    return (jnp.exp(jnp.sum(lq1 * lk1, axis=-1, keepdims=True))
            - jnp.exp(jnp.sum(lq2 * lk2, axis=-1, keepdims=True)) + lam_init)


def _split_maps(q, axis):
    idx = lax.broadcasted_iota(jnp.int32, q.shape, axis)
    zero = jnp.zeros_like(q)
    return jnp.where(idx < DH_A, q, zero), jnp.where(idx >= DH_A, q, zero)


def _attn_prompt_kernel(qi_tab, kj_tab, qt_ref, k_ref, vt_ref, qn_ref, kn_ref, lq1_ref, lk1_ref, lq2_ref,
                        lk2_ref, gcol_ref, o_ref, qz_s, m_s, l_s, acc_s, mode_s, *, lam_init):
    p = pl.program_id(1)
    qi = qi_tab[p]
    kj = kj_tab[p]
    t = ATTN_TILE
    diag = kj == qi
    pairs = [(h, mp) for h in range(H_A) for mp in range(2)]
    n = len(pairs)

    @pl.when(kj == 0)
    def _init():
        for h in range(H_A):
            q1, q2 = _split_maps(qt_ref[h], 0)
            qz_s[0, h] = q1
            qz_s[1, h] = q2
        m_s[...] = jnp.full(m_s.shape, -jnp.inf, F32)
        l_s[...] = jnp.zeros(l_s.shape, F32)
        acc_s[...] = jnp.zeros(acc_s.shape, F32)
        bound_sq = (jnp.max(qn_ref[...], axis=0, keepdims=True)
                    * jnp.max(kn_ref[...], axis=0, keepdims=True))
        mode_s[0] = jnp.where(jnp.max(bound_sq) <= SCORE_BOUND * SCORE_BOUND, 1, 0)

    def chunk_mask():
        kr = lax.broadcasted_iota(jnp.int32, (t, t), 0) // CHUNK
        qc = lax.broadcasted_iota(jnp.int32, (t, t), 1) // CHUNK
        return kr <= qc

    def scores(h, mp, visible):
        s = jnp.dot(k_ref[h], qz_s[mp, h], preferred_element_type=F32)
        return s if visible is None else jnp.where(visible, s, -jnp.inf)

    def bounded_pass(visible):
        s_next = scores(*pairs[0], visible)
        for i, (h, mp) in enumerate(pairs):
            s = s_next
            if i + 1 < n:
                s_next = scores(*pairs[i + 1], visible)
            pr = jnp.exp2(s)
            l_s[mp, h] = l_s[mp, h] + jnp.sum(pr, axis=0, keepdims=True)
            acc_s[mp, h] = acc_s[mp, h] + jnp.dot(
                vt_ref[h], pr.astype(BF16), preferred_element_type=F32)

    def online_pass(visible):
        s_next = scores(*pairs[0], visible)
        for i, (h, mp) in enumerate(pairs):
            s = s_next
            m_old = m_s[mp, h]
            m_new = jnp.maximum(m_old, jnp.max(s, axis=0, keepdims=True))
            if i + 1 < n:
                s_next = scores(*pairs[i + 1], visible)
            alpha = jnp.exp2(m_old - m_new)
            pr = jnp.exp2(s - m_new)
            l_s[mp, h] = alpha * l_s[mp, h] + jnp.sum(pr, axis=0, keepdims=True)
            acc_s[mp, h] = alpha * acc_s[mp, h] + jnp.dot(
                vt_ref[h], pr.astype(BF16), preferred_element_type=F32)
            m_s[mp, h] = m_new

    bounded = mode_s[0] == 1
    off_diag = jnp.logical_not(diag)

    @pl.when(jnp.logical_and(bounded, off_diag))
    def _bounded_full():
        bounded_pass(None)

    @pl.when(jnp.logical_and(bounded, diag))
    def _bounded_diag():
        bounded_pass(chunk_mask())

    @pl.when(jnp.logical_and(jnp.logical_not(bounded), off_diag))
    def _online_full():
        online_pass(None)

    @pl.when(jnp.logical_and(jnp.logical_not(bounded), diag))
    def _online_diag():
        online_pass(chunk_mask())

    @pl.when(diag)
    def _finish():
        lam = _lam(lq1_ref[...], lk1_ref[...], lq2_ref[...], lk2_ref[...], lam_init)
        for h in range(H_A):
            ot = acc_s[0, h] / l_s[0, h] - lam * (acc_s[1, h] / l_s[1, h])
            ms = jnp.mean(ot * ot, axis=0, keepdims=True)
            ot = ot * lax.rsqrt(ms + EPS) * gcol_ref[...] * (1.0 - lam_init)
            o_ref[:, h * DV_A:(h + 1) * DV_A] = ot.T.astype(BF16)


def _attn_prompt(qt, k, vt, qn, kn, lq1, lk1, lq2, lk2, gh, lam_init):
    b, _, s, _ = k.shape
    t = ATTN_TILE
    nq = s // t
    pairs = [(i, j) for i in range(nq) for j in range(i + 1)]
    qi_tab = jnp.asarray(np.array([a for a, _ in pairs], np.int32))
    kj_tab = jnp.asarray(np.array([c for _, c in pairs], np.int32))
    qspec = pl.BlockSpec((None, H_A, DV_A, t), lambda bb, p, qi, kj: (bb, 0, 0, qi[p]))
    kspec = pl.BlockSpec((None, H_A, t, DV_A), lambda bb, p, qi, kj: (bb, 0, kj[p], 0))
    vspec = pl.BlockSpec((None, H_A, DV_A, t), lambda bb, p, qi, kj: (bb, 0, 0, kj[p]))
    small = lambda shape: pl.BlockSpec(shape, lambda bb, p, qi, kj: (0, 0))
    nspec = pl.BlockSpec((None,) + qn.shape[1:], lambda bb, p, qi, kj: (bb, 0, 0))
    grid_spec = pltpu.PrefetchScalarGridSpec(
        num_scalar_prefetch=2,
        grid=(b, len(pairs)),
        in_specs=[qspec, kspec, vspec, nspec, nspec, small((1, DH_A)), small((1, DH_A)), small((1, DH_A)),
                  small((1, DH_A)), small((DV_A, 1))],
        out_specs=pl.BlockSpec((None, t, H_A * DV_A), lambda bb, p, qi, kj: (bb, qi[p], 0)),
        scratch_shapes=[pltpu.VMEM((2, H_A, DV_A, t), BF16),
                        pltpu.VMEM((2, H_A, 1, t), F32),
                        pltpu.VMEM((2, H_A, 1, t), F32),
                        pltpu.VMEM((2, H_A, DV_A, t), F32),
                        pltpu.SMEM((1,), jnp.int32)],
    )
    return pl.pallas_call(
        functools.partial(_attn_prompt_kernel, lam_init=lam_init),
        grid_spec=grid_spec,
        out_shape=jax.ShapeDtypeStruct((b, s, H_A * DV_A), BF16),
        compiler_params=pltpu.CompilerParams(
            dimension_semantics=("arbitrary", "arbitrary"), vmem_limit_bytes=V7X_VMEM_LIMIT),
        name="attn_prompt",
    )(qi_tab, kj_tab, qt, k, vt, qn, kn, lq1, lk1, lq2, lk2, gh.reshape(DV_A, 1))


def _attn_sample_kernel(q_ref, kn_ref, vn_ref, ck_ref, cv_ref, lq1_ref, lk1_ref, lq2_ref, lk2_ref,
                        gh_ref, o_ref, *, lam_init):
    lam = _lam(lq1_ref[...], lk1_ref[...], lq2_ref[...], lk2_ref[...], lam_init)
    nq = q_ref.shape[1]
    past = ck_ref.shape[0] // H_A
    zero = jnp.zeros((2 * nq, DV_A), BF16)
    for ha in range(0, H_A, 2):
        hb = ha + 1
        rows_a, rows_b = pl.ds(ha, past, stride=H_A), pl.ds(hb, past, stride=H_A)
        qa = jnp.concatenate(_split_maps(q_ref[ha], 1), axis=0)
        qb = jnp.concatenate(_split_maps(q_ref[hb], 1), axis=0)
        qz = jnp.concatenate([jnp.concatenate([qa, zero], axis=1),
                              jnp.concatenate([zero, qb], axis=1)], axis=0)
        kc = jnp.concatenate([ck_ref[rows_a, :], ck_ref[rows_b, :]], axis=1).astype(BF16)
        kn = jnp.concatenate([kn_ref[ha], kn_ref[hb]], axis=1)
        sc = lax.dot_general(qz, kc, NT_DIMS, preferred_element_type=F32)
        sn = lax.dot_general(qz, kn, NT_DIMS, preferred_element_type=F32)
        m = jnp.maximum(jnp.max(sc, axis=-1, keepdims=True), jnp.max(sn, axis=-1, keepdims=True))
        pc = jnp.exp2(sc - m)
        pn = jnp.exp2(sn - m)
        inv = 1.0 / (jnp.sum(pc, axis=-1, keepdims=True) + jnp.sum(pn, axis=-1, keepdims=True))
        pc = pc * inv
        pn = pn * inv

        def diff_maps(p):
            return jnp.concatenate([p[0:nq] - lam * p[nq:2 * nq],
                                    p[2 * nq:3 * nq] - lam * p[3 * nq:4 * nq]], axis=0)

        vc = jnp.concatenate([cv_ref[rows_a, :], cv_ref[rows_b, :]], axis=1).astype(BF16)
        vn = jnp.concatenate([vn_ref[ha], vn_ref[hb]], axis=1)
        o2 = (jnp.dot(diff_maps(pc).astype(BF16), vc, preferred_element_type=F32)
              + jnp.dot(diff_maps(pn).astype(BF16), vn, preferred_element_type=F32))
        for h, o in ((ha, o2[0:nq, 0:DV_A]), (hb, o2[nq:2 * nq, DV_A:2 * DV_A])):
            o_ref[:, h * DV_A:(h + 1) * DV_A] = (_rms(o, gh_ref[...]) * (1.0 - lam_init)).astype(BF16)


def _attn_sample(q, kn, vn, cache_k, cache_v, b_off, lq1, lk1, lq2, lk2, gh, lam_init):
    b, _, l, _ = q.shape
    hspec = pl.BlockSpec((None, H_A, l, DV_A), lambda i: (i, 0, 0, 0))
    cspec = pl.BlockSpec((None,) + cache_k.shape[1:], lambda i: (b_off + i, 0, 0))
    small = lambda shape: pl.BlockSpec(shape, lambda i: (0, 0))
    return pl.pallas_call(
        functools.partial(_attn_sample_kernel, lam_init=lam_init),
        grid=(b,),
        in_specs=[hspec, hspec, hspec, cspec, cspec, small((1, DH_A)), small((1, DH_A)),
                  small((1, DH_A)), small((1, DH_A)), small((1, DV_A))],
        out_specs=pl.BlockSpec((None, l, H_A * DV_A), lambda i: (i, 0, 0)),
        out_shape=jax.ShapeDtypeStruct((b, l, H_A * DV_A), BF16),
        compiler_params=pltpu.CompilerParams(
            dimension_semantics=("arbitrary",), vmem_limit_bytes=V7X_VMEM_LIMIT),
        name="attn_sample",
    )(q, kn, vn, cache_k, cache_v, lq1, lk1, lq2, lk2, gh)


def _split3(x):
    hi = x.astype(BF16)
    r1 = x - hi.astype(F32)
    mid = r1.astype(BF16)
    lo = (r1 - mid.astype(F32)).astype(BF16)
    return hi, mid, lo


def _mlstm_kernel(q_ref, k_ref, v_ref, og_ref, gz_ref, bias_ref, gh_ref, c0_ref, n0_ref, m0_ref,
                  h_ref, c_ref, n_ref, m_ref, c_s, n_s, m_s, *, t, bt):
    c = pl.program_id(1)

    @pl.when(c == 0)
    def _load_state():
        c_s[...] = c0_ref[...]
        n_s[...] = n0_ref[...]
        m_s[...] = m0_ref[...]

    lane = lax.broadcasted_iota(jnp.int32, (t, GATE_LANES), 1)
    row = lax.broadcasted_iota(jnp.int32, (t, t), 0)
    col = lax.broadcasted_iota(jnp.int32, (t, t), 1)
    causal = col <= row
    tril = jnp.where(causal, 1.0, 0.0).astype(BF16)
    sel = jnp.where(lax.broadcasted_iota(jnp.int32, (8, GATE_LANES), 0)
                    == lax.broadcasted_iota(jnp.int32, (8, GATE_LANES), 1), 1.0, 0.0).astype(BF16)

    gz = [gz_ref[b] + bias_ref[...] for b in range(bt)]
    lf = [jnp.minimum(g, 0.0) - jnp.log1p(jnp.exp(-jnp.abs(g))) for g in gz]
    lf = [jnp.where((lane >= H_M) & (lane < 2 * H_M), x, 0.0) for x in lf]
    bcum = [sum(jnp.dot(tril, piece, preferred_element_type=F32) for piece in _split3(x)) for x in lf]
    gc = [jnp.where(lane < H_M, gz[b], bcum[b]) for b in range(bt)]
    gr = [sum(lax.dot_general(sel, piece, NT_DIMS, preferred_element_type=F32) for piece in _split3(x))
          for x in gc]

    chains = [(b, h) for b in range(bt) for h in range(H_M)]
    ids = range(len(chains))
    sls = [slice(h * DH_M, (h + 1) * DH_M) for _, h in chains]
    ig_c = [gc[b][:, h:h + 1] for b, h in chains]
    b_c = [gc[b][:, H_M + h:H_M + h + 1] for b, h in chains]
    ig_r = [gr[b][h:h + 1, :] for b, h in chains]
    b_r = [gr[b][H_M + h:H_M + h + 1, :] for b, h in chains]
    m_prev = [m_s[b, :, h:h + 1] for b, h in chains]
    c_prev = [c_s[b, h] for b, h in chains]
    n_prev = [n_s[b, h:h + 1, :] for b, h in chains]
    q = [q_ref[b, :, sls[i]] for i, (b, _) in enumerate(chains)]
    k = [k_ref[b, :, sls[i]] for i, (b, _) in enumerate(chains)]
    v = [v_ref[b, :, sls[i]] for i, (b, _) in enumerate(chains)]

    qk = [lax.dot_general(q[i], k[i], NT_DIMS, preferred_element_type=F32) for i in ids]
    qc = [lax.dot_general(q[i], c_prev[i].astype(BF16), NT_DIMS, preferred_element_type=F32)
          for i in ids]
    d = [jnp.where(causal, b_c[i] - b_r[i] + ig_r[i], -jnp.inf) for i in ids]
    inter = [b_c[i] + m_prev[i] for i in ids]
    m_t = [jnp.maximum(inter[i], jnp.max(d[i], axis=-1, keepdims=True)) for i in ids]
    w_inter = [jnp.exp(inter[i] - m_t[i]) for i in ids]
    s = [qk[i] * jnp.exp(d[i] - m_t[i]) for i in ids]
    num = [jnp.dot(s[i].astype(BF16), v[i], preferred_element_type=F32) + w_inter[i] * qc[i]
           for i in ids]
    den = [jnp.sum(s[i], axis=-1, keepdims=True)
           + w_inter[i] * jnp.sum(q[i].astype(F32) * n_prev[i], axis=-1, keepdims=True) for i in ids]
    hh = [num[i] / jnp.maximum(jnp.abs(den[i]), jnp.exp(-m_t[i])) for i in ids]
    for i, (b, h) in enumerate(chains):
        h_ref[b, :, sls[i]] = (_rms(hh[i], gh_ref[h:h + 1, :])
                               * og_ref[b, :, sls[i]].astype(F32)).astype(BF16)

    g_last = [b_c[i][t - 1:t, :] for i in ids]
    logw = [g_last[i] - b_c[i] + ig_c[i] for i in ids]
    m_new = [jnp.maximum(g_last[i] + m_prev[i], jnp.max(logw[i], axis=0, keepdims=True)) for i in ids]
    ws = [jnp.exp(logw[i] - m_new[i]) for i in ids]
    wc = [jnp.exp(g_last[i] + m_prev[i] - m_new[i]) for i in ids]
    vw = [(v[i].astype(F32) * ws[i]).astype(BF16) for i in ids]
    for i, (b, h) in enumerate(chains):
        c_s[b, h] = wc[i] * c_prev[i] + lax.dot_general(vw[i], k[i], TN_DIMS,
                                                        preferred_element_type=F32)
        n_s[b, h:h + 1, :] = wc[i] * n_prev[i] + jnp.sum(k[i].astype(F32) * ws[i], axis=0,
                                                          keepdims=True)
        m_s[b, :, h:h + 1] = m_new[i]

    @pl.when(c == pl.num_programs(1) - 1)
    def _store_state():
        c_ref[...] = c_s[...]
        n_ref[...] = n_s[...]
        m_ref[...] = m_s[...]


def _mlstm(q, k, v, og, gz, bias, gh, c0, n0, m0):
    b, l, _ = q.shape
    t = min(MLSTM_CHUNK, l)
    nc = l // t
    bt = min(b, MLSTM_STREAMS)
    seq = lambda width: pl.BlockSpec((bt, t, width), lambda i, j: (i, j, 0))
    const = lambda shape: pl.BlockSpec(shape, lambda i, j: (0,) * len(shape))
    cspec = pl.BlockSpec((bt, H_M, DH_M, DH_M), lambda i, j: (i, 0, 0, 0))
    nspec = pl.BlockSpec((bt, H_M, DH_M), lambda i, j: (i, 0, 0))
    mspec = pl.BlockSpec((bt, 1, H_M), lambda i, j: (i, 0, 0))
    return pl.pallas_call(
        functools.partial(_mlstm_kernel, t=t, bt=bt),
        grid=(b // bt, nc),
        in_specs=[seq(D_MODEL), seq(D_MODEL), seq(D_MODEL), seq(D_MODEL), seq(GATE_LANES),
                  const((1, GATE_LANES)), const((H_M, DH_M)), cspec, nspec, mspec],
        out_specs=[seq(D_MODEL), cspec, nspec, mspec],
        out_shape=[jax.ShapeDtypeStruct((b, l, D_MODEL), BF16),
                   jax.ShapeDtypeStruct((b, H_M, DH_M, DH_M), F32),
                   jax.ShapeDtypeStruct((b, H_M, DH_M), F32),
                   jax.ShapeDtypeStruct((b, 1, H_M), F32)],
        scratch_shapes=[pltpu.VMEM((bt, H_M, DH_M, DH_M), F32),
                        pltpu.VMEM((bt, H_M, DH_M), F32),
                        pltpu.VMEM((bt, 1, H_M), F32)],
        compiler_params=pltpu.CompilerParams(
            dimension_semantics=("arbitrary", "arbitrary"), vmem_limit_bytes=V7X_VMEM_LIMIT),
        name="mlstm",
    )(q, k, v, og, gz, bias, gh, c0, n0, m0)


def _gated_gelu(x, u):
    c = -2.0 * math.sqrt(2.0 / math.pi)
    return (x * u) / (1.0 + jnp.exp(x * (c + (c * 0.044715) * (x * x))))


def _ffn_kernel(x_ref, attn_ref, hg_ref, ga_ref, gm_ref, conv0_ref, wout_ref, wup_ref, wdn_ref,
                gpm_ref, gpf_ref, gpo_ref, cw_ref, cb_ref, y_ref, cs_ref, g_s,
                *, parts, l, tiles_per_batch):
    i = pl.program_id(0)

    @pl.when(i % tiles_per_batch == 0)
    def _from_state():
        g_s[:, 6:8, :] = conv0_ref[...]

    @pl.when(i % tiles_per_batch != 0)
    def _from_prev_tile():
        g_s[:, 6:8, :] = g_s[:, l + 6:l + 8, :]

    def pre(rows):
        merged = ga_ref[rows, :] * attn_ref[rows, :] + gm_ref[rows, :] * hg_ref[rows, :]
        x1 = x_ref[rows, :] + _rms(jnp.dot(merged, wout_ref[...], preferred_element_type=F32),
                                   gpm_ref[...])
        return x1, _rms(x1, gpf_ref[...]).astype(BF16)

    def up(h2):
        return jnp.dot(h2, wup_ref[...], preferred_element_type=F32)

    def act(ug, bsl, r0, lh):
        nbh = bsl.stop - bsl.start
        g3 = ug[:, D_FF:].reshape(nbh, lh, D_FF)
        g_s[bsl, 8 + r0:8 + r0 + lh, :] = g3
        gconv = (cb_ref[...] + cw_ref[0:1, :] * g_s[bsl, 6 + r0:6 + r0 + lh, :]
                 + cw_ref[1:2, :] * g_s[bsl, 7 + r0:7 + r0 + lh, :] + cw_ref[2:3, :] * g3)
        return _gated_gelu(gconv.reshape(nbh * lh, D_FF), ug[:, :D_FF]).astype(BF16)

    def post(x1, a, rows):
        ff = jnp.dot(a, wdn_ref[...], preferred_element_type=F32)
        y_ref[rows, :] = x1 + _rms(ff, gpo_ref[...])

    n = len(parts)
    x1, h2, ug = [None] * n, [None] * n, [None] * n
    x1[0], h2[0] = pre(parts[0][0])
    for i in range(n + 1):
        if i + 1 < n:
            x1[i + 1], h2[i + 1] = pre(parts[i + 1][0])
        if i < n:
            ug[i] = up(h2[i])
        if i >= 1:
            rows, *conv = parts[i - 1]
            post(x1[i - 1], act(ug[i - 1], *conv), rows)
    cs_ref[...] = g_s[:, l + 6:l + 8, :]


def _ffn(x2d, attn, hg, ga, gm, conv0, wout, wup, wdn, gpm, gpf, gpo, cw, cb, nb_total, seq):
    n = x2d.shape[0]
    np_ = FFN_PARTS
    if seq >= FFN_ROWS:
        tm, nb, l = FFN_ROWS, 1, FFN_ROWS
        tiles_per_batch = seq // tm
        pl_ = l // np_
        parts = tuple((slice(j * pl_, (j + 1) * pl_), slice(0, 1), j * pl_, pl_) for j in range(np_))
    else:
        tm, nb, l = n, nb_total, seq
        tiles_per_batch = 1
        pb = nb // np_
        parts = tuple((slice(j * pb * l, (j + 1) * pb * l), slice(j * pb, (j + 1) * pb), 0, l)
                      for j in range(np_))
    row = pl.BlockSpec((tm, D_MODEL), lambda i: (i, 0))
    cstate = pl.BlockSpec((nb, CONV_W - 1, D_FF), lambda i: (i // tiles_per_batch, 0, 0))
    return pl.pallas_call(
        functools.partial(_ffn_kernel, parts=parts, l=l, tiles_per_batch=tiles_per_batch),
        grid=(n // tm,),
        in_specs=[row, row, row, row, row, cstate,
                  _resident((D_MODEL, D_MODEL)), _resident((D_MODEL, 2 * D_FF)),
                  _resident((D_FF, D_MODEL)), _resident((1, D_MODEL)), _resident((1, D_MODEL)),
                  _resident((1, D_MODEL)), _resident((CONV_W, D_FF)), _resident((1, D_FF))],
        out_specs=[row, cstate],
        out_shape=[jax.ShapeDtypeStruct((n, D_MODEL), F32),
                   jax.ShapeDtypeStruct((nb_total, CONV_W - 1, D_FF), F32)],
        scratch_shapes=[pltpu.VMEM((nb, l + 8, D_FF), F32)],
        compiler_params=pltpu.CompilerParams(
            dimension_semantics=("arbitrary",), vmem_limit_bytes=V7X_VMEM_LIMIT),
        name="merge_ffn",
    )(x2d, attn, hg, ga, gm, conv0, wout, wup, wdn, gpm, gpf, gpo, cw, cb)


def _layer(x, caches, c0, n0, m0, conv0, wts, lam_init):
    (g_pre_mix, wa, wb, wg, gate_bias, lq1, lk1, lq2, lk2, g_attn_head, g_mlstm_head, wout, g_post_mix,
     g_pre_ffn, wup, conv_w, conv_b, wdn, g_post_ffn) = wts
    b, l, _ = x.shape
    x2d = x.reshape(b * l, D_MODEL)
    (qa, kf, kb, vf, vb, qm, km, vm, om, ga, gm, gz, qn, kn) = _inproj(
        x2d, g_pre_mix, wa, wb, wg, b, l)

    if caches is None:
        per_b = qn.shape[0] // b
        attn = _attn_prompt(qa, kb, vb, qn.reshape(b, per_b, GATE_LANES), kn.reshape(b, per_b, GATE_LANES),
                            lq1, lk1, lq2, lk2, g_attn_head, lam_init)
    else:
        cache_k, cache_v, b_off = caches
        attn = _attn_sample(qa, kb, vb, cache_k, cache_v, b_off,
                            lq1, lk1, lq2, lk2, g_attn_head, lam_init)

    seq3 = lambda a: a.reshape(b, l, a.shape[-1])
    hm, c1, n1, m1 = _mlstm(seq3(qm), seq3(km), seq3(vm), seq3(om), seq3(gz), gate_bias, g_mlstm_head,
                            c0, n0, m0.reshape(b, 1, H_M))

    y, conv1 = _ffn(x2d, attn.reshape(b * l, D_MODEL), hm.reshape(b * l, D_MODEL), ga, gm,
                    conv0, wout, wup, wdn, g_post_mix, g_pre_ffn, g_post_ffn, conv_w, conv_b, b, l)
    return (y.reshape(b, l, D_MODEL), kf.reshape(b, l, H_A, DV_A), vf.reshape(b, l, H_A, DV_A),
            c1, n1, m1.reshape(b, H_M), conv1)


def kernel(x_prompt, x_sample, cache_k, cache_v, state_C, state_n, state_m, state_conv, g_pre_mix, w_in, b_gates, lam_q1, lam_k1, lam_q2, lam_k2, g_attn_head, g_mlstm_head, w_out, g_post_mix, g_pre_ffn, w_up, conv_w, conv_b, w_down, g_post_ffn):
    depth = w_in.shape[0]
    bp = x_prompt.shape[0]
    yp, ys = x_prompt, x_sample
    outs_p, outs_s = [], []
    n_main = 6 * D_MODEL
    for li in range(depth):
        lam_init = 0.8 - 0.6 * math.exp(-0.3 * li)
        w = w_in[li]
        wa = w[:, :n_main].astype(BF16)
        wb = w[:, n_main + 2 * H_M:].astype(BF16)
        wg = jnp.pad(w[:, n_main:n_main + 2 * H_M], ((0, 0), (0, GATE_LANES - 2 * H_M))).astype(BF16)
        gate_bias = jnp.concatenate(
            [b_gates[li], jnp.zeros((GATE_LANES - 2 * H_M,), F32)]).reshape(1, GATE_LANES)
        row = lambda a: a.reshape(1, -1)
        wts = (row(g_pre_mix[li]), wa, wb, wg, gate_bias, row(lam_q1[li]), row(lam_k1[li]),
               row(lam_q2[li]), row(lam_k2[li]), row(g_attn_head[li]), g_mlstm_head[li],
               w_out[li].astype(BF16), row(g_post_mix[li]), row(g_pre_ffn[li]),
               w_up[li].astype(BF16), conv_w[li], row(conv_b[li]), w_down[li].astype(BF16),
               row(g_post_ffn[li]))
        c0 = jnp.zeros((bp, H_M, DH_M, DH_M), F32)
        n0 = jnp.zeros((bp, H_M, DH_M), F32)
        m0 = jnp.zeros((bp, H_M), F32)
        conv0 = jnp.zeros((bp, CONV_W - 1, D_FF), F32)
        yp, *sp = _layer(yp, None, c0, n0, m0, conv0, wts, lam_init)
        bs, past = cache_k.shape[1], cache_k.shape[2]
        caches = (cache_k.reshape(depth * bs, past * H_A, DV_A),
                  cache_v.reshape(depth * bs, past * H_A, DV_A), li * bs)
        ys, *ss = _layer(ys, caches, state_C[li], state_n[li], state_m[li],
                         state_conv[li], wts, lam_init)
        outs_p.append(sp)
        outs_s.append(ss)
    k_p, v_p, c_p, n_p, m_p, conv_p = [jnp.stack([o[i] for o in outs_p]) for i in range(6)]
    k_s, v_s, c_s, n_s, m_s, conv_s = [jnp.stack([o[i] for o in outs_s]) for i in range(6)]
    return (yp, ys, k_p, v_p, c_p, n_p, m_p, conv_p, k_s, v_s, c_s, n_s, m_s, conv_s)
```

```python
import functools
import math

import jax
import jax.numpy as jnp
import numpy as np
from jax import lax
from jax.experimental import pallas as pl
from jax.experimental.pallas import tpu as pltpu

F32 = jnp.float32
BF16 = jnp.bfloat16

D_MODEL = 1024
CHUNK = 64
H_A = 8
DH_A = 64
DV_A = 2 * DH_A
H_M = 4
DH_M = D_MODEL // H_M
D_FF = 2816
CONV_W = 3
EPS = 1e-6
N_SEG_A = 6
GATE_LANES = 128
QK_SCALE = DH_A ** -0.5 * math.log2(math.e)

V7X_VMEM_LIMIT = 56 * 1024 * 1024

PROJ_ROWS = 256
ATTN_TILE = 512
SCORE_BOUND = 80.0
MLSTM_CHUNK = 256
MLSTM_STREAMS = 4
FFN_ROWS = 512
FFN_PARTS = 4

NT_DIMS = (((1,), (1,)), ((), ()))
TN_DIMS = (((0,), (0,)), ((), ()))


def _rms(x, g):
    return x * lax.rsqrt(jnp.mean(x * x, axis=-1, keepdims=True) + EPS) * g


def _resident(shape):
    nd = len(shape)
    return pl.BlockSpec(shape, lambda *_: (0,) * nd, pipeline_mode=pl.Buffered(1))


def _inproj_kernel(x_ref, g_ref, wa_ref, wb_ref, wg_ref, qa_ref, kf_ref, kb_ref, vf_ref, vb_ref,
                   qm_ref, km_ref, vm_ref, om_ref, ga_ref, gm_ref, gz_ref, qn_ref, kn_ref,
                   *, nbt, lt, transposed):
    hb = _rms(x_ref[...], g_ref[...]).astype(BF16)

    def seg(j):
        w_ref, jj = (wa_ref, j) if j < N_SEG_A else (wb_ref, j - N_SEG_A)
        return jnp.dot(hb, w_ref[:, jj * D_MODEL:(jj + 1) * D_MODEL], preferred_element_type=F32)

    def put_heads(ref, val, transpose):
        for h in range(H_A):
            piece = val[:, h * DV_A:(h + 1) * DV_A]
            if transpose:
                ref[0, h] = piece.T.astype(BF16)
            else:
                ref[:, h] = piece.astype(BF16).reshape(nbt, lt, DV_A)

    def max_sq_norm(val):
        lane = lax.broadcasted_iota(jnp.int32, (1, GATE_LANES), 1)
        out = jnp.zeros((1, GATE_LANES), F32)
        for h in range(H_A):
            piece = val[:, h * DV_A:(h + 1) * DV_A]
            top = jnp.max(jnp.sum(piece * piece, axis=-1, keepdims=True), axis=0, keepdims=True)
            out = jnp.where(lane == h, top, out)
        return out

    q = seg(0) * QK_SCALE
    put_heads(qa_ref, q, transposed)
    qn_ref[0] = max_sq_norm(q)
    k = seg(1)
    kf_ref[...] = k
    put_heads(kb_ref, k, False)
    kn_ref[0] = max_sq_norm(k)
    v = seg(2)
    vf_ref[...] = v
    put_heads(vb_ref, v, transposed)
    qm_ref[...] = (seg(3) * DH_M ** -0.5).astype(BF16)
    km_ref[...] = seg(4).astype(BF16)
    vm_ref[...] = seg(5).astype(BF16)
    om_ref[...] = jax.nn.sigmoid(seg(6)).astype(BF16)
    ga_ref[...] = jax.nn.sigmoid(seg(7)).astype(BF16)
    gm_ref[...] = jax.nn.sigmoid(seg(8)).astype(BF16)
    gz_ref[...] = jnp.dot(hb, wg_ref[...], preferred_element_type=F32)


def _inproj(x2d, g, wa, wb, wg, nb, seq):
    n = x2d.shape[0]
    tm = PROJ_ROWS
    transposed = seq >= tm
    if transposed:
        nbt, lt = 1, tm
        per_b = seq // tm
        head = pl.BlockSpec((1, H_A, lt, DV_A), lambda i: (i // per_b, 0, i % per_b, 0))
        head_t = pl.BlockSpec((1, H_A, DV_A, lt), lambda i: (i // per_b, 0, 0, i % per_b))
        hm16_t = jax.ShapeDtypeStruct((nb, H_A, DV_A, seq), BF16)
    else:
        nbt, lt = tm // seq, seq
        head = head_t = pl.BlockSpec((nbt, H_A, lt, DV_A), lambda i: (i, 0, 0, 0))
        hm16_t = jax.ShapeDtypeStruct((nb, H_A, seq, DV_A), BF16)
    row = pl.BlockSpec((tm, D_MODEL), lambda i: (i, 0))
    norm = pl.BlockSpec((1, 1, GATE_LANES), lambda i: (i, 0, 0))
    norm_shape = jax.ShapeDtypeStruct((n // tm, 1, GATE_LANES), F32)
    full32 = jax.ShapeDtypeStruct((n, D_MODEL), F32)
    full16 = jax.ShapeDtypeStruct((n, D_MODEL), BF16)
    hm16 = jax.ShapeDtypeStruct((nb, H_A, seq, DV_A), BF16)
    return pl.pallas_call(
        functools.partial(_inproj_kernel, nbt=nbt, lt=lt, transposed=transposed),
        grid=(n // tm,),
        in_specs=[row, _resident((1, D_MODEL)), _resident(wa.shape), _resident(wb.shape),
                  _resident(wg.shape)],
        out_specs=[head_t, row, head, row, head_t, row, row, row, row, row, row,
                   pl.BlockSpec((tm, GATE_LANES), lambda i: (i, 0)), norm, norm],
        out_shape=[hm16_t, full32, hm16, full32, hm16_t, full16, full16, full16,
                   full16, full16, full16, jax.ShapeDtypeStruct((n, GATE_LANES), F32),
                   norm_shape, norm_shape],
        compiler_params=pltpu.CompilerParams(
            dimension_semantics=("arbitrary",), vmem_limit_bytes=V7X_VMEM_LIMIT),
        name="inproj",
    )(x2d, g, wa, wb, wg)


def _lam(lq1, lk1, lq2, lk2, lam_init):
    return (jnp.exp(jnp.sum(lq1 * lk1, axis=-1, keepdims=True))
            - jnp.exp(jnp.sum(lq2 * lk2, axis=-1, keepdims=True)) + lam_init)


def _split_maps(q, axis):
    idx = lax.broadcasted_iota(jnp.int32, q.shape, axis)
    zero = jnp.zeros_like(q)
    return jnp.where(idx < DH_A, q, zero), jnp.where(idx >= DH_A, q, zero)


def _attn_prompt_kernel(qi_tab, kj_tab, qt_ref, k_ref, vt_ref, qn_ref, kn_ref, lq1_ref, lk1_ref, lq2_ref,
                        lk2_ref, gcol_ref, o_ref, qz_s, m_s, l_s, acc_s, mode_s, *, lam_init):
    p = pl.program_id(1)
    qi = qi_tab[p]
    kj = kj_tab[p]
    t = ATTN_TILE
    diag = kj == qi
    pairs = [(h, mp) for h in range(H_A) for mp in range(2)]
    n = len(pairs)

    @pl.when(kj == 0)
    def _init():
        for h in range(H_A):
            q1, q2 = _split_maps(qt_ref[h], 0)
            qz_s[0, h] = q1
            qz_s[1, h] = q2
        m_s[...] = jnp.full(m_s.shape, -jnp.inf, F32)
        l_s[...] = jnp.zeros(l_s.shape, F32)
        acc_s[...] = jnp.zeros(acc_s.shape, F32)
        bound_sq = (jnp.max(qn_ref[...], axis=0, keepdims=True)
                    * jnp.max(kn_ref[...], axis=0, keepdims=True))
        mode_s[0] = jnp.where(jnp.max(bound_sq) <= SCORE_BOUND * SCORE_BOUND, 1, 0)

    def chunk_mask():
        kr = lax.broadcasted_iota(jnp.int32, (t, t), 0) // CHUNK
        qc = lax.broadcasted_iota(jnp.int32, (t, t), 1) // CHUNK
        return kr <= qc

    def scores(h, mp, visible):
        s = jnp.dot(k_ref[h], qz_s[mp, h], preferred_element_type=F32)
        return s if visible is None else jnp.where(visible, s, -jnp.inf)

    def bounded_pass(visible):
        s_next = scores(*pairs[0], visible)
        for i, (h, mp) in enumerate(pairs):
            s = s_next
            if i + 1 < n:
                s_next = scores(*pairs[i + 1], visible)
            pr = jnp.exp2(s)
            l_s[mp, h] = l_s[mp, h] + jnp.sum(pr, axis=0, keepdims=True)
            acc_s[mp, h] = acc_s[mp, h] + jnp.dot(
                vt_ref[h], pr.astype(BF16), preferred_element_type=F32)

    def bounded_diag_pass():
        hq = t // 2
        visible = chunk_mask()
        vis_lo, vis_hi = visible[:hq, :hq], visible[:, hq:]

        def halves(h, mp):
            qz = qz_s[mp, h]
            s_lo = jnp.dot(k_ref[h, :hq, :], qz[:, :hq], preferred_element_type=F32)
            s_hi = jnp.dot(k_ref[h], qz[:, hq:], preferred_element_type=F32)
            return jnp.where(vis_lo, s_lo, -jnp.inf), jnp.where(vis_hi, s_hi, -jnp.inf)

        s_next = halves(*pairs[0])
        for i, (h, mp) in enumerate(pairs):
            s_lo, s_hi = s_next
            if i + 1 < n:
                s_next = halves(*pairs[i + 1])
            p_lo, p_hi = jnp.exp2(s_lo), jnp.exp2(s_hi)
            l_s[mp, h] = l_s[mp, h] + jnp.concatenate(
                [jnp.sum(p_lo, axis=0, keepdims=True), jnp.sum(p_hi, axis=0, keepdims=True)], axis=1)
            acc_s[mp, h] = acc_s[mp, h] + jnp.concatenate(
                [jnp.dot(vt_ref[h, :, :hq], p_lo.astype(BF16), preferred_element_type=F32),
                 jnp.dot(vt_ref[h], p_hi.astype(BF16), preferred_element_type=F32)], axis=1)

    def online_pass(visible):
        s_next = scores(*pairs[0], visible)
        for i, (h, mp) in enumerate(pairs):
            s = s_next
            m_old = m_s[mp, h]
            m_new = jnp.maximum(m_old, jnp.max(s, axis=0, keepdims=True))
            if i + 1 < n:
                s_next = scores(*pairs[i + 1], visible)
            alpha = jnp.exp2(m_old - m_new)
            pr = jnp.exp2(s - m_new)
            l_s[mp, h] = alpha * l_s[mp, h] + jnp.sum(pr, axis=0, keepdims=True)
            acc_s[mp, h] = alpha * acc_s[mp, h] + jnp.dot(
                vt_ref[h], pr.astype(BF16), preferred_element_type=F32)
            m_s[mp, h] = m_new

    bounded = mode_s[0] == 1
    off_diag = jnp.logical_not(diag)

    @pl.when(jnp.logical_and(bounded, off_diag))
    def _bounded_full():
        bounded_pass(None)

    @pl.when(jnp.logical_and(bounded, diag))
    def _bounded_diag():
        bounded_diag_pass()

    @pl.when(jnp.logical_and(jnp.logical_not(bounded), off_diag))
    def _online_full():
        online_pass(None)

    @pl.when(jnp.logical_and(jnp.logical_not(bounded), diag))
    def _online_diag():
        online_pass(chunk_mask())

    @pl.when(diag)
    def _finish():
        lam = _lam(lq1_ref[...], lk1_ref[...], lq2_ref[...], lk2_ref[...], lam_init)
        for h in range(H_A):
            ot = acc_s[0, h] / l_s[0, h] - lam * (acc_s[1, h] / l_s[1, h])
            ms = jnp.mean(ot * ot, axis=0, keepdims=True)
            ot = ot * lax.rsqrt(ms + EPS) * gcol_ref[...] * (1.0 - lam_init)
            o_ref[:, h * DV_A:(h + 1) * DV_A] = ot.T.astype(BF16)


def _attn_prompt(qt, k, vt, qn, kn, lq1, lk1, lq2, lk2, gh, lam_init):
    b, _, s, _ = k.shape
    t = ATTN_TILE
    nq = s // t
    pairs = [(i, j) for i in range(nq) for j in range(i + 1)]
    qi_tab = jnp.asarray(np.array([a for a, _ in pairs], np.int32))
    kj_tab = jnp.asarray(np.array([c for _, c in pairs], np.int32))
    qspec = pl.BlockSpec((None, H_A, DV_A, t), lambda bb, p, qi, kj: (bb, 0, 0, qi[p]))
    kspec = pl.BlockSpec((None, H_A, t, DV_A), lambda bb, p, qi, kj: (bb, 0, kj[p], 0))
    vspec = pl.BlockSpec((None, H_A, DV_A, t), lambda bb, p, qi, kj: (bb, 0, 0, kj[p]))
    small = lambda shape: pl.BlockSpec(shape, lambda bb, p, qi, kj: (0, 0))
    nspec = pl.BlockSpec((None,) + qn.shape[1:], lambda bb, p, qi, kj: (bb, 0, 0))
    grid_spec = pltpu.PrefetchScalarGridSpec(
        num_scalar_prefetch=2,
        grid=(b, len(pairs)),
        in_specs=[qspec, kspec, vspec, nspec, nspec, small((1, DH_A)), small((1, DH_A)), small((1, DH_A)),
                  small((1, DH_A)), small((DV_A, 1))],
        out_specs=pl.BlockSpec((None, t, H_A * DV_A), lambda bb, p, qi, kj: (bb, qi[p], 0)),
        scratch_shapes=[pltpu.VMEM((2, H_A, DV_A, t), BF16),
                        pltpu.VMEM((2, H_A, 1, t), F32),
                        pltpu.VMEM((2, H_A, 1, t), F32),
                        pltpu.VMEM((2, H_A, DV_A, t), F32),
                        pltpu.SMEM((1,), jnp.int32)],
    )
    return pl.pallas_call(
        functools.partial(_attn_prompt_kernel, lam_init=lam_init),
        grid_spec=grid_spec,
        out_shape=jax.ShapeDtypeStruct((b, s, H_A * DV_A), BF16),
        compiler_params=pltpu.CompilerParams(
            dimension_semantics=("arbitrary", "arbitrary"), vmem_limit_bytes=V7X_VMEM_LIMIT),
        name="attn_prompt",
    )(qi_tab, kj_tab, qt, k, vt, qn, kn, lq1, lk1, lq2, lk2, gh.reshape(DV_A, 1))


def _attn_sample_kernel(q_ref, kn_ref, vn_ref, ck_ref, cv_ref, lq1_ref, lk1_ref, lq2_ref, lk2_ref,
                        gh_ref, o_ref, *, lam_init):
    lam = _lam(lq1_ref[...], lk1_ref[...], lq2_ref[...], lk2_ref[...], lam_init)
    nq = q_ref.shape[1]
    past = ck_ref.shape[0] // H_A
    zero = jnp.zeros((2 * nq, DV_A), BF16)
    for ha in range(0, H_A, 2):
        hb = ha + 1
        rows_a, rows_b = pl.ds(ha, past, stride=H_A), pl.ds(hb, past, stride=H_A)
        qa = jnp.concatenate(_split_maps(q_ref[ha], 1), axis=0)
        qb = jnp.concatenate(_split_maps(q_ref[hb], 1), axis=0)
        qz = jnp.concatenate([jnp.concatenate([qa, zero], axis=1),
                              jnp.concatenate([zero, qb], axis=1)], axis=0)
        kc = jnp.concatenate([ck_ref[rows_a, :], ck_ref[rows_b, :]], axis=1).astype(BF16)
        kn = jnp.concatenate([kn_ref[ha], kn_ref[hb]], axis=1)
        sc = lax.dot_general(qz, kc, NT_DIMS, preferred_element_type=F32)
        sn = lax.dot_general(qz, kn, NT_DIMS, preferred_element_type=F32)
        m = jnp.maximum(jnp.max(sc, axis=-1, keepdims=True), jnp.max(sn, axis=-1, keepdims=True))
        pc = jnp.exp2(sc - m)
        pn = jnp.exp2(sn - m)
        inv = 1.0 / (jnp.sum(pc, axis=-1, keepdims=True) + jnp.sum(pn, axis=-1, keepdims=True))
        pc = pc * inv
        pn = pn * inv

        def diff_maps(p):
            return jnp.concatenate([p[0:nq] - lam * p[nq:2 * nq],
                                    p[2 * nq:3 * nq] - lam * p[3 * nq:4 * nq]], axis=0)

        vc = jnp.concatenate([cv_ref[rows_a, :], cv_ref[rows_b, :]], axis=1).astype(BF16)
        vn = jnp.concatenate([vn_ref[ha], vn_ref[hb]], axis=1)
        o2 = (jnp.dot(diff_maps(pc).astype(BF16), vc, preferred_element_type=F32)
              + jnp.dot(diff_maps(pn).astype(BF16), vn, preferred_element_type=F32))
        for h, o in ((ha, o2[0:nq, 0:DV_A]), (hb, o2[nq:2 * nq, DV_A:2 * DV_A])):
            o_ref[:, h * DV_A:(h + 1) * DV_A] = (_rms(o, gh_ref[...]) * (1.0 - lam_init)).astype(BF16)


def _attn_sample(q, kn, vn, cache_k, cache_v, b_off, lq1, lk1, lq2, lk2, gh, lam_init):
    b, _, l, _ = q.shape
    hspec = pl.BlockSpec((None, H_A, l, DV_A), lambda i: (i, 0, 0, 0))
    cspec = pl.BlockSpec((None,) + cache_k.shape[1:], lambda i: (b_off + i, 0, 0))
    small = lambda shape: pl.BlockSpec(shape, lambda i: (0, 0))
    return pl.pallas_call(
        functools.partial(_attn_sample_kernel, lam_init=lam_init),
        grid=(b,),
        in_specs=[hspec, hspec, hspec, cspec, cspec, small((1, DH_A)), small((1, DH_A)),
                  small((1, DH_A)), small((1, DH_A)), small((1, DV_A))],
        out_specs=pl.BlockSpec((None, l, H_A * DV_A), lambda i: (i, 0, 0)),
        out_shape=jax.ShapeDtypeStruct((b, l, H_A * DV_A), BF16),
        compiler_params=pltpu.CompilerParams(
            dimension_semantics=("arbitrary",), vmem_limit_bytes=V7X_VMEM_LIMIT),
        name="attn_sample",
    )(q, kn, vn, cache_k, cache_v, lq1, lk1, lq2, lk2, gh)


def _split3(x):
    hi = x.astype(BF16)
    r1 = x - hi.astype(F32)
    mid = r1.astype(BF16)
    lo = (r1 - mid.astype(F32)).astype(BF16)
    return hi, mid, lo


def _mlstm_kernel(q_ref, k_ref, v_ref, og_ref, gz_ref, bias_ref, gh_ref, c0_ref, n0_ref, m0_ref,
                  h_ref, c_ref, n_ref, m_ref, c_s, n_s, m_s, *, t, bt):
    c = pl.program_id(1)

    @pl.when(c == 0)
    def _load_state():
        c_s[...] = c0_ref[...]
        n_s[...] = n0_ref[...]
        m_s[...] = m0_ref[...]

    lane = lax.broadcasted_iota(jnp.int32, (t, GATE_LANES), 1)
    row = lax.broadcasted_iota(jnp.int32, (t, t), 0)
    col = lax.broadcasted_iota(jnp.int32, (t, t), 1)
    causal = col <= row
    tril = jnp.where(causal, 1.0, 0.0).astype(BF16)
    sel = jnp.where(lax.broadcasted_iota(jnp.int32, (8, GATE_LANES), 0)
                    == lax.broadcasted_iota(jnp.int32, (8, GATE_LANES), 1), 1.0, 0.0).astype(BF16)

    gz = [gz_ref[b] + bias_ref[...] for b in range(bt)]
    lf = [jnp.minimum(g, 0.0) - jnp.log1p(jnp.exp(-jnp.abs(g))) for g in gz]
    lf = [jnp.where((lane >= H_M) & (lane < 2 * H_M), x, 0.0) for x in lf]
    bcum = [sum(jnp.dot(tril, piece, preferred_element_type=F32) for piece in _split3(x)) for x in lf]
    gc = [jnp.where(lane < H_M, gz[b], bcum[b]) for b in range(bt)]
    gr = [sum(lax.dot_general(sel, piece, NT_DIMS, preferred_element_type=F32) for piece in _split3(x))
          for x in gc]

    chains = [(b, h) for b in range(bt) for h in range(H_M)]
    ids = range(len(chains))
    sls = [slice(h * DH_M, (h + 1) * DH_M) for _, h in chains]
    ig_c = [gc[b][:, h:h + 1] for b, h in chains]
    b_c = [gc[b][:, H_M + h:H_M + h + 1] for b, h in chains]
    ig_r = [gr[b][h:h + 1, :] for b, h in chains]
    b_r = [gr[b][H_M + h:H_M + h + 1, :] for b, h in chains]
    m_prev = [m_s[b, :, h:h + 1] for b, h in chains]
    c_prev = [c_s[b, h] for b, h in chains]
    n_prev = [n_s[b, h:h + 1, :] for b, h in chains]
    q = [q_ref[b, :, sls[i]] for i, (b, _) in enumerate(chains)]
    k = [k_ref[b, :, sls[i]] for i, (b, _) in enumerate(chains)]
    v = [v_ref[b, :, sls[i]] for i, (b, _) in enumerate(chains)]

    qk = [lax.dot_general(q[i], k[i], NT_DIMS, preferred_element_type=F32) for i in ids]
    qc = [lax.dot_general(q[i], c_prev[i].astype(BF16), NT_DIMS, preferred_element_type=F32)
          for i in ids]
    d = [jnp.where(causal, b_c[i] - b_r[i] + ig_r[i], -jnp.inf) for i in ids]
    inter = [b_c[i] + m_prev[i] for i in ids]
    m_t = [jnp.maximum(inter[i], jnp.max(d[i], axis=-1, keepdims=True)) for i in ids]
    w_inter = [jnp.exp(inter[i] - m_t[i]) for i in ids]
    s = [qk[i] * jnp.exp(d[i] - m_t[i]) for i in ids]
    num = [jnp.dot(s[i].astype(BF16), v[i], preferred_element_type=F32) + w_inter[i] * qc[i]
           for i in ids]
    den = [jnp.sum(s[i], axis=-1, keepdims=True)
           + w_inter[i] * jnp.sum(q[i].astype(F32) * n_prev[i], axis=-1, keepdims=True) for i in ids]
    hh = [num[i] / jnp.maximum(jnp.abs(den[i]), jnp.exp(-m_t[i])) for i in ids]
    for i, (b, h) in enumerate(chains):
        h_ref[b, :, sls[i]] = (_rms(hh[i], gh_ref[h:h + 1, :])
                               * og_ref[b, :, sls[i]].astype(F32)).astype(BF16)

    g_last = [b_c[i][t - 1:t, :] for i in ids]
    logw = [g_last[i] - b_c[i] + ig_c[i] for i in ids]
    m_new = [jnp.maximum(g_last[i] + m_prev[i], jnp.max(logw[i], axis=0, keepdims=True)) for i in ids]
    ws = [jnp.exp(logw[i] - m_new[i]) for i in ids]
    wc = [jnp.exp(g_last[i] + m_prev[i] - m_new[i]) for i in ids]
    vw = [(v[i].astype(F32) * ws[i]).astype(BF16) for i in ids]
    for i, (b, h) in enumerate(chains):
        c_s[b, h] = wc[i] * c_prev[i] + lax.dot_general(vw[i], k[i], TN_DIMS,
                                                        preferred_element_type=F32)
        n_s[b, h:h + 1, :] = wc[i] * n_prev[i] + jnp.sum(k[i].astype(F32) * ws[i], axis=0,
                                                          keepdims=True)
        m_s[b, :, h:h + 1] = m_new[i]

    @pl.when(c == pl.num_programs(1) - 1)
    def _store_state():
        c_ref[...] = c_s[...]
        n_ref[...] = n_s[...]
        m_ref[...] = m_s[...]


def _mlstm(q, k, v, og, gz, bias, gh, c0, n0, m0):
    b, l, _ = q.shape
    t = min(MLSTM_CHUNK, l)
    nc = l // t
    bt = min(b, MLSTM_STREAMS)
    seq = lambda width: pl.BlockSpec((bt, t, width), lambda i, j: (i, j, 0))
    const = lambda shape: pl.BlockSpec(shape, lambda i, j: (0,) * len(shape))
    cspec = pl.BlockSpec((bt, H_M, DH_M, DH_M), lambda i, j: (i, 0, 0, 0))
    nspec = pl.BlockSpec((bt, H_M, DH_M), lambda i, j: (i, 0, 0))
    mspec = pl.BlockSpec((bt, 1, H_M), lambda i, j: (i, 0, 0))
    return pl.pallas_call(
        functools.partial(_mlstm_kernel, t=t, bt=bt),
        grid=(b // bt, nc),
        in_specs=[seq(D_MODEL), seq(D_MODEL), seq(D_MODEL), seq(D_MODEL), seq(GATE_LANES),
                  const((1, GATE_LANES)), const((H_M, DH_M)), cspec, nspec, mspec],
        out_specs=[seq(D_MODEL), cspec, nspec, mspec],
        out_shape=[jax.ShapeDtypeStruct((b, l, D_MODEL), BF16),
                   jax.ShapeDtypeStruct((b, H_M, DH_M, DH_M), F32),
                   jax.ShapeDtypeStruct((b, H_M, DH_M), F32),
                   jax.ShapeDtypeStruct((b, 1, H_M), F32)],
        scratch_shapes=[pltpu.VMEM((bt, H_M, DH_M, DH_M), F32),
                        pltpu.VMEM((bt, H_M, DH_M), F32),
                        pltpu.VMEM((bt, 1, H_M), F32)],
        compiler_params=pltpu.CompilerParams(
            dimension_semantics=("arbitrary", "arbitrary"), vmem_limit_bytes=V7X_VMEM_LIMIT),
        name="mlstm",
    )(q, k, v, og, gz, bias, gh, c0, n0, m0)


def _gated_gelu(x, u):
    c = -2.0 * math.sqrt(2.0 / math.pi)
    return (x * u) / (1.0 + jnp.exp(x * (c + (c * 0.044715) * (x * x))))


def _ffn_kernel(x_ref, attn_ref, hg_ref, ga_ref, gm_ref, conv0_ref, wout_ref, wup_ref, wdn_ref,
                gpm_ref, gpf_ref, gpo_ref, cw_ref, cb_ref, y_ref, cs_ref, g_s,
                *, parts, l, tiles_per_batch):
    i = pl.program_id(0)

    @pl.when(i % tiles_per_batch == 0)
    def _from_state():
        g_s[:, 6:8, :] = conv0_ref[...]

    @pl.when(i % tiles_per_batch != 0)
    def _from_prev_tile():
        g_s[:, 6:8, :] = g_s[:, l + 6:l + 8, :]

    def pre(rows):
        merged = ga_ref[rows, :] * attn_ref[rows, :] + gm_ref[rows, :] * hg_ref[rows, :]
        x1 = x_ref[rows, :] + _rms(jnp.dot(merged, wout_ref[...], preferred_element_type=F32),
                                   gpm_ref[...])
        return x1, _rms(x1, gpf_ref[...]).astype(BF16)

    def up(h2):
        return jnp.dot(h2, wup_ref[...], preferred_element_type=F32)

    def act(ug, bsl, r0, lh):
        nbh = bsl.stop - bsl.start
        g3 = ug[:, D_FF:].reshape(nbh, lh, D_FF)
        g_s[bsl, 8 + r0:8 + r0 + lh, :] = g3
        gconv = (cb_ref[...] + cw_ref[0:1, :] * g_s[bsl, 6 + r0:6 + r0 + lh, :]
                 + cw_ref[1:2, :] * g_s[bsl, 7 + r0:7 + r0 + lh, :] + cw_ref[2:3, :] * g3)
        return _gated_gelu(gconv.reshape(nbh * lh, D_FF), ug[:, :D_FF]).astype(BF16)

    def post(x1, a, rows):
        ff = jnp.dot(a, wdn_ref[...], preferred_element_type=F32)
        y_ref[rows, :] = x1 + _rms(ff, gpo_ref[...])

    n = len(parts)
    x1, h2, ug = [None] * n, [None] * n, [None] * n
    x1[0], h2[0] = pre(parts[0][0])
    for i in range(n + 1):
        if i + 1 < n:
            x1[i + 1], h2[i + 1] = pre(parts[i + 1][0])
        if i < n:
            ug[i] = up(h2[i])
        if i >= 1:
            rows, *conv = parts[i - 1]
            post(x1[i - 1], act(ug[i - 1], *conv), rows)
    cs_ref[...] = g_s[:, l + 6:l + 8, :]


def _ffn(x2d, attn, hg, ga, gm, conv0, wout, wup, wdn, gpm, gpf, gpo, cw, cb, nb_total, seq):
    n = x2d.shape[0]
    np_ = FFN_PARTS
    if seq >= FFN_ROWS:
        tm, nb, l = FFN_ROWS, 1, FFN_ROWS
        tiles_per_batch = seq // tm
        pl_ = l // np_
        parts = tuple((slice(j * pl_, (j + 1) * pl_), slice(0, 1), j * pl_, pl_) for j in range(np_))
    else:
        tm, nb, l = n, nb_total, seq
        tiles_per_batch = 1
        pb = nb // np_
        parts = tuple((slice(j * pb * l, (j + 1) * pb * l), slice(j * pb, (j + 1) * pb), 0, l)
                      for j in range(np_))
    row = pl.BlockSpec((tm, D_MODEL), lambda i: (i, 0))
    cstate = pl.BlockSpec((nb, CONV_W - 1, D_FF), lambda i: (i // tiles_per_batch, 0, 0))
    return pl.pallas_call(
        functools.partial(_ffn_kernel, parts=parts, l=l, tiles_per_batch=tiles_per_batch),
        grid=(n // tm,),
        in_specs=[row, row, row, row, row, cstate,
                  _resident((D_MODEL, D_MODEL)), _resident((D_MODEL, 2 * D_FF)),
                  _resident((D_FF, D_MODEL)), _resident((1, D_MODEL)), _resident((1, D_MODEL)),
                  _resident((1, D_MODEL)), _resident((CONV_W, D_FF)), _resident((1, D_FF))],
        out_specs=[row, cstate],
        out_shape=[jax.ShapeDtypeStruct((n, D_MODEL), F32),
                   jax.ShapeDtypeStruct((nb_total, CONV_W - 1, D_FF), F32)],
        scratch_shapes=[pltpu.VMEM((nb, l + 8, D_FF), F32)],
        compiler_params=pltpu.CompilerParams(
            dimension_semantics=("arbitrary",), vmem_limit_bytes=V7X_VMEM_LIMIT),
        name="merge_ffn",
    )(x2d, attn, hg, ga, gm, conv0, wout, wup, wdn, gpm, gpf, gpo, cw, cb)


def _layer(x, caches, c0, n0, m0, conv0, wts, lam_init):
    (g_pre_mix, wa, wb, wg, gate_bias, lq1, lk1, lq2, lk2, g_attn_head, g_mlstm_head, wout, g_post_mix,
     g_pre_ffn, wup, conv_w, conv_b, wdn, g_post_ffn) = wts
    b, l, _ = x.shape
    x2d = x.reshape(b * l, D_MODEL)
    (qa, kf, kb, vf, vb, qm, km, vm, om, ga, gm, gz, qn, kn) = _inproj(
        x2d, g_pre_mix, wa, wb, wg, b, l)

    if caches is None:
        per_b = qn.shape[0] // b
        attn = _attn_prompt(qa, kb, vb, qn.reshape(b, per_b, GATE_LANES), kn.reshape(b, per_b, GATE_LANES),
                            lq1, lk1, lq2, lk2, g_attn_head, lam_init)
    else:
        cache_k, cache_v, b_off = caches
        attn = _attn_sample(qa, kb, vb, cache_k, cache_v, b_off,
                            lq1, lk1, lq2, lk2, g_attn_head, lam_init)

    seq3 = lambda a: a.reshape(b, l, a.shape[-1])
    hm, c1, n1, m1 = _mlstm(seq3(qm), seq3(km), seq3(vm), seq3(om), seq3(gz), gate_bias, g_mlstm_head,
                            c0, n0, m0.reshape(b, 1, H_M))

    y, conv1 = _ffn(x2d, attn.reshape(b * l, D_MODEL), hm.reshape(b * l, D_MODEL), ga, gm,
                    conv0, wout, wup, wdn, g_post_mix, g_pre_ffn, g_post_ffn, conv_w, conv_b, b, l)
    return (y.reshape(b, l, D_MODEL), kf.reshape(b, l, H_A, DV_A), vf.reshape(b, l, H_A, DV_A),
            c1, n1, m1.reshape(b, H_M), conv1)


def kernel(x_prompt, x_sample, cache_k, cache_v, state_C, state_n, state_m, state_conv, g_pre_mix, w_in, b_gates, lam_q1, lam_k1, lam_q2, lam_k2, g_attn_head, g_mlstm_head, w_out, g_post_mix, g_pre_ffn, w_up, conv_w, conv_b, w_down, g_post_ffn):
    depth = w_in.shape[0]
    bp = x_prompt.shape[0]
    yp, ys = x_prompt, x_sample
    outs_p, outs_s = [], []
    n_main = 6 * D_MODEL
    for li in range(depth):
        lam_init = 0.8 - 0.6 * math.exp(-0.3 * li)
        w = w_in[li]
        wa = w[:, :n_main].astype(BF16)
        wb = w[:, n_main + 2 * H_M:].astype(BF16)
        wg = jnp.pad(w[:, n_main:n_main + 2 * H_M], ((0, 0), (0, GATE_LANES - 2 * H_M))).astype(BF16)
        gate_bias = jnp.concatenate(
            [b_gates[li], jnp.zeros((GATE_LANES - 2 * H_M,), F32)]).reshape(1, GATE_LANES)
        row = lambda a: a.reshape(1, -1)
        wts = (row(g_pre_mix[li]), wa, wb, wg, gate_bias, row(lam_q1[li]), row(lam_k1[li]),
               row(lam_q2[li]), row(lam_k2[li]), row(g_attn_head[li]), g_mlstm_head[li],
               w_out[li].astype(BF16), row(g_post_mix[li]), row(g_pre_ffn[li]),
               w_up[li].astype(BF16), conv_w[li], row(conv_b[li]), w_down[li].astype(BF16),
               row(g_post_ffn[li]))
        c0 = jnp.zeros((bp, H_M, DH_M, DH_M), F32)
        n0 = jnp.zeros((bp, H_M, DH_M), F32)
        m0 = jnp.zeros((bp, H_M), F32)
        conv0 = jnp.zeros((bp, CONV_W - 1, D_FF), F32)
        yp, *sp = _layer(yp, None, c0, n0, m0, conv0, wts, lam_init)
        bs, past = cache_k.shape[1], cache_k.shape[2]
        caches = (cache_k.reshape(depth * bs, past * H_A, DV_A),
                  cache_v.reshape(depth * bs, past * H_A, DV_A), li * bs)
        ys, *ss = _layer(ys, caches, state_C[li], state_n[li], state_m[li],
                         state_conv[li], wts, lam_init)
        outs_p.append(sp)
        outs_s.append(ss)
    k_p, v_p, c_p, n_p, m_p, conv_p = [jnp.stack([o[i] for o in outs_p]) for i in range(6)]
    k_s, v_s, c_s, n_s, m_s, conv_s = [jnp.stack([o[i] for o in outs_s]) for i in range(6)]
    return (yp, ys, k_p, v_p, c_p, n_p, m_p, conv_p, k_s, v_s, c_s, n_s, m_s, conv_s)
```

```python
import functools
import math

import jax
import jax.numpy as jnp
import numpy as np
from jax import lax
from jax.experimental import pallas as pl
from jax.experimental.pallas import tpu as pltpu

F32 = jnp.float32
BF16 = jnp.bfloat16

D_MODEL = 1024
CHUNK = 64
H_A = 8
DH_A = 64
DV_A = 2 * DH_A
H_M = 4
DH_M = D_MODEL // H_M
D_FF = 2816
CONV_W = 3
EPS = 1e-6
N_SEG_A = 6
GATE_LANES = 128
QK_SCALE = DH_A ** -0.5 * math.log2(math.e)

V7X_VMEM_LIMIT = 56 * 1024 * 1024

PROJ_ROWS = 256
ATTN_TILE = 512
SCORE_BOUND = 80.0
MLSTM_CHUNK = 256
MLSTM_STREAMS = 4
FFN_ROWS = 512
FFN_PARTS = 4

NT_DIMS = (((1,), (1,)), ((), ()))
TN_DIMS = (((0,), (0,)), ((), ()))


def _rms(x, g):
    return x * lax.rsqrt(jnp.mean(x * x, axis=-1, keepdims=True) + EPS) * g


def _resident(shape):
    nd = len(shape)
    return pl.BlockSpec(shape, lambda *_: (0,) * nd, pipeline_mode=pl.Buffered(1))


def _inproj_kernel(x_ref, g_ref, w_ref, qa_ref, kf_ref, kb_ref, vf_ref, vb_ref,
                   qm_ref, km_ref, vm_ref, om_ref, ga_ref, gm_ref, gz_ref, qn_ref, kn_ref, wb_s, wg_s,
                   *, transposed):
    gate0 = N_SEG_A * D_MODEL

    @pl.when(pl.program_id(0) == 0)
    def _align_tail():
        wb_s[...] = w_ref[:, gate0 + 2 * H_M:]
        lane = lax.broadcasted_iota(jnp.int32, wg_s.shape, 1)
        head = w_ref[:, gate0:gate0 + GATE_LANES]
        wg_s[...] = jnp.where(lane < 2 * H_M, head, jnp.zeros_like(head))

    hb = _rms(x_ref[...], g_ref[...]).astype(BF16)

    def seg(j):
        w, jj = (w_ref, j) if j < N_SEG_A else (wb_s, j - N_SEG_A)
        return jnp.dot(hb, w[:, jj * D_MODEL:(jj + 1) * D_MODEL], preferred_element_type=F32)

    def put_heads(ref, val, transpose):
        dst = ref.at[0] if transposed else ref
        for h in range(H_A):
            piece = val[:, h * DV_A:(h + 1) * DV_A]
            dst[h] = (piece.T if transpose else piece).astype(BF16)

    def max_sq_norm(val):
        lane = lax.broadcasted_iota(jnp.int32, (1, GATE_LANES), 1)
        out = jnp.zeros((1, GATE_LANES), F32)
        for h in range(H_A):
            piece = val[:, h * DV_A:(h + 1) * DV_A]
            top = jnp.max(jnp.sum(piece * piece, axis=-1, keepdims=True), axis=0, keepdims=True)
            out = jnp.where(lane == h, top, out)
        return out

    q = seg(0) * QK_SCALE
    put_heads(qa_ref, q, transposed)
    qn_ref[0] = max_sq_norm(q)
    k = seg(1)
    kf_ref[...] = k
    put_heads(kb_ref, k, False)
    kn_ref[0] = max_sq_norm(k)
    v = seg(2)
    vf_ref[...] = v
    put_heads(vb_ref, v, transposed)
    qm_ref[...] = (seg(3) * DH_M ** -0.5).astype(BF16)
    km_ref[...] = seg(4).astype(BF16)
    vm_ref[...] = seg(5).astype(BF16)
    om_ref[...] = jax.nn.sigmoid(seg(6)).astype(BF16)
    ga_ref[...] = jax.nn.sigmoid(seg(7)).astype(BF16)
    gm_ref[...] = jax.nn.sigmoid(seg(8)).astype(BF16)
    gz_ref[...] = jnp.dot(hb, wg_s[...], preferred_element_type=F32)


def _inproj(x2d, g, w, nb, seq):
    n = x2d.shape[0]
    tm = PROJ_ROWS
    transposed = seq >= tm
    if transposed:
        per_b = seq // tm
        head = pl.BlockSpec((1, H_A, tm, DV_A), lambda i: (i // per_b, 0, i % per_b, 0))
        head_t = pl.BlockSpec((1, H_A, DV_A, tm), lambda i: (i // per_b, 0, 0, i % per_b))
        hm16 = jax.ShapeDtypeStruct((nb, H_A, seq, DV_A), BF16)
        hm16_t = jax.ShapeDtypeStruct((nb, H_A, DV_A, seq), BF16)
    else:
        head = head_t = pl.BlockSpec((H_A, tm, DV_A), lambda i: (0, i, 0))
        hm16 = hm16_t = jax.ShapeDtypeStruct((H_A, n, DV_A), BF16)
    row = pl.BlockSpec((tm, D_MODEL), lambda i: (i, 0))
    norm = pl.BlockSpec((1, 1, GATE_LANES), lambda i: (i, 0, 0))
    norm_shape = jax.ShapeDtypeStruct((n // tm, 1, GATE_LANES), F32)
    full32 = jax.ShapeDtypeStruct((n, D_MODEL), F32)
    full16 = jax.ShapeDtypeStruct((n, D_MODEL), BF16)
    return pl.pallas_call(
        functools.partial(_inproj_kernel, transposed=transposed),
        grid=(n // tm,),
        in_specs=[row, _resident((1, D_MODEL)), _resident(w.shape)],
        out_specs=[head_t, row, head, row, head_t, row, row, row, row, row, row,
                   pl.BlockSpec((tm, GATE_LANES), lambda i: (i, 0)), norm, norm],
        out_shape=[hm16_t, full32, hm16, full32, hm16_t, full16, full16, full16,
                   full16, full16, full16, jax.ShapeDtypeStruct((n, GATE_LANES), F32),
                   norm_shape, norm_shape],
        scratch_shapes=[pltpu.VMEM((D_MODEL, w.shape[1] - N_SEG_A * D_MODEL - 2 * H_M), BF16),
                        pltpu.VMEM((D_MODEL, GATE_LANES), BF16)],
        compiler_params=pltpu.CompilerParams(
            dimension_semantics=("arbitrary",), vmem_limit_bytes=V7X_VMEM_LIMIT),
        name="inproj",
    )(x2d, g, w)


def _lam(lq1, lk1, lq2, lk2, lam_init):
    return (jnp.exp(jnp.sum(lq1 * lk1, axis=-1, keepdims=True))
            - jnp.exp(jnp.sum(lq2 * lk2, axis=-1, keepdims=True)) + lam_init)


def _split_maps(q, axis):
    idx = lax.broadcasted_iota(jnp.int32, q.shape, axis)
    zero = jnp.zeros_like(q)
    return jnp.where(idx < DH_A, q, zero), jnp.where(idx >= DH_A, q, zero)


def _attn_prompt_kernel(qi_tab, kj_tab, qt_ref, k_ref, vt_ref, qn_ref, kn_ref, lq1_ref, lk1_ref, lq2_ref,
                        lk2_ref, gcol_ref, o_ref, qz_s, m_s, l_s, acc_s, mode_s, *, lam_init):
    p = pl.program_id(1)
    qi = qi_tab[p]
    kj = kj_tab[p]
    t = ATTN_TILE
    diag = kj == qi
    pairs = [(h, mp) for h in range(H_A) for mp in range(2)]
    n = len(pairs)

    @pl.when(kj == 0)
    def _init():
        for h in range(H_A):
            q1, q2 = _split_maps(qt_ref[h], 0)
            qz_s[0, h] = q1
            qz_s[1, h] = q2
        m_s[...] = jnp.full(m_s.shape, -jnp.inf, F32)
        l_s[...] = jnp.zeros(l_s.shape, F32)
        acc_s[...] = jnp.zeros(acc_s.shape, F32)
        bound_sq = (jnp.max(qn_ref[...], axis=0, keepdims=True)
                    * jnp.max(kn_ref[...], axis=0, keepdims=True))
        mode_s[0] = jnp.where(jnp.max(bound_sq) <= SCORE_BOUND * SCORE_BOUND, 1, 0)

    def chunk_mask():
        kr = lax.broadcasted_iota(jnp.int32, (t, t), 0) // CHUNK
        qc = lax.broadcasted_iota(jnp.int32, (t, t), 1) // CHUNK
        return kr <= qc

    def scores(h, mp, visible):
        s = jnp.dot(k_ref[h], qz_s[mp, h], preferred_element_type=F32)
        return s if visible is None else jnp.where(visible, s, -jnp.inf)

    def bounded_pass(visible):
        s_next = scores(*pairs[0], visible)
        for i, (h, mp) in enumerate(pairs):
            s = s_next
            if i + 1 < n:
                s_next = scores(*pairs[i + 1], visible)
            pr = jnp.exp2(s)
            l_s[mp, h] = l_s[mp, h] + jnp.sum(pr, axis=0, keepdims=True)
            acc_s[mp, h] = acc_s[mp, h] + jnp.dot(
                vt_ref[h], pr.astype(BF16), preferred_element_type=F32)

    def bounded_diag_pass():
        hq = t // 2
        visible = chunk_mask()
        vis_lo, vis_hi = visible[:hq, :hq], visible[:, hq:]

        def halves(h, mp):
            qz = qz_s[mp, h]
            s_lo = jnp.dot(k_ref[h, :hq, :], qz[:, :hq], preferred_element_type=F32)
            s_hi = jnp.dot(k_ref[h], qz[:, hq:], preferred_element_type=F32)
            return jnp.where(vis_lo, s_lo, -jnp.inf), jnp.where(vis_hi, s_hi, -jnp.inf)

        s_next = halves(*pairs[0])
        for i, (h, mp) in enumerate(pairs):
            s_lo, s_hi = s_next
            if i + 1 < n:
                s_next = halves(*pairs[i + 1])
            p_lo, p_hi = jnp.exp2(s_lo), jnp.exp2(s_hi)
            l_s[mp, h] = l_s[mp, h] + jnp.concatenate(
                [jnp.sum(p_lo, axis=0, keepdims=True), jnp.sum(p_hi, axis=0, keepdims=True)], axis=1)
            acc_s[mp, h] = acc_s[mp, h] + jnp.concatenate(
                [jnp.dot(vt_ref[h, :, :hq], p_lo.astype(BF16), preferred_element_type=F32),
                 jnp.dot(vt_ref[h], p_hi.astype(BF16), preferred_element_type=F32)], axis=1)

    def online_pass(visible):
        s_next = scores(*pairs[0], visible)
        for i, (h, mp) in enumerate(pairs):
            s = s_next
            m_old = m_s[mp, h]
            m_new = jnp.maximum(m_old, jnp.max(s, axis=0, keepdims=True))
            if i + 1 < n:
                s_next = scores(*pairs[i + 1], visible)
            alpha = jnp.exp2(m_old - m_new)
            pr = jnp.exp2(s - m_new)
            l_s[mp, h] = alpha * l_s[mp, h] + jnp.sum(pr, axis=0, keepdims=True)
            acc_s[mp, h] = alpha * acc_s[mp, h] + jnp.dot(
                vt_ref[h], pr.astype(BF16), preferred_element_type=F32)
            m_s[mp, h] = m_new

    bounded = mode_s[0] == 1
    off_diag = jnp.logical_not(diag)

    @pl.when(jnp.logical_and(bounded, off_diag))
    def _bounded_full():
        bounded_pass(None)

    @pl.when(jnp.logical_and(bounded, diag))
    def _bounded_diag():
        bounded_diag_pass()

    @pl.when(jnp.logical_and(jnp.logical_not(bounded), off_diag))
    def _online_full():
        online_pass(None)

    @pl.when(jnp.logical_and(jnp.logical_not(bounded), diag))
    def _online_diag():
        online_pass(chunk_mask())

    @pl.when(diag)
    def _finish():
        lam = _lam(lq1_ref[...], lk1_ref[...], lq2_ref[...], lk2_ref[...], lam_init)
        for h in range(H_A):
            ot = acc_s[0, h] / l_s[0, h] - lam * (acc_s[1, h] / l_s[1, h])
            ms = jnp.mean(ot * ot, axis=0, keepdims=True)
            ot = ot * lax.rsqrt(ms + EPS) * gcol_ref[...] * (1.0 - lam_init)
            o_ref[:, h * DV_A:(h + 1) * DV_A] = ot.T.astype(BF16)


def _attn_prompt(qt, k, vt, qn, kn, lq1, lk1, lq2, lk2, gh, lam_init):
    b, _, s, _ = k.shape
    t = ATTN_TILE
    nq = s // t
    pairs = [(i, j) for i in range(nq) for j in range(i + 1)]
    qi_tab = jnp.asarray(np.array([a for a, _ in pairs], np.int32))
    kj_tab = jnp.asarray(np.array([c for _, c in pairs], np.int32))
    qspec = pl.BlockSpec((None, H_A, DV_A, t), lambda bb, p, qi, kj: (bb, 0, 0, qi[p]))
    kspec = pl.BlockSpec((None, H_A, t, DV_A), lambda bb, p, qi, kj: (bb, 0, kj[p], 0))
    vspec = pl.BlockSpec((None, H_A, DV_A, t), lambda bb, p, qi, kj: (bb, 0, 0, kj[p]))
    small = lambda shape: pl.BlockSpec(shape, lambda bb, p, qi, kj: (0, 0))
    nspec = pl.BlockSpec((None,) + qn.shape[1:], lambda bb, p, qi, kj: (bb, 0, 0))
    grid_spec = pltpu.PrefetchScalarGridSpec(
        num_scalar_prefetch=2,
        grid=(b, len(pairs)),
        in_specs=[qspec, kspec, vspec, nspec, nspec, small((1, DH_A)), small((1, DH_A)), small((1, DH_A)),
                  small((1, DH_A)), small((DV_A, 1))],
        out_specs=pl.BlockSpec((None, t, H_A * DV_A), lambda bb, p, qi, kj: (bb, qi[p], 0)),
        scratch_shapes=[pltpu.VMEM((2, H_A, DV_A, t), BF16),
                        pltpu.VMEM((2, H_A, 1, t), F32),
                        pltpu.VMEM((2, H_A, 1, t), F32),
                        pltpu.VMEM((2, H_A, DV_A, t), F32),
                        pltpu.SMEM((1,), jnp.int32)],
    )
    return pl.pallas_call(
        functools.partial(_attn_prompt_kernel, lam_init=lam_init),
        grid_spec=grid_spec,
        out_shape=jax.ShapeDtypeStruct((b, s, H_A * DV_A), BF16),
        compiler_params=pltpu.CompilerParams(
            dimension_semantics=("arbitrary", "arbitrary"), vmem_limit_bytes=V7X_VMEM_LIMIT),
        name="attn_prompt",
    )(qi_tab, kj_tab, qt, k, vt, qn, kn, lq1, lk1, lq2, lk2, gh.reshape(DV_A, 1))


def _attn_sample_kernel(q_ref, kn_ref, vn_ref, ck_ref, cv_ref, lq1_ref, lk1_ref, lq2_ref, lk2_ref,
                        gh_ref, o_ref, *, lam_init):
    lam = _lam(lq1_ref[...], lk1_ref[...], lq2_ref[...], lk2_ref[...], lam_init)
    nq = q_ref.shape[1]
    past = ck_ref.shape[0] // H_A
    zero = jnp.zeros((2 * nq, DV_A), BF16)
    for ha in range(0, H_A, 2):
        hb = ha + 1
        rows_a, rows_b = pl.ds(ha, past, stride=H_A), pl.ds(hb, past, stride=H_A)
        qa = jnp.concatenate(_split_maps(q_ref[ha], 1), axis=0)
        qb = jnp.concatenate(_split_maps(q_ref[hb], 1), axis=0)
        qz = jnp.concatenate([jnp.concatenate([qa, zero], axis=1),
                              jnp.concatenate([zero, qb], axis=1)], axis=0)
        kc = jnp.concatenate([ck_ref[rows_a, :], ck_ref[rows_b, :]], axis=1).astype(BF16)
        kn = jnp.concatenate([kn_ref[ha], kn_ref[hb]], axis=1)
        sc = lax.dot_general(qz, kc, NT_DIMS, preferred_element_type=F32)
        sn = lax.dot_general(qz, kn, NT_DIMS, preferred_element_type=F32)
        m = jnp.maximum(jnp.max(sc, axis=-1, keepdims=True), jnp.max(sn, axis=-1, keepdims=True))
        pc = jnp.exp2(sc - m)
        pn = jnp.exp2(sn - m)
        inv = 1.0 / (jnp.sum(pc, axis=-1, keepdims=True) + jnp.sum(pn, axis=-1, keepdims=True))
        pc = pc * inv
        pn = pn * inv

        def diff_maps(p):
            return jnp.concatenate([p[0:nq] - lam * p[nq:2 * nq],
                                    p[2 * nq:3 * nq] - lam * p[3 * nq:4 * nq]], axis=0)

        vc = jnp.concatenate([cv_ref[rows_a, :], cv_ref[rows_b, :]], axis=1).astype(BF16)
        vn = jnp.concatenate([vn_ref[ha], vn_ref[hb]], axis=1)
        o2 = (jnp.dot(diff_maps(pc).astype(BF16), vc, preferred_element_type=F32)
              + jnp.dot(diff_maps(pn).astype(BF16), vn, preferred_element_type=F32))
        for h, o in ((ha, o2[0:nq, 0:DV_A]), (hb, o2[nq:2 * nq, DV_A:2 * DV_A])):
            o_ref[:, h * DV_A:(h + 1) * DV_A] = (_rms(o, gh_ref[...]) * (1.0 - lam_init)).astype(BF16)


def _attn_sample(q, kn, vn, cache_k, cache_v, b_off, b, l, lq1, lk1, lq2, lk2, gh, lam_init):
    hspec = pl.BlockSpec((H_A, l, DV_A), lambda i: (0, i, 0))
    cspec = pl.BlockSpec((None,) + cache_k.shape[1:], lambda i: (b_off + i, 0, 0))
    small = lambda shape: pl.BlockSpec(shape, lambda i: (0, 0))
    return pl.pallas_call(
        functools.partial(_attn_sample_kernel, lam_init=lam_init),
        grid=(b,),
        in_specs=[hspec, hspec, hspec, cspec, cspec, small((1, DH_A)), small((1, DH_A)),
                  small((1, DH_A)), small((1, DH_A)), small((1, DV_A))],
        out_specs=pl.BlockSpec((None, l, H_A * DV_A), lambda i: (i, 0, 0)),
        out_shape=jax.ShapeDtypeStruct((b, l, H_A * DV_A), BF16),
        compiler_params=pltpu.CompilerParams(
            dimension_semantics=("arbitrary",), vmem_limit_bytes=V7X_VMEM_LIMIT),
        name="attn_sample",
    )(q, kn, vn, cache_k, cache_v, lq1, lk1, lq2, lk2, gh)


def _split3(x):
    hi = x.astype(BF16)
    r1 = x - hi.astype(F32)
    mid = r1.astype(BF16)
    lo = (r1 - mid.astype(F32)).astype(BF16)
    return hi, mid, lo


def _mlstm_kernel(q_ref, k_ref, v_ref, og_ref, gz_ref, bias_ref, gh_ref, c0_ref, n0_ref, m0_ref,
                  h_ref, c_ref, n_ref, m_ref, c_s, n_s, m_s, *, t, bt):
    c = pl.program_id(1)

    @pl.when(c == 0)
    def _load_state():
        c_s[...] = c0_ref[...]
        n_s[...] = n0_ref[...]
        m_s[...] = m0_ref[...]

    lane = lax.broadcasted_iota(jnp.int32, (t, GATE_LANES), 1)
    row = lax.broadcasted_iota(jnp.int32, (t, t), 0)
    col = lax.broadcasted_iota(jnp.int32, (t, t), 1)
    causal = col <= row
    tril = jnp.where(causal, 1.0, 0.0).astype(BF16)
    sel = jnp.where(lax.broadcasted_iota(jnp.int32, (8, GATE_LANES), 0)
                    == lax.broadcasted_iota(jnp.int32, (8, GATE_LANES), 1), 1.0, 0.0).astype(BF16)

    gz = [gz_ref[b] + bias_ref[...] for b in range(bt)]
    lf = [jnp.minimum(g, 0.0) - jnp.log1p(jnp.exp(-jnp.abs(g))) for g in gz]
    lf = [jnp.where((lane >= H_M) & (lane < 2 * H_M), x, 0.0) for x in lf]
    bcum = [sum(jnp.dot(tril, piece, preferred_element_type=F32) for piece in _split3(x)) for x in lf]
    gc = [jnp.where(lane < H_M, gz[b], bcum[b]) for b in range(bt)]
    gr = [sum(lax.dot_general(sel, piece, NT_DIMS, preferred_element_type=F32) for piece in _split3(x))
          for x in gc]

    chains = [(b, h) for b in range(bt) for h in range(H_M)]
    ids = range(len(chains))
    sls = [slice(h * DH_M, (h + 1) * DH_M) for _, h in chains]
    ig_c = [gc[b][:, h:h + 1] for b, h in chains]
    b_c = [gc[b][:, H_M + h:H_M + h + 1] for b, h in chains]
    ig_r = [gr[b][h:h + 1, :] for b, h in chains]
    b_r = [gr[b][H_M + h:H_M + h + 1, :] for b, h in chains]
    m_prev = [m_s[b, :, h:h + 1] for b, h in chains]
    c_prev = [c_s[b, h] for b, h in chains]
    n_prev = [n_s[b, h:h + 1, :] for b, h in chains]
    q = [q_ref[b, :, sls[i]] for i, (b, _) in enumerate(chains)]
    k = [k_ref[b, :, sls[i]] for i, (b, _) in enumerate(chains)]
    v = [v_ref[b, :, sls[i]] for i, (b, _) in enumerate(chains)]

    qk = [lax.dot_general(q[i], k[i], NT_DIMS, preferred_element_type=F32) for i in ids]
    qc = [lax.dot_general(q[i], c_prev[i].astype(BF16), NT_DIMS, preferred_element_type=F32)
          for i in ids]
    d = [jnp.where(causal, b_c[i] - b_r[i] + ig_r[i], -jnp.inf) for i in ids]
    inter = [b_c[i] + m_prev[i] for i in ids]
    m_t = [jnp.maximum(inter[i], jnp.max(d[i], axis=-1, keepdims=True)) for i in ids]
    w_inter = [jnp.exp(inter[i] - m_t[i]) for i in ids]
    s = [qk[i] * jnp.exp(d[i] - m_t[i]) for i in ids]
    num = [jnp.dot(s[i].astype(BF16), v[i], preferred_element_type=F32) + w_inter[i] * qc[i]
           for i in ids]
    den = [jnp.sum(s[i], axis=-1, keepdims=True)
           + w_inter[i] * jnp.sum(q[i].astype(F32) * n_prev[i], axis=-1, keepdims=True) for i in ids]
    hh = [num[i] / jnp.maximum(jnp.abs(den[i]), jnp.exp(-m_t[i])) for i in ids]
    for i, (b, h) in enumerate(chains):
        h_ref[b, :, sls[i]] = (_rms(hh[i], gh_ref[h:h + 1, :])
                               * og_ref[b, :, sls[i]].astype(F32)).astype(BF16)

    g_last = [b_c[i][t - 1:t, :] for i in ids]
    logw = [g_last[i] - b_c[i] + ig_c[i] for i in ids]
    m_new = [jnp.maximum(g_last[i] + m_prev[i], jnp.max(logw[i], axis=0, keepdims=True)) for i in ids]
    ws = [jnp.exp(logw[i] - m_new[i]) for i in ids]
    wc = [jnp.exp(g_last[i] + m_prev[i] - m_new[i]) for i in ids]
    vw = [(v[i].astype(F32) * ws[i]).astype(BF16) for i in ids]
    for i, (b, h) in enumerate(chains):
        c_s[b, h] = wc[i] * c_prev[i] + lax.dot_general(vw[i], k[i], TN_DIMS,
                                                        preferred_element_type=F32)
        n_s[b, h:h + 1, :] = wc[i] * n_prev[i] + jnp.sum(k[i].astype(F32) * ws[i], axis=0,
                                                          keepdims=True)
        m_s[b, :, h:h + 1] = m_new[i]

    @pl.when(c == pl.num_programs(1) - 1)
    def _store_state():
        c_ref[...] = c_s[...]
        n_ref[...] = n_s[...]
        m_ref[...] = m_s[...]


def _mlstm(q, k, v, og, gz, bias, gh, c0, n0, m0):
    b, l, _ = q.shape
    t = min(MLSTM_CHUNK, l)
    nc = l // t
    bt = min(b, MLSTM_STREAMS)
    seq = lambda width: pl.BlockSpec((bt, t, width), lambda i, j: (i, j, 0))
    const = lambda shape: pl.BlockSpec(shape, lambda i, j: (0,) * len(shape))
    cspec = pl.BlockSpec((bt, H_M, DH_M, DH_M), lambda i, j: (i, 0, 0, 0))
    nspec = pl.BlockSpec((bt, H_M, DH_M), lambda i, j: (i, 0, 0))
    mspec = pl.BlockSpec((bt, 1, H_M), lambda i, j: (i, 0, 0))
    return pl.pallas_call(
        functools.partial(_mlstm_kernel, t=t, bt=bt),
        grid=(b // bt, nc),
        in_specs=[seq(D_MODEL), seq(D_MODEL), seq(D_MODEL), seq(D_MODEL), seq(GATE_LANES),
                  const((1, GATE_LANES)), const((H_M, DH_M)), cspec, nspec, mspec],
        out_specs=[seq(D_MODEL), cspec, nspec, mspec],
        out_shape=[jax.ShapeDtypeStruct((b, l, D_MODEL), BF16),
                   jax.ShapeDtypeStruct((b, H_M, DH_M, DH_M), F32),
                   jax.ShapeDtypeStruct((b, H_M, DH_M), F32),
                   jax.ShapeDtypeStruct((b, 1, H_M), F32)],
        scratch_shapes=[pltpu.VMEM((bt, H_M, DH_M, DH_M), F32),
                        pltpu.VMEM((bt, H_M, DH_M), F32),
                        pltpu.VMEM((bt, 1, H_M), F32)],
        compiler_params=pltpu.CompilerParams(
            dimension_semantics=("arbitrary", "arbitrary"), vmem_limit_bytes=V7X_VMEM_LIMIT),
        name="mlstm",
    )(q, k, v, og, gz, bias, gh, c0, n0, m0)


def _gated_gelu(x, u):
    c = -2.0 * math.sqrt(2.0 / math.pi)
    return (x * u) / (1.0 + jnp.exp(x * (c + (c * 0.044715) * (x * x))))


def _ffn_kernel(x_ref, attn_ref, hg_ref, ga_ref, gm_ref, conv0_ref, wout_ref, wup_ref, wdn_ref,
                gpm_ref, gpf_ref, gpo_ref, cw_ref, cb_ref, y_ref, cs_ref, g_s,
                *, parts, l, tiles_per_batch):
    i = pl.program_id(0)

    @pl.when(i % tiles_per_batch == 0)
    def _from_state():
        g_s[:, 6:8, :] = conv0_ref[...]

    @pl.when(i % tiles_per_batch != 0)
    def _from_prev_tile():
        g_s[:, 6:8, :] = g_s[:, l + 6:l + 8, :]

    def pre(rows):
        merged = ga_ref[rows, :] * attn_ref[rows, :] + gm_ref[rows, :] * hg_ref[rows, :]
        x1 = x_ref[rows, :] + _rms(jnp.dot(merged, wout_ref[...], preferred_element_type=F32),
                                   gpm_ref[...])
        return x1, _rms(x1, gpf_ref[...]).astype(BF16)

    def up(h2):
        return jnp.dot(h2, wup_ref[...], preferred_element_type=F32)

    def act(ug, bsl, r0, lh):
        nbh = bsl.stop - bsl.start
        g3 = ug[:, D_FF:].reshape(nbh, lh, D_FF)
        g_s[bsl, 8 + r0:8 + r0 + lh, :] = g3
        gconv = (cb_ref[...] + cw_ref[0:1, :] * g_s[bsl, 6 + r0:6 + r0 + lh, :]
                 + cw_ref[1:2, :] * g_s[bsl, 7 + r0:7 + r0 + lh, :] + cw_ref[2:3, :] * g3)
        return _gated_gelu(gconv.reshape(nbh * lh, D_FF), ug[:, :D_FF]).astype(BF16)

    def post(x1, a, rows):
        ff = jnp.dot(a, wdn_ref[...], preferred_element_type=F32)
        y_ref[rows, :] = x1 + _rms(ff, gpo_ref[...])

    n = len(parts)
    x1, h2, ug = [None] * n, [None] * n, [None] * n
    x1[0], h2[0] = pre(parts[0][0])
    for i in range(n + 1):
        if i + 1 < n:
            x1[i + 1], h2[i + 1] = pre(parts[i + 1][0])
        if i < n:
            ug[i] = up(h2[i])
        if i >= 1:
            rows, *conv = parts[i - 1]
            post(x1[i - 1], act(ug[i - 1], *conv), rows)
    cs_ref[...] = g_s[:, l + 6:l + 8, :]


def _ffn(x2d, attn, hg, ga, gm, conv0, wout, wup, wdn, gpm, gpf, gpo, cw, cb, nb_total, seq):
    n = x2d.shape[0]
    np_ = FFN_PARTS
    if seq >= FFN_ROWS:
        tm, nb, l = FFN_ROWS, 1, FFN_ROWS
        tiles_per_batch = seq // tm
        pl_ = l // np_
        parts = tuple((slice(j * pl_, (j + 1) * pl_), slice(0, 1), j * pl_, pl_) for j in range(np_))
    else:
        tm, nb, l = n, nb_total, seq
        tiles_per_batch = 1
        pb = nb // np_
        parts = tuple((slice(j * pb * l, (j + 1) * pb * l), slice(j * pb, (j + 1) * pb), 0, l)
                      for j in range(np_))
    row = pl.BlockSpec((tm, D_MODEL), lambda i: (i, 0))
    cstate = pl.BlockSpec((nb, CONV_W - 1, D_FF), lambda i: (i // tiles_per_batch, 0, 0))
    return pl.pallas_call(
        functools.partial(_ffn_kernel, parts=parts, l=l, tiles_per_batch=tiles_per_batch),
        grid=(n // tm,),
        in_specs=[row, row, row, row, row, cstate,
                  _resident((D_MODEL, D_MODEL)), _resident((D_MODEL, 2 * D_FF)),
                  _resident((D_FF, D_MODEL)), _resident((1, D_MODEL)), _resident((1, D_MODEL)),
                  _resident((1, D_MODEL)), _resident((CONV_W, D_FF)), _resident((1, D_FF))],
        out_specs=[row, cstate],
        out_shape=[jax.ShapeDtypeStruct((n, D_MODEL), F32),
                   jax.ShapeDtypeStruct((nb_total, CONV_W - 1, D_FF), F32)],
        scratch_shapes=[pltpu.VMEM((nb, l + 8, D_FF), F32)],
        compiler_params=pltpu.CompilerParams(
            dimension_semantics=("arbitrary",), vmem_limit_bytes=V7X_VMEM_LIMIT),
        name="merge_ffn",
    )(x2d, attn, hg, ga, gm, conv0, wout, wup, wdn, gpm, gpf, gpo, cw, cb)


def _layer(x, caches, c0, n0, m0, conv0, wts, lam_init):
    (g_pre_mix, w_in, gate_bias, lq1, lk1, lq2, lk2, g_attn_head, g_mlstm_head, wout, g_post_mix,
     g_pre_ffn, wup, conv_w, conv_b, wdn, g_post_ffn) = wts
    b, l, _ = x.shape
    x2d = x.reshape(b * l, D_MODEL)
    (qa, kf, kb, vf, vb, qm, km, vm, om, ga, gm, gz, qn, kn) = _inproj(x2d, g_pre_mix, w_in, b, l)

    if caches is None:
        per_b = qn.shape[0] // b
        attn = _attn_prompt(qa, kb, vb, qn.reshape(b, per_b, GATE_LANES), kn.reshape(b, per_b, GATE_LANES),
                            lq1, lk1, lq2, lk2, g_attn_head, lam_init)
    else:
        cache_k, cache_v, b_off = caches
        attn = _attn_sample(qa, kb, vb, cache_k, cache_v, b_off, b, l,
                            lq1, lk1, lq2, lk2, g_attn_head, lam_init)

    seq3 = lambda a: a.reshape(b, l, a.shape[-1])
    hm, c1, n1, m1 = _mlstm(seq3(qm), seq3(km), seq3(vm), seq3(om), seq3(gz), gate_bias, g_mlstm_head,
                            c0, n0, m0.reshape(b, 1, H_M))

    y, conv1 = _ffn(x2d, attn.reshape(b * l, D_MODEL), hm.reshape(b * l, D_MODEL), ga, gm,
                    conv0, wout, wup, wdn, g_post_mix, g_pre_ffn, g_post_ffn, conv_w, conv_b, b, l)
    return (y.reshape(b, l, D_MODEL), kf.reshape(b, l, H_A, DV_A), vf.reshape(b, l, H_A, DV_A),
            c1, n1, m1.reshape(b, H_M), conv1)


def kernel(x_prompt, x_sample, cache_k, cache_v, state_C, state_n, state_m, state_conv, g_pre_mix, w_in, b_gates, lam_q1, lam_k1, lam_q2, lam_k2, g_attn_head, g_mlstm_head, w_out, g_post_mix, g_pre_ffn, w_up, conv_w, conv_b, w_down, g_post_ffn):
    depth = w_in.shape[0]
    bp = x_prompt.shape[0]
    yp, ys = x_prompt, x_sample
    outs_p, outs_s = [], []
    for li in range(depth):
        lam_init = 0.8 - 0.6 * math.exp(-0.3 * li)
        gate_bias = jnp.concatenate(
            [b_gates[li], jnp.zeros((GATE_LANES - 2 * H_M,), F32)]).reshape(1, GATE_LANES)
        row = lambda a: a.reshape(1, -1)
        wts = (row(g_pre_mix[li]), w_in[li].astype(BF16), gate_bias, row(lam_q1[li]), row(lam_k1[li]),
               row(lam_q2[li]), row(lam_k2[li]), row(g_attn_head[li]), g_mlstm_head[li],
               w_out[li].astype(BF16), row(g_post_mix[li]), row(g_pre_ffn[li]),
               w_up[li].astype(BF16), conv_w[li], row(conv_b[li]), w_down[li].astype(BF16),
               row(g_post_ffn[li]))
        c0 = jnp.zeros((bp, H_M, DH_M, DH_M), F32)
        n0 = jnp.zeros((bp, H_M, DH_M), F32)
        m0 = jnp.zeros((bp, H_M), F32)
        conv0 = jnp.zeros((bp, CONV_W - 1, D_FF), F32)
        yp, *sp = _layer(yp, None, c0, n0, m0, conv0, wts, lam_init)
        bs, past = cache_k.shape[1], cache_k.shape[2]
        caches = (cache_k.reshape(depth * bs, past * H_A, DV_A),
                  cache_v.reshape(depth * bs, past * H_A, DV_A), li * bs)
        ys, *ss = _layer(ys, caches, state_C[li], state_n[li], state_m[li],
                         state_conv[li], wts, lam_init)
        outs_p.append(sp)
        outs_s.append(ss)
    k_p, v_p, c_p, n_p, m_p, conv_p = [jnp.stack([o[i] for o in outs_p]) for i in range(6)]
    k_s, v_s, c_s, n_s, m_s, conv_s = [jnp.stack([o[i] for o in outs_s]) for i in range(6)]
    return (yp, ys, k_p, v_p, c_p, n_p, m_p, conv_p, k_s, v_s, c_s, n_s, m_s, conv_s)
```

```python
import functools
import math

import jax
import jax.numpy as jnp
import numpy as np
from jax import lax
from jax.experimental import pallas as pl
from jax.experimental.pallas import tpu as pltpu

F32 = jnp.float32
BF16 = jnp.bfloat16

D_MODEL = 1024
CHUNK = 64
H_A = 8
DH_A = 64
DV_A = 2 * DH_A
H_M = 4
DH_M = D_MODEL // H_M
D_FF = 2816
CONV_W = 3
EPS = 1e-6
N_SEG_A = 6
GATE_LANES = 128
QK_SCALE = DH_A ** -0.5 * math.log2(math.e)

V7X_VMEM_LIMIT = 56 * 1024 * 1024

PROJ_ROWS = 256
ATTN_TILE = 512
SCORE_BOUND = 80.0
MLSTM_CHUNK = 256
MLSTM_STREAMS = 4
FFN_ROWS = 512
FFN_PARTS = 2
CONV_PAD = 8
CONV_CARRY = CONV_PAD - (CONV_W - 1)

NT_DIMS = (((1,), (1,)), ((), ()))
TN_DIMS = (((0,), (0,)), ((), ()))


def _rms(x, g):
    return x * lax.rsqrt(jnp.mean(x * x, axis=-1, keepdims=True) + EPS) * g


def _resident(shape):
    nd = len(shape)
    return pl.BlockSpec(shape, lambda *_: (0,) * nd, pipeline_mode=pl.Buffered(1))


def _inproj_kernel(x_ref, g_ref, w_ref, qa_ref, kf_ref, kb_ref, vf_ref, vb_ref,
                   qm_ref, km_ref, vm_ref, om_ref, ga_ref, gm_ref, gz_ref, qn_ref, kn_ref, wb_s, wg_s,
                   *, transposed):
    gate0 = N_SEG_A * D_MODEL

    @pl.when(pl.program_id(0) == 0)
    def _align_tail():
        wb_s[...] = w_ref[:, gate0 + 2 * H_M:]
        lane = lax.broadcasted_iota(jnp.int32, wg_s.shape, 1)
        head = w_ref[:, gate0:gate0 + GATE_LANES]
        wg_s[...] = jnp.where(lane < 2 * H_M, head, jnp.zeros_like(head))

    hb = _rms(x_ref[...], g_ref[...]).astype(BF16)

    def seg(j):
        w, jj = (w_ref, j) if j < N_SEG_A else (wb_s, j - N_SEG_A)
        return jnp.dot(hb, w[:, jj * D_MODEL:(jj + 1) * D_MODEL], preferred_element_type=F32)

    def put_heads(ref, val, transpose):
        dst = ref.at[0] if transposed else ref
        for h in range(H_A):
            piece = val[:, h * DV_A:(h + 1) * DV_A]
            dst[h] = (piece.T if transpose else piece).astype(BF16)

    def max_sq_norm(val):
        lane = lax.broadcasted_iota(jnp.int32, (1, GATE_LANES), 1)
        out = jnp.zeros((1, GATE_LANES), F32)
        for h in range(H_A):
            piece = val[:, h * DV_A:(h + 1) * DV_A]
            top = jnp.max(jnp.sum(piece * piece, axis=-1, keepdims=True), axis=0, keepdims=True)
            out = jnp.where(lane == h, top, out)
        return out

    q = seg(0) * QK_SCALE
    put_heads(qa_ref, q, transposed)
    qn_ref[0] = max_sq_norm(q)
    k = seg(1)
    kf_ref[...] = k
    put_heads(kb_ref, k, False)
    kn_ref[0] = max_sq_norm(k)
    v = seg(2)
    vf_ref[...] = v
    put_heads(vb_ref, v, transposed)
    qm_ref[...] = (seg(3) * DH_M ** -0.5).astype(BF16)
    km_ref[...] = seg(4).astype(BF16)
    vm_ref[...] = seg(5).astype(BF16)
    om_ref[...] = jax.nn.sigmoid(seg(6)).astype(BF16)
    ga_ref[...] = jax.nn.sigmoid(seg(7)).astype(BF16)
    gm_ref[...] = jax.nn.sigmoid(seg(8)).astype(BF16)
    gz_ref[...] = jnp.dot(hb, wg_s[...], preferred_element_type=F32)


def _inproj(x2d, g, w, nb, seq):
    n = x2d.shape[0]
    tm = PROJ_ROWS
    transposed = seq >= tm
    if transposed:
        per_b = seq // tm
        head = pl.BlockSpec((1, H_A, tm, DV_A), lambda i: (i // per_b, 0, i % per_b, 0))
        head_t = pl.BlockSpec((1, H_A, DV_A, tm), lambda i: (i // per_b, 0, 0, i % per_b))
        hm16 = jax.ShapeDtypeStruct((nb, H_A, seq, DV_A), BF16)
        hm16_t = jax.ShapeDtypeStruct((nb, H_A, DV_A, seq), BF16)
    else:
        head = head_t = pl.BlockSpec((H_A, tm, DV_A), lambda i: (0, i, 0))
        hm16 = hm16_t = jax.ShapeDtypeStruct((H_A, n, DV_A), BF16)
    row = pl.BlockSpec((tm, D_MODEL), lambda i: (i, 0))
    norm = pl.BlockSpec((1, 1, GATE_LANES), lambda i: (i, 0, 0))
    norm_shape = jax.ShapeDtypeStruct((n // tm, 1, GATE_LANES), F32)
    full32 = jax.ShapeDtypeStruct((n, D_MODEL), F32)
    full16 = jax.ShapeDtypeStruct((n, D_MODEL), BF16)
    return pl.pallas_call(
        functools.partial(_inproj_kernel, transposed=transposed),
        grid=(n // tm,),
        in_specs=[row, _resident((1, D_MODEL)), _resident(w.shape)],
        out_specs=[head_t, row, head, row, head_t, row, row, row, row, row, row,
                   pl.BlockSpec((tm, GATE_LANES), lambda i: (i, 0)), norm, norm],
        out_shape=[hm16_t, full32, hm16, full32, hm16_t, full16, full16, full16,
                   full16, full16, full16, jax.ShapeDtypeStruct((n, GATE_LANES), F32),
                   norm_shape, norm_shape],
        scratch_shapes=[pltpu.VMEM((D_MODEL, w.shape[1] - N_SEG_A * D_MODEL - 2 * H_M), BF16),
                        pltpu.VMEM((D_MODEL, GATE_LANES), BF16)],
        compiler_params=pltpu.CompilerParams(
            dimension_semantics=("arbitrary",), vmem_limit_bytes=V7X_VMEM_LIMIT),
        name="inproj",
    )(x2d, g, w)


def _lam(lq1, lk1, lq2, lk2, lam_init):
    return (jnp.exp(jnp.sum(lq1 * lk1, axis=-1, keepdims=True))
            - jnp.exp(jnp.sum(lq2 * lk2, axis=-1, keepdims=True)) + lam_init)


def _split_maps(q, axis):
    idx = lax.broadcasted_iota(jnp.int32, q.shape, axis)
    zero = jnp.zeros_like(q)
    return jnp.where(idx < DH_A, q, zero), jnp.where(idx >= DH_A, q, zero)


def _attn_prompt_kernel(qi_tab, kj_tab, qt_ref, k_ref, vt_ref, qn_ref, kn_ref, lq1_ref, lk1_ref, lq2_ref,
                        lk2_ref, gcol_ref, o_ref, qz_s, m_s, l_s, acc_s, mode_s, *, lam_init):
    p = pl.program_id(1)
    qi = qi_tab[p]
    kj = kj_tab[p]
    t = ATTN_TILE
    diag = kj == qi
    pairs = [(h, mp) for h in range(H_A) for mp in range(2)]
    n = len(pairs)

    @pl.when(kj == 0)
    def _init():
        for h in range(H_A):
            q1, q2 = _split_maps(qt_ref[h], 0)
            qz_s[0, h] = q1
            qz_s[1, h] = q2
        m_s[...] = jnp.full(m_s.shape, -jnp.inf, F32)
        l_s[...] = jnp.zeros(l_s.shape, F32)
        acc_s[...] = jnp.zeros(acc_s.shape, F32)
        bound_sq = (jnp.max(qn_ref[...], axis=0, keepdims=True)
                    * jnp.max(kn_ref[...], axis=0, keepdims=True))
        mode_s[0] = jnp.where(jnp.max(bound_sq) <= SCORE_BOUND * SCORE_BOUND, 1, 0)

    def chunk_mask():
        kr = lax.broadcasted_iota(jnp.int32, (t, t), 0) // CHUNK
        qc = lax.broadcasted_iota(jnp.int32, (t, t), 1) // CHUNK
        return kr <= qc

    def scores(h, mp, visible):
        s = jnp.dot(k_ref[h], qz_s[mp, h], preferred_element_type=F32)
        return s if visible is None else jnp.where(visible, s, -jnp.inf)

    def bounded_pass(visible):
        s_next = scores(*pairs[0], visible)
        for i, (h, mp) in enumerate(pairs):
            s = s_next
            if i + 1 < n:
                s_next = scores(*pairs[i + 1], visible)
            pr = jnp.exp2(s)
            l_s[mp, h] = l_s[mp, h] + jnp.sum(pr, axis=0, keepdims=True)
            acc_s[mp, h] = acc_s[mp, h] + jnp.dot(
                vt_ref[h], pr.astype(BF16), preferred_element_type=F32)

    def bounded_diag_pass():
        hq = t // 2
        visible = chunk_mask()
        vis_lo, vis_hi = visible[:hq, :hq], visible[:, hq:]

        def halves(h, mp):
            qz = qz_s[mp, h]
            s_lo = jnp.dot(k_ref[h, :hq, :], qz[:, :hq], preferred_element_type=F32)
            s_hi = jnp.dot(k_ref[h], qz[:, hq:], preferred_element_type=F32)
            return jnp.where(vis_lo, s_lo, -jnp.inf), jnp.where(vis_hi, s_hi, -jnp.inf)

        s_next = halves(*pairs[0])
        for i, (h, mp) in enumerate(pairs):
            s_lo, s_hi = s_next
            if i + 1 < n:
                s_next = halves(*pairs[i + 1])
            p_lo, p_hi = jnp.exp2(s_lo), jnp.exp2(s_hi)
            l_s[mp, h] = l_s[mp, h] + jnp.concatenate(
                [jnp.sum(p_lo, axis=0, keepdims=True), jnp.sum(p_hi, axis=0, keepdims=True)], axis=1)
            acc_s[mp, h] = acc_s[mp, h] + jnp.concatenate(
                [jnp.dot(vt_ref[h, :, :hq], p_lo.astype(BF16), preferred_element_type=F32),
                 jnp.dot(vt_ref[h], p_hi.astype(BF16), preferred_element_type=F32)], axis=1)

    def online_pass(visible):
        s_next = scores(*pairs[0], visible)
        for i, (h, mp) in enumerate(pairs):
            s = s_next
            m_old = m_s[mp, h]
            m_new = jnp.maximum(m_old, jnp.max(s, axis=0, keepdims=True))
            if i + 1 < n:
                s_next = scores(*pairs[i + 1], visible)
            alpha = jnp.exp2(m_old - m_new)
            pr = jnp.exp2(s - m_new)
            l_s[mp, h] = alpha * l_s[mp, h] + jnp.sum(pr, axis=0, keepdims=True)
            acc_s[mp, h] = alpha * acc_s[mp, h] + jnp.dot(
                vt_ref[h], pr.astype(BF16), preferred_element_type=F32)
            m_s[mp, h] = m_new

    bounded = mode_s[0] == 1
    off_diag = jnp.logical_not(diag)

    @pl.when(jnp.logical_and(bounded, off_diag))
    def _bounded_full():
        bounded_pass(None)

    @pl.when(jnp.logical_and(bounded, diag))
    def _bounded_diag():
        bounded_diag_pass()

    @pl.when(jnp.logical_and(jnp.logical_not(bounded), off_diag))
    def _online_full():
        online_pass(None)

    @pl.when(jnp.logical_and(jnp.logical_not(bounded), diag))
    def _online_diag():
        online_pass(chunk_mask())

    @pl.when(diag)
    def _finish():
        lam = _lam(lq1_ref[...], lk1_ref[...], lq2_ref[...], lk2_ref[...], lam_init)
        for h in range(H_A):
            ot = acc_s[0, h] / l_s[0, h] - lam * (acc_s[1, h] / l_s[1, h])
            ms = jnp.mean(ot * ot, axis=0, keepdims=True)
            ot = ot * lax.rsqrt(ms + EPS) * gcol_ref[...] * (1.0 - lam_init)
            o_ref[:, h * DV_A:(h + 1) * DV_A] = ot.T.astype(BF16)


def _attn_prompt(qt, k, vt, qn, kn, lq1, lk1, lq2, lk2, gh, lam_init):
    b, _, s, _ = k.shape
    t = ATTN_TILE
    nq = s // t
    pairs = [(i, j) for i in range(nq) for j in range(i + 1)]
    qi_tab = jnp.asarray(np.array([a for a, _ in pairs], np.int32))
    kj_tab = jnp.asarray(np.array([c for _, c in pairs], np.int32))
    qspec = pl.BlockSpec((None, H_A, DV_A, t), lambda bb, p, qi, kj: (bb, 0, 0, qi[p]))
    kspec = pl.BlockSpec((None, H_A, t, DV_A), lambda bb, p, qi, kj: (bb, 0, kj[p], 0))
    vspec = pl.BlockSpec((None, H_A, DV_A, t), lambda bb, p, qi, kj: (bb, 0, 0, kj[p]))
    small = lambda shape: pl.BlockSpec(shape, lambda bb, p, qi, kj: (0, 0))
    nspec = pl.BlockSpec((None,) + qn.shape[1:], lambda bb, p, qi, kj: (bb, 0, 0))
    grid_spec = pltpu.PrefetchScalarGridSpec(
        num_scalar_prefetch=2,
        grid=(b, len(pairs)),
        in_specs=[qspec, kspec, vspec, nspec, nspec, small((1, DH_A)), small((1, DH_A)), small((1, DH_A)),
                  small((1, DH_A)), small((DV_A, 1))],
        out_specs=pl.BlockSpec((None, t, H_A * DV_A), lambda bb, p, qi, kj: (bb, qi[p], 0)),
        scratch_shapes=[pltpu.VMEM((2, H_A, DV_A, t), BF16),
                        pltpu.VMEM((2, H_A, 1, t), F32),
                        pltpu.VMEM((2, H_A, 1, t), F32),
                        pltpu.VMEM((2, H_A, DV_A, t), F32),
                        pltpu.SMEM((1,), jnp.int32)],
    )
    return pl.pallas_call(
        functools.partial(_attn_prompt_kernel, lam_init=lam_init),
        grid_spec=grid_spec,
        out_shape=jax.ShapeDtypeStruct((b, s, H_A * DV_A), BF16),
        compiler_params=pltpu.CompilerParams(
            dimension_semantics=("arbitrary", "arbitrary"), vmem_limit_bytes=V7X_VMEM_LIMIT),
        name="attn_prompt",
    )(qi_tab, kj_tab, qt, k, vt, qn, kn, lq1, lk1, lq2, lk2, gh.reshape(DV_A, 1))


def _attn_sample_kernel(q_ref, kn_ref, vn_ref, ck_ref, cv_ref, lq1_ref, lk1_ref, lq2_ref, lk2_ref,
                        gh_ref, o_ref, *, lam_init):
    lam = _lam(lq1_ref[...], lk1_ref[...], lq2_ref[...], lk2_ref[...], lam_init)
    nq = q_ref.shape[1]
    past = ck_ref.shape[0] // H_A
    zero = jnp.zeros((2 * nq, DV_A), BF16)
    for ha in range(0, H_A, 2):
        hb = ha + 1
        rows_a, rows_b = pl.ds(ha, past, stride=H_A), pl.ds(hb, past, stride=H_A)
        qa = jnp.concatenate(_split_maps(q_ref[ha], 1), axis=0)
        qb = jnp.concatenate(_split_maps(q_ref[hb], 1), axis=0)
        qz = jnp.concatenate([jnp.concatenate([qa, zero], axis=1),
                              jnp.concatenate([zero, qb], axis=1)], axis=0)
        kc = jnp.concatenate([ck_ref[rows_a, :], ck_ref[rows_b, :]], axis=1).astype(BF16)
        kn = jnp.concatenate([kn_ref[ha], kn_ref[hb]], axis=1)
        sc = lax.dot_general(qz, kc, NT_DIMS, preferred_element_type=F32)
        sn = lax.dot_general(qz, kn, NT_DIMS, preferred_element_type=F32)
        m = jnp.maximum(jnp.max(sc, axis=-1, keepdims=True), jnp.max(sn, axis=-1, keepdims=True))
        pc = jnp.exp2(sc - m)
        pn = jnp.exp2(sn - m)
        inv = 1.0 / (jnp.sum(pc, axis=-1, keepdims=True) + jnp.sum(pn, axis=-1, keepdims=True))
        pc = pc * inv
        pn = pn * inv

        def diff_maps(p):
            return jnp.concatenate([p[0:nq] - lam * p[nq:2 * nq],
                                    p[2 * nq:3 * nq] - lam * p[3 * nq:4 * nq]], axis=0)

        vc = jnp.concatenate([cv_ref[rows_a, :], cv_ref[rows_b, :]], axis=1).astype(BF16)
        vn = jnp.concatenate([vn_ref[ha], vn_ref[hb]], axis=1)
        o2 = (jnp.dot(diff_maps(pc).astype(BF16), vc, preferred_element_type=F32)
              + jnp.dot(diff_maps(pn).astype(BF16), vn, preferred_element_type=F32))
        for h, o in ((ha, o2[0:nq, 0:DV_A]), (hb, o2[nq:2 * nq, DV_A:2 * DV_A])):
            o_ref[:, h * DV_A:(h + 1) * DV_A] = (_rms(o, gh_ref[...]) * (1.0 - lam_init)).astype(BF16)


def _attn_sample(q, kn, vn, cache_k, cache_v, b_off, b, l, lq1, lk1, lq2, lk2, gh, lam_init):
    hspec = pl.BlockSpec((H_A, l, DV_A), lambda i: (0, i, 0))
    cspec = pl.BlockSpec((None,) + cache_k.shape[1:], lambda i: (b_off + i, 0, 0))
    small = lambda shape: pl.BlockSpec(shape, lambda i: (0, 0))
    return pl.pallas_call(
        functools.partial(_attn_sample_kernel, lam_init=lam_init),
        grid=(b,),
        in_specs=[hspec, hspec, hspec, cspec, cspec, small((1, DH_A)), small((1, DH_A)),
                  small((1, DH_A)), small((1, DH_A)), small((1, DV_A))],
        out_specs=pl.BlockSpec((None, l, H_A * DV_A), lambda i: (i, 0, 0)),
        out_shape=jax.ShapeDtypeStruct((b, l, H_A * DV_A), BF16),
        compiler_params=pltpu.CompilerParams(
            dimension_semantics=("arbitrary",), vmem_limit_bytes=V7X_VMEM_LIMIT),
        name="attn_sample",
    )(q, kn, vn, cache_k, cache_v, lq1, lk1, lq2, lk2, gh)


def _split3(x):
    hi = x.astype(BF16)
    r1 = x - hi.astype(F32)
    mid = r1.astype(BF16)
    lo = (r1 - mid.astype(F32)).astype(BF16)
    return hi, mid, lo


def _mlstm_kernel(q_ref, k_ref, v_ref, og_ref, gz_ref, bias_ref, gh_ref, c0_ref, n0_ref, m0_ref,
                  h_ref, c_ref, n_ref, m_ref, c_s, n_s, m_s, *, t, bt):
    c = pl.program_id(1)

    @pl.when(c == 0)
    def _load_state():
        c_s[...] = c0_ref[...]
        n_s[...] = n0_ref[...]
        m_s[...] = m0_ref[...]

    lane = lax.broadcasted_iota(jnp.int32, (t, GATE_LANES), 1)
    row = lax.broadcasted_iota(jnp.int32, (t, t), 0)
    col = lax.broadcasted_iota(jnp.int32, (t, t), 1)
    causal = col <= row
    tril = jnp.where(causal, 1.0, 0.0).astype(BF16)
    sel = jnp.where(lax.broadcasted_iota(jnp.int32, (8, GATE_LANES), 0)
                    == lax.broadcasted_iota(jnp.int32, (8, GATE_LANES), 1), 1.0, 0.0).astype(BF16)

    gz = [gz_ref[b] + bias_ref[...] for b in range(bt)]
    lf = [jnp.minimum(g, 0.0) - jnp.log1p(jnp.exp(-jnp.abs(g))) for g in gz]
    lf = [jnp.where((lane >= H_M) & (lane < 2 * H_M), x, 0.0) for x in lf]
    bcum = [sum(jnp.dot(tril, piece, preferred_element_type=F32) for piece in _split3(x)) for x in lf]
    gc = [jnp.where(lane < H_M, gz[b], bcum[b]) for b in range(bt)]
    gr = [sum(lax.dot_general(sel, piece, NT_DIMS, preferred_element_type=F32) for piece in _split3(x))
          for x in gc]

    chains = [(b, h) for b in range(bt) for h in range(H_M)]
    ids = range(len(chains))
    sls = [slice(h * DH_M, (h + 1) * DH_M) for _, h in chains]
    ig_c = [gc[b][:, h:h + 1] for b, h in chains]
    b_c = [gc[b][:, H_M + h:H_M + h + 1] for b, h in chains]
    ig_r = [gr[b][h:h + 1, :] for b, h in chains]
    b_r = [gr[b][H_M + h:H_M + h + 1, :] for b, h in chains]
    m_prev = [m_s[b, :, h:h + 1] for b, h in chains]
    c_prev = [c_s[b, h] for b, h in chains]
    n_prev = [n_s[b, h:h + 1, :] for b, h in chains]
    q = [q_ref[b, :, sls[i]] for i, (b, _) in enumerate(chains)]
    k = [k_ref[b, :, sls[i]] for i, (b, _) in enumerate(chains)]
    v = [v_ref[b, :, sls[i]] for i, (b, _) in enumerate(chains)]

    qk = [lax.dot_general(q[i], k[i], NT_DIMS, preferred_element_type=F32) for i in ids]
    qc = [lax.dot_general(q[i], c_prev[i].astype(BF16), NT_DIMS, preferred_element_type=F32)
          for i in ids]
    d = [jnp.where(causal, b_c[i] - b_r[i] + ig_r[i], -jnp.inf) for i in ids]
    inter = [b_c[i] + m_prev[i] for i in ids]
    m_t = [jnp.maximum(inter[i], jnp.max(d[i], axis=-1, keepdims=True)) for i in ids]
    w_inter = [jnp.exp(inter[i] - m_t[i]) for i in ids]
    s = [qk[i] * jnp.exp(d[i] - m_t[i]) for i in ids]
    num = [jnp.dot(s[i].astype(BF16), v[i], preferred_element_type=F32) + w_inter[i] * qc[i]
           for i in ids]
    den = [jnp.sum(s[i], axis=-1, keepdims=True)
           + w_inter[i] * jnp.sum(q[i].astype(F32) * n_prev[i], axis=-1, keepdims=True) for i in ids]
    hh = [num[i] / jnp.maximum(jnp.abs(den[i]), jnp.exp(-m_t[i])) for i in ids]
    for i, (b, h) in enumerate(chains):
        h_ref[b, :, sls[i]] = (_rms(hh[i], gh_ref[h:h + 1, :])
                               * og_ref[b, :, sls[i]].astype(F32)).astype(BF16)

    g_last = [b_c[i][t - 1:t, :] for i in ids]
    logw = [g_last[i] - b_c[i] + ig_c[i] for i in ids]
    m_new = [jnp.maximum(g_last[i] + m_prev[i], jnp.max(logw[i], axis=0, keepdims=True)) for i in ids]
    ws = [jnp.exp(logw[i] - m_new[i]) for i in ids]
    wc = [jnp.exp(g_last[i] + m_prev[i] - m_new[i]) for i in ids]
    vw = [(v[i].astype(F32) * ws[i]).astype(BF16) for i in ids]
    for i, (b, h) in enumerate(chains):
        c_s[b, h] = wc[i] * c_prev[i] + lax.dot_general(vw[i], k[i], TN_DIMS,
                                                        preferred_element_type=F32)
        n_s[b, h:h + 1, :] = wc[i] * n_prev[i] + jnp.sum(k[i].astype(F32) * ws[i], axis=0,
                                                          keepdims=True)
        m_s[b, :, h:h + 1] = m_new[i]

    @pl.when(c == pl.num_programs(1) - 1)
    def _store_state():
        c_ref[...] = c_s[...]
        n_ref[...] = n_s[...]
        m_ref[...] = m_s[...]


def _mlstm(q, k, v, og, gz, bias, gh, c0, n0, m0):
    b, l, _ = q.shape
    t = min(MLSTM_CHUNK, l)
    nc = l // t
    bt = min(b, MLSTM_STREAMS)
    seq = lambda width: pl.BlockSpec((bt, t, width), lambda i, j: (i, j, 0))
    const = lambda shape: pl.BlockSpec(shape, lambda i, j: (0,) * len(shape))
    cspec = pl.BlockSpec((bt, H_M, DH_M, DH_M), lambda i, j: (i, 0, 0, 0))
    nspec = pl.BlockSpec((bt, H_M, DH_M), lambda i, j: (i, 0, 0))
    mspec = pl.BlockSpec((bt, 1, H_M), lambda i, j: (i, 0, 0))
    return pl.pallas_call(
        functools.partial(_mlstm_kernel, t=t, bt=bt),
        grid=(b // bt, nc),
        in_specs=[seq(D_MODEL), seq(D_MODEL), seq(D_MODEL), seq(D_MODEL), seq(GATE_LANES),
                  const((1, GATE_LANES)), const((H_M, DH_M)), cspec, nspec, mspec],
        out_specs=[seq(D_MODEL), cspec, nspec, mspec],
        out_shape=[jax.ShapeDtypeStruct((b, l, D_MODEL), BF16),
                   jax.ShapeDtypeStruct((b, H_M, DH_M, DH_M), F32),
                   jax.ShapeDtypeStruct((b, H_M, DH_M), F32),
                   jax.ShapeDtypeStruct((b, 1, H_M), F32)],
        scratch_shapes=[pltpu.VMEM((bt, H_M, DH_M, DH_M), F32),
                        pltpu.VMEM((bt, H_M, DH_M), F32),
                        pltpu.VMEM((bt, 1, H_M), F32)],
        compiler_params=pltpu.CompilerParams(
            dimension_semantics=("arbitrary", "arbitrary"), vmem_limit_bytes=V7X_VMEM_LIMIT),
        name="mlstm",
    )(q, k, v, og, gz, bias, gh, c0, n0, m0)


def _gated_gelu(x, u):
    c = -2.0 * math.sqrt(2.0 / math.pi)
    return (x * u) / (1.0 + jnp.exp(x * (c + (c * 0.044715) * (x * x))))


def _ffn_kernel(x_ref, attn_ref, hg_ref, ga_ref, gm_ref, conv0_ref, wout_ref, wup_ref, wdn_ref,
                gpm_ref, gpf_ref, gpo_ref, cw_ref, cb_ref, y_ref, cs_ref, g_s,
                *, parts, l, tiles_per_batch):
    i = pl.program_id(0)

    @pl.when(i % tiles_per_batch == 0)
    def _from_state():
        g_s[:, CONV_CARRY:CONV_PAD, :] = conv0_ref[...]

    @pl.when(i % tiles_per_batch != 0)
    def _from_prev_tile():
        g_s[:, CONV_CARRY:CONV_PAD, :] = g_s[:, l + CONV_CARRY:l + CONV_PAD, :]

    def pre(rows):
        merged = ga_ref[rows, :] * attn_ref[rows, :] + gm_ref[rows, :] * hg_ref[rows, :]
        x1 = x_ref[rows, :] + _rms(jnp.dot(merged, wout_ref[...], preferred_element_type=F32),
                                   gpm_ref[...])
        return x1, _rms(x1, gpf_ref[...]).astype(BF16)

    def up(h2):
        return jnp.dot(h2, wup_ref[...], preferred_element_type=F32)

    def act(ug, bsl, r0, lh):
        nbh = bsl.stop - bsl.start
        g3 = ug[:, D_FF:].reshape(nbh, lh, D_FF)
        g_s[bsl, CONV_PAD + r0:CONV_PAD + r0 + lh, :] = g3
        gconv = cb_ref[...]
        for j in range(CONV_W):
            lo = CONV_CARRY + j + r0
            gconv = gconv + cw_ref[j:j + 1, :] * (g3 if j == CONV_W - 1 else g_s[bsl, lo:lo + lh, :])
        return _gated_gelu(gconv.reshape(nbh * lh, D_FF), ug[:, :D_FF]).astype(BF16)

    def post(x1, a, rows):
        ff = jnp.dot(a, wdn_ref[...], preferred_element_type=F32)
        y_ref[rows, :] = x1 + _rms(ff, gpo_ref[...])

    n = len(parts)
    x1, h2, ug = [None] * n, [None] * n, [None] * n
    x1[0], h2[0] = pre(parts[0][0])
    for i in range(n + 1):
        if i + 1 < n:
            x1[i + 1], h2[i + 1] = pre(parts[i + 1][0])
        if i < n:
            ug[i] = up(h2[i])
        if i >= 1:
            rows, *conv = parts[i - 1]
            post(x1[i - 1], act(ug[i - 1], *conv), rows)
    cs_ref[...] = g_s[:, l + CONV_CARRY:l + CONV_PAD, :]


def _ffn(x2d, attn, hg, ga, gm, conv0, wout, wup, wdn, gpm, gpf, gpo, cw, cb, nb_total, seq):
    n = x2d.shape[0]
    np_ = FFN_PARTS
    if seq >= FFN_ROWS:
        tm, nb, l = FFN_ROWS, 1, FFN_ROWS
        tiles_per_batch = seq // tm
        pl_ = l // np_
        parts = tuple((slice(j * pl_, (j + 1) * pl_), slice(0, 1), j * pl_, pl_) for j in range(np_))
    else:
        tm, nb, l = n, nb_total, seq
        tiles_per_batch = 1
        pb = nb // np_
        parts = tuple((slice(j * pb * l, (j + 1) * pb * l), slice(j * pb, (j + 1) * pb), 0, l)
                      for j in range(np_))
    row = pl.BlockSpec((tm, D_MODEL), lambda i: (i, 0))
    cstate = pl.BlockSpec((nb, CONV_W - 1, D_FF), lambda i: (i // tiles_per_batch, 0, 0))
    return pl.pallas_call(
        functools.partial(_ffn_kernel, parts=parts, l=l, tiles_per_batch=tiles_per_batch),
        grid=(n // tm,),
        in_specs=[row, row, row, row, row, cstate,
                  _resident((D_MODEL, D_MODEL)), _resident((D_MODEL, 2 * D_FF)),
                  _resident((D_FF, D_MODEL)), _resident((1, D_MODEL)), _resident((1, D_MODEL)),
                  _resident((1, D_MODEL)), _resident((CONV_W, D_FF)), _resident((1, D_FF))],
        out_specs=[row, cstate],
        out_shape=[jax.ShapeDtypeStruct((n, D_MODEL), F32),
                   jax.ShapeDtypeStruct((nb_total, CONV_W - 1, D_FF), F32)],
        scratch_shapes=[pltpu.VMEM((nb, l + CONV_PAD, D_FF), F32)],
        compiler_params=pltpu.CompilerParams(
            dimension_semantics=("arbitrary",), vmem_limit_bytes=V7X_VMEM_LIMIT),
        name="merge_ffn",
    )(x2d, attn, hg, ga, gm, conv0, wout, wup, wdn, gpm, gpf, gpo, cw, cb)


def _layer(x, caches, c0, n0, m0, conv0, wts, lam_init):
    (g_pre_mix, w_in, gate_bias, lq1, lk1, lq2, lk2, g_attn_head, g_mlstm_head, wout, g_post_mix,
     g_pre_ffn, wup, conv_w, conv_b, wdn, g_post_ffn) = wts
    b, l, _ = x.shape
    x2d = x.reshape(b * l, D_MODEL)
    (qa, kf, kb, vf, vb, qm, km, vm, om, ga, gm, gz, qn, kn) = _inproj(x2d, g_pre_mix, w_in, b, l)

    if caches is None:
        per_b = qn.shape[0] // b
        attn = _attn_prompt(qa, kb, vb, qn.reshape(b, per_b, GATE_LANES), kn.reshape(b, per_b, GATE_LANES),
                            lq1, lk1, lq2, lk2, g_attn_head, lam_init)
    else:
        cache_k, cache_v, b_off = caches
        attn = _attn_sample(qa, kb, vb, cache_k, cache_v, b_off, b, l,
                            lq1, lk1, lq2, lk2, g_attn_head, lam_init)

    seq3 = lambda a: a.reshape(b, l, a.shape[-1])
    hm, c1, n1, m1 = _mlstm(seq3(qm), seq3(km), seq3(vm), seq3(om), seq3(gz), gate_bias, g_mlstm_head,
                            c0, n0, m0.reshape(b, 1, H_M))

    y, conv1 = _ffn(x2d, attn.reshape(b * l, D_MODEL), hm.reshape(b * l, D_MODEL), ga, gm,
                    conv0, wout, wup, wdn, g_post_mix, g_pre_ffn, g_post_ffn, conv_w, conv_b, b, l)
    return (y.reshape(b, l, D_MODEL), kf.reshape(b, l, H_A, DV_A), vf.reshape(b, l, H_A, DV_A),
            c1, n1, m1.reshape(b, H_M), conv1)


def kernel(x_prompt, x_sample, cache_k, cache_v, state_C, state_n, state_m, state_conv, g_pre_mix, w_in, b_gates, lam_q1, lam_k1, lam_q2, lam_k2, g_attn_head, g_mlstm_head, w_out, g_post_mix, g_pre_ffn, w_up, conv_w, conv_b, w_down, g_post_ffn):
    depth = w_in.shape[0]
    bp = x_prompt.shape[0]
    yp, ys = x_prompt, x_sample
    outs_p, outs_s = [], []
    for li in range(depth):
        lam_init = 0.8 - 0.6 * math.exp(-0.3 * li)
        gate_bias = jnp.concatenate(
            [b_gates[li], jnp.zeros((GATE_LANES - 2 * H_M,), F32)]).reshape(1, GATE_LANES)
        row = lambda a: a.reshape(1, -1)
        wts = (row(g_pre_mix[li]), w_in[li].astype(BF16), gate_bias, row(lam_q1[li]), row(lam_k1[li]),
               row(lam_q2[li]), row(lam_k2[li]), row(g_attn_head[li]), g_mlstm_head[li],
               w_out[li].astype(BF16), row(g_post_mix[li]), row(g_pre_ffn[li]),
               w_up[li].astype(BF16), conv_w[li], row(conv_b[li]), w_down[li].astype(BF16),
               row(g_post_ffn[li]))
        c0 = jnp.zeros((bp, H_M, DH_M, DH_M), F32)
        n0 = jnp.zeros((bp, H_M, DH_M), F32)
        m0 = jnp.zeros((bp, H_M), F32)
        conv0 = jnp.zeros((bp, CONV_W - 1, D_FF), F32)
        yp, *sp = _layer(yp, None, c0, n0, m0, conv0, wts, lam_init)
        bs, past = cache_k.shape[1], cache_k.shape[2]
        caches = (cache_k.reshape(depth * bs, past * H_A, DV_A),
                  cache_v.reshape(depth * bs, past * H_A, DV_A), li * bs)
        ys, *ss = _layer(ys, caches, state_C[li], state_n[li], state_m[li],
                         state_conv[li], wts, lam_init)
        outs_p.append(sp)
        outs_s.append(ss)
    k_p, v_p, c_p, n_p, m_p, conv_p = [jnp.stack([o[i] for o in outs_p]) for i in range(6)]
    k_s, v_s, c_s, n_s, m_s, conv_s = [jnp.stack([o[i] for o in outs_s]) for i in range(6)]
    return (yp, ys, k_p, v_p, c_p, n_p, m_p, conv_p, k_s, v_s, c_s, n_s, m_s, conv_s)
```

```python
import functools
import math

import jax
import jax.numpy as jnp
import numpy as np
from jax import lax
from jax.experimental import pallas as pl
from jax.experimental.pallas import tpu as pltpu

F32 = jnp.float32
BF16 = jnp.bfloat16

D_MODEL = 1024
CHUNK = 64
H_A = 8
DH_A = 64
DV_A = 2 * DH_A
H_M = 4
DH_M = D_MODEL // H_M
D_FF = 2816
CONV_W = 3
EPS = 1e-6
N_SEG_A = 6
GATE_LANES = 128
QK_SCALE = DH_A ** -0.5 * math.log2(math.e)

V7X_VMEM_LIMIT = 56 * 1024 * 1024

PROJ_ROWS = 256
ATTN_TILE = 512
SCORE_BOUND = 80.0
MLSTM_CHUNK = 256
MLSTM_STREAMS = 4
FFN_ROWS = 512
FFN_PARTS = 2
CONV_PAD = 8
CONV_CARRY = CONV_PAD - (CONV_W - 1)

NT_DIMS = (((1,), (1,)), ((), ()))
TN_DIMS = (((0,), (0,)), ((), ()))


def _rms(x, g):
    return x * lax.rsqrt(jnp.mean(x * x, axis=-1, keepdims=True) + EPS) * g


def _resident(shape):
    nd = len(shape)
    return pl.BlockSpec(shape, lambda *_: (0,) * nd, pipeline_mode=pl.Buffered(1))


def _inproj_kernel(x_ref, g_ref, w_ref, qa_ref, kf_ref, kb_ref, vf_ref, vb_ref,
                   qm_ref, km_ref, vm_ref, om_ref, ga_ref, gm_ref, gz_ref, qn_ref, kn_ref, wb_s, wg_s,
                   *, transposed):
    gate0 = N_SEG_A * D_MODEL

    @pl.when(pl.program_id(0) == 0)
    def _align_tail():
        wb_s[...] = w_ref[:, gate0 + 2 * H_M:]
        lane = lax.broadcasted_iota(jnp.int32, wg_s.shape, 1)
        head = w_ref[:, gate0:gate0 + GATE_LANES]
        wg_s[...] = jnp.where(lane < 2 * H_M, head, jnp.zeros_like(head))

    hb = _rms(x_ref[...], g_ref[...]).astype(BF16)

    def seg(j):
        w, jj = (w_ref, j) if j < N_SEG_A else (wb_s, j - N_SEG_A)
        return jnp.dot(hb, w[:, jj * D_MODEL:(jj + 1) * D_MODEL], preferred_element_type=F32)

    def put_heads(ref, val, transpose):
        dst = ref.at[0] if transposed else ref
        for h in range(H_A):
            piece = val[:, h * DV_A:(h + 1) * DV_A]
            dst[h] = (piece.T if transpose else piece).astype(BF16)

    def max_sq_norm(val):
        lane = lax.broadcasted_iota(jnp.int32, (1, GATE_LANES), 1)
        out = jnp.zeros((1, GATE_LANES), F32)
        for h in range(H_A):
            piece = val[:, h * DV_A:(h + 1) * DV_A]
            top = jnp.max(jnp.sum(piece * piece, axis=-1, keepdims=True), axis=0, keepdims=True)
            out = jnp.where(lane == h, top, out)
        return out

    q = seg(0) * QK_SCALE
    put_heads(qa_ref, q, transposed)
    qn_ref[0] = max_sq_norm(q)
    k = seg(1)
    kf_ref[...] = k
    put_heads(kb_ref, k, False)
    kn_ref[0] = max_sq_norm(k)
    v = seg(2)
    vf_ref[...] = v
    put_heads(vb_ref, v, transposed)
    qm_ref[...] = (seg(3) * DH_M ** -0.5).astype(BF16)
    km_ref[...] = seg(4).astype(BF16)
    vm_ref[...] = seg(5).astype(BF16)
    om_ref[...] = jax.nn.sigmoid(seg(6)).astype(BF16)
    ga_ref[...] = jax.nn.sigmoid(seg(7)).astype(BF16)
    gm_ref[...] = jax.nn.sigmoid(seg(8)).astype(BF16)
    gz_ref[...] = jnp.dot(hb, wg_s[...], preferred_element_type=F32)


def _inproj(x2d, g, w, nb, seq):
    n = x2d.shape[0]
    tm = PROJ_ROWS
    transposed = seq >= tm
    if transposed:
        per_b = seq // tm
        head = pl.BlockSpec((1, H_A, tm, DV_A), lambda i: (i // per_b, 0, i % per_b, 0))
        head_t = pl.BlockSpec((1, H_A, DV_A, tm), lambda i: (i // per_b, 0, 0, i % per_b))
        hm16 = jax.ShapeDtypeStruct((nb, H_A, seq, DV_A), BF16)
        hm16_t = jax.ShapeDtypeStruct((nb, H_A, DV_A, seq), BF16)
    else:
        head = head_t = pl.BlockSpec((H_A, tm, DV_A), lambda i: (0, i, 0))
        hm16 = hm16_t = jax.ShapeDtypeStruct((H_A, n, DV_A), BF16)
    row = pl.BlockSpec((tm, D_MODEL), lambda i: (i, 0))
    norm = pl.BlockSpec((1, 1, GATE_LANES), lambda i: (i, 0, 0))
    norm_shape = jax.ShapeDtypeStruct((n // tm, 1, GATE_LANES), F32)
    full32 = jax.ShapeDtypeStruct((n, D_MODEL), F32)
    full16 = jax.ShapeDtypeStruct((n, D_MODEL), BF16)
    return pl.pallas_call(
        functools.partial(_inproj_kernel, transposed=transposed),
        grid=(n // tm,),
        in_specs=[row, _resident((1, D_MODEL)), _resident(w.shape)],
        out_specs=[head_t, row, head, row, head_t, row, row, row, row, row, row,
                   pl.BlockSpec((tm, GATE_LANES), lambda i: (i, 0)), norm, norm],
        out_shape=[hm16_t, full32, hm16, full32, hm16_t, full16, full16, full16,
                   full16, full16, full16, jax.ShapeDtypeStruct((n, GATE_LANES), F32),
                   norm_shape, norm_shape],
        scratch_shapes=[pltpu.VMEM((D_MODEL, w.shape[1] - N_SEG_A * D_MODEL - 2 * H_M), BF16),
                        pltpu.VMEM((D_MODEL, GATE_LANES), BF16)],
        compiler_params=pltpu.CompilerParams(
            dimension_semantics=("arbitrary",), vmem_limit_bytes=V7X_VMEM_LIMIT),
        name="inproj",
    )(x2d, g, w)


def _lam(lq1, lk1, lq2, lk2, lam_init):
    return (jnp.exp(jnp.sum(lq1 * lk1, axis=-1, keepdims=True))
            - jnp.exp(jnp.sum(lq2 * lk2, axis=-1, keepdims=True)) + lam_init)


def _split_maps(q, axis):
    idx = lax.broadcasted_iota(jnp.int32, q.shape, axis)
    zero = jnp.zeros_like(q)
    return jnp.where(idx < DH_A, q, zero), jnp.where(idx >= DH_A, q, zero)


def _attn_prompt_kernel(qi_tab, kj_tab, qt_ref, k_ref, vt_ref, qn_ref, kn_ref, lq1_ref, lk1_ref, lq2_ref,
                        lk2_ref, gcol_ref, o_ref, qz_s, m_s, l_s, acc_s, mode_s, *, lam_init):
    p = pl.program_id(1)
    qi = qi_tab[p]
    kj = kj_tab[p]
    t = ATTN_TILE
    diag = kj == qi
    pairs = [(h, mp) for h in range(H_A) for mp in range(2)]
    n = len(pairs)

    @pl.when(kj == 0)
    def _init():
        for h in range(H_A):
            q1, q2 = _split_maps(qt_ref[h], 0)
            qz_s[0, h] = q1
            qz_s[1, h] = q2
        m_s[...] = jnp.full(m_s.shape, -jnp.inf, F32)
        l_s[...] = jnp.zeros(l_s.shape, F32)
        acc_s[...] = jnp.zeros(acc_s.shape, F32)
        bound_sq = (jnp.max(qn_ref[...], axis=0, keepdims=True)
                    * jnp.max(kn_ref[...], axis=0, keepdims=True))
        mode_s[0] = jnp.where(jnp.max(bound_sq) <= SCORE_BOUND * SCORE_BOUND, 1, 0)

    def chunk_mask():
        kr = lax.broadcasted_iota(jnp.int32, (t, t), 0) // CHUNK
        qc = lax.broadcasted_iota(jnp.int32, (t, t), 1) // CHUNK
        return kr <= qc

    def scores(h, mp, visible):
        s = jnp.dot(k_ref[h], qz_s[mp, h], preferred_element_type=F32)
        return s if visible is None else jnp.where(visible, s, -jnp.inf)

    def bounded_pass(visible):
        s_next = scores(*pairs[0], visible)
        for i, (h, mp) in enumerate(pairs):
            s = s_next
            if i + 1 < n:
                s_next = scores(*pairs[i + 1], visible)
            pr = jnp.exp2(s)
            l_s[mp, h] = l_s[mp, h] + jnp.sum(pr, axis=0, keepdims=True)
            acc_s[mp, h] = acc_s[mp, h] + jnp.dot(
                vt_ref[h], pr.astype(BF16), preferred_element_type=F32)

    def bounded_diag_pass():
        hq = t // 2
        visible = chunk_mask()
        vis_lo, vis_hi = visible[:hq, :hq], visible[:, hq:]

        def halves(h, mp):
            qz = qz_s[mp, h]
            s_lo = jnp.dot(k_ref[h, :hq, :], qz[:, :hq], preferred_element_type=F32)
            s_hi = jnp.dot(k_ref[h], qz[:, hq:], preferred_element_type=F32)
            return jnp.where(vis_lo, s_lo, -jnp.inf), jnp.where(vis_hi, s_hi, -jnp.inf)

        s_next = halves(*pairs[0])
        for i, (h, mp) in enumerate(pairs):
            s_lo, s_hi = s_next
            if i + 1 < n:
                s_next = halves(*pairs[i + 1])
            p_lo, p_hi = jnp.exp2(s_lo), jnp.exp2(s_hi)
            l_s[mp, h] = l_s[mp, h] + jnp.concatenate(
                [jnp.sum(p_lo, axis=0, keepdims=True), jnp.sum(p_hi, axis=0, keepdims=True)], axis=1)
            acc_s[mp, h] = acc_s[mp, h] + jnp.concatenate(
                [jnp.dot(vt_ref[h, :, :hq], p_lo.astype(BF16), preferred_element_type=F32),
                 jnp.dot(vt_ref[h], p_hi.astype(BF16), preferred_element_type=F32)], axis=1)

    def online_pass(visible):
        s_next = scores(*pairs[0], visible)
        for i, (h, mp) in enumerate(pairs):
            s = s_next
            m_old = m_s[mp, h]
            m_new = jnp.maximum(m_old, jnp.max(s, axis=0, keepdims=True))
            if i + 1 < n:
                s_next = scores(*pairs[i + 1], visible)
            alpha = jnp.exp2(m_old - m_new)
            pr = jnp.exp2(s - m_new)
            l_s[mp, h] = alpha * l_s[mp, h] + jnp.sum(pr, axis=0, keepdims=True)
            acc_s[mp, h] = alpha * acc_s[mp, h] + jnp.dot(
                vt_ref[h], pr.astype(BF16), preferred_element_type=F32)
            m_s[mp, h] = m_new

    bounded = mode_s[0] == 1
    off_diag = jnp.logical_not(diag)

    @pl.when(jnp.logical_and(bounded, off_diag))
    def _bounded_full():
        bounded_pass(None)

    @pl.when(jnp.logical_and(bounded, diag))
    def _bounded_diag():
        bounded_diag_pass()

    @pl.when(jnp.logical_and(jnp.logical_not(bounded), off_diag))
    def _online_full():
        online_pass(None)

    @pl.when(jnp.logical_and(jnp.logical_not(bounded), diag))
    def _online_diag():
        online_pass(chunk_mask())

    @pl.when(diag)
    def _finish():
        lam = _lam(lq1_ref[...], lk1_ref[...], lq2_ref[...], lk2_ref[...], lam_init)
        gain = gcol_ref[...] * (1.0 - lam_init)
        for h in range(H_A):
            ot = acc_s[0, h] * (1.0 / l_s[0, h]) - acc_s[1, h] * (lam / l_s[1, h])
            ms = jnp.mean(ot * ot, axis=0, keepdims=True)
            o_ref[:, h * DV_A:(h + 1) * DV_A] = (ot * lax.rsqrt(ms + EPS) * gain).T.astype(BF16)


def _attn_prompt(qt, k, vt, qn, kn, lq1, lk1, lq2, lk2, gh, lam_init):
    b, _, s, _ = k.shape
    t = ATTN_TILE
    nq = s // t
    pairs = [(i, j) for i in range(nq) for j in range(i + 1)]
    qi_tab = jnp.asarray(np.array([a for a, _ in pairs], np.int32))
    kj_tab = jnp.asarray(np.array([c for _, c in pairs], np.int32))
    qspec = pl.BlockSpec((None, H_A, DV_A, t), lambda bb, p, qi, kj: (bb, 0, 0, qi[p]))
    kspec = pl.BlockSpec((None, H_A, t, DV_A), lambda bb, p, qi, kj: (bb, 0, kj[p], 0))
    vspec = pl.BlockSpec((None, H_A, DV_A, t), lambda bb, p, qi, kj: (bb, 0, 0, kj[p]))
    small = lambda shape: pl.BlockSpec(shape, lambda bb, p, qi, kj: (0, 0))
    nspec = pl.BlockSpec((None,) + qn.shape[1:], lambda bb, p, qi, kj: (bb, 0, 0))
    grid_spec = pltpu.PrefetchScalarGridSpec(
        num_scalar_prefetch=2,
        grid=(b, len(pairs)),
        in_specs=[qspec, kspec, vspec, nspec, nspec, small((1, DH_A)), small((1, DH_A)), small((1, DH_A)),
                  small((1, DH_A)), small((DV_A, 1))],
        out_specs=pl.BlockSpec((None, t, H_A * DV_A), lambda bb, p, qi, kj: (bb, qi[p], 0)),
        scratch_shapes=[pltpu.VMEM((2, H_A, DV_A, t), BF16),
                        pltpu.VMEM((2, H_A, 1, t), F32),
                        pltpu.VMEM((2, H_A, 1, t), F32),
                        pltpu.VMEM((2, H_A, DV_A, t), F32),
                        pltpu.SMEM((1,), jnp.int32)],
    )
    return pl.pallas_call(
        functools.partial(_attn_prompt_kernel, lam_init=lam_init),
        grid_spec=grid_spec,
        out_shape=jax.ShapeDtypeStruct((b, s, H_A * DV_A), BF16),
        compiler_params=pltpu.CompilerParams(
            dimension_semantics=("arbitrary", "arbitrary"), vmem_limit_bytes=V7X_VMEM_LIMIT),
        name="attn_prompt",
    )(qi_tab, kj_tab, qt, k, vt, qn, kn, lq1, lk1, lq2, lk2, gh.reshape(DV_A, 1))


def _attn_sample_kernel(q_ref, kn_ref, vn_ref, ck_ref, cv_ref, lq1_ref, lk1_ref, lq2_ref, lk2_ref,
                        gh_ref, o_ref, *, lam_init):
    lam = _lam(lq1_ref[...], lk1_ref[...], lq2_ref[...], lk2_ref[...], lam_init)
    nq = q_ref.shape[1]
    past = ck_ref.shape[0] // H_A
    zero = jnp.zeros((2 * nq, DV_A), BF16)
    for ha in range(0, H_A, 2):
        hb = ha + 1
        rows_a, rows_b = pl.ds(ha, past, stride=H_A), pl.ds(hb, past, stride=H_A)
        qa = jnp.concatenate(_split_maps(q_ref[ha], 1), axis=0)
        qb = jnp.concatenate(_split_maps(q_ref[hb], 1), axis=0)
        qz = jnp.concatenate([jnp.concatenate([qa, zero], axis=1),
                              jnp.concatenate([zero, qb], axis=1)], axis=0)
        kc = jnp.concatenate([ck_ref[rows_a, :], ck_ref[rows_b, :]], axis=1).astype(BF16)
        kn = jnp.concatenate([kn_ref[ha], kn_ref[hb]], axis=1)
        sc = lax.dot_general(qz, kc, NT_DIMS, preferred_element_type=F32)
        sn = lax.dot_general(qz, kn, NT_DIMS, preferred_element_type=F32)
        m = jnp.maximum(jnp.max(sc, axis=-1, keepdims=True), jnp.max(sn, axis=-1, keepdims=True))
        pc = jnp.exp2(sc - m)
        pn = jnp.exp2(sn - m)
        inv = 1.0 / (jnp.sum(pc, axis=-1, keepdims=True) + jnp.sum(pn, axis=-1, keepdims=True))
        pc = pc * inv
        pn = pn * inv

        def diff_maps(p):
            return jnp.concatenate([p[0:nq] - lam * p[nq:2 * nq],
                                    p[2 * nq:3 * nq] - lam * p[3 * nq:4 * nq]], axis=0)

        vc = jnp.concatenate([cv_ref[rows_a, :], cv_ref[rows_b, :]], axis=1).astype(BF16)
        vn = jnp.concatenate([vn_ref[ha], vn_ref[hb]], axis=1)
        o2 = (jnp.dot(diff_maps(pc).astype(BF16), vc, preferred_element_type=F32)
              + jnp.dot(diff_maps(pn).astype(BF16), vn, preferred_element_type=F32))
        for h, o in ((ha, o2[0:nq, 0:DV_A]), (hb, o2[nq:2 * nq, DV_A:2 * DV_A])):
            o_ref[:, h * DV_A:(h + 1) * DV_A] = (_rms(o, gh_ref[...]) * (1.0 - lam_init)).astype(BF16)


def _attn_sample(q, kn, vn, cache_k, cache_v, b_off, b, l, lq1, lk1, lq2, lk2, gh, lam_init):
    hspec = pl.BlockSpec((H_A, l, DV_A), lambda i: (0, i, 0))
    cspec = pl.BlockSpec((None,) + cache_k.shape[1:], lambda i: (b_off + i, 0, 0))
    small = lambda shape: pl.BlockSpec(shape, lambda i: (0, 0))
    return pl.pallas_call(
        functools.partial(_attn_sample_kernel, lam_init=lam_init),
        grid=(b,),
        in_specs=[hspec, hspec, hspec, cspec, cspec, small((1, DH_A)), small((1, DH_A)),
                  small((1, DH_A)), small((1, DH_A)), small((1, DV_A))],
        out_specs=pl.BlockSpec((None, l, H_A * DV_A), lambda i: (i, 0, 0)),
        out_shape=jax.ShapeDtypeStruct((b, l, H_A * DV_A), BF16),
        compiler_params=pltpu.CompilerParams(
            dimension_semantics=("arbitrary",), vmem_limit_bytes=V7X_VMEM_LIMIT),
        name="attn_sample",
    )(q, kn, vn, cache_k, cache_v, lq1, lk1, lq2, lk2, gh)


def _split3(x):
    hi = x.astype(BF16)
    r1 = x - hi.astype(F32)
    mid = r1.astype(BF16)
    lo = (r1 - mid.astype(F32)).astype(BF16)
    return hi, mid, lo


def _mlstm_kernel(q_ref, k_ref, v_ref, og_ref, gz_ref, bias_ref, gh_ref, c0_ref, n0_ref, m0_ref,
                  h_ref, c_ref, n_ref, m_ref, c_s, n_s, m_s, *, t, bt):
    c = pl.program_id(1)

    @pl.when(c == 0)
    def _load_state():
        c_s[...] = c0_ref[...]
        n_s[...] = n0_ref[...]
        m_s[...] = m0_ref[...]

    lane = lax.broadcasted_iota(jnp.int32, (t, GATE_LANES), 1)
    row = lax.broadcasted_iota(jnp.int32, (t, t), 0)
    col = lax.broadcasted_iota(jnp.int32, (t, t), 1)
    causal = col <= row
    tril = jnp.where(causal, 1.0, 0.0).astype(BF16)
    sel = jnp.where(lax.broadcasted_iota(jnp.int32, (8, GATE_LANES), 0)
                    == lax.broadcasted_iota(jnp.int32, (8, GATE_LANES), 1), 1.0, 0.0).astype(BF16)

    gz = [gz_ref[b] + bias_ref[...] for b in range(bt)]
    lf = [jnp.minimum(g, 0.0) - jnp.log1p(jnp.exp(-jnp.abs(g))) for g in gz]
    lf = [jnp.where((lane >= H_M) & (lane < 2 * H_M), x, 0.0) for x in lf]
    bcum = [sum(jnp.dot(tril, piece, preferred_element_type=F32) for piece in _split3(x)) for x in lf]
    gc = [jnp.where(lane < H_M, gz[b], bcum[b]) for b in range(bt)]
    gr = [sum(lax.dot_general(sel, piece, NT_DIMS, preferred_element_type=F32) for piece in _split3(x))
          for x in gc]

    chains = [(b, h) for b in range(bt) for h in range(H_M)]
    ids = range(len(chains))
    sls = [slice(h * DH_M, (h + 1) * DH_M) for _, h in chains]
    ig_c = [gc[b][:, h:h + 1] for b, h in chains]
    b_c = [gc[b][:, H_M + h:H_M + h + 1] for b, h in chains]
    ig_r = [gr[b][h:h + 1, :] for b, h in chains]
    b_r = [gr[b][H_M + h:H_M + h + 1, :] for b, h in chains]
    m_prev = [m_s[b, :, h:h + 1] for b, h in chains]
    c_prev = [c_s[b, h] for b, h in chains]
    n_prev = [n_s[b, h:h + 1, :] for b, h in chains]
    q = [q_ref[b, :, sls[i]] for i, (b, _) in enumerate(chains)]
    k = [k_ref[b, :, sls[i]] for i, (b, _) in enumerate(chains)]
    v = [v_ref[b, :, sls[i]] for i, (b, _) in enumerate(chains)]

    qk = [lax.dot_general(q[i], k[i], NT_DIMS, preferred_element_type=F32) for i in ids]
    qc = [lax.dot_general(q[i], c_prev[i].astype(BF16), NT_DIMS, preferred_element_type=F32)
          for i in ids]
    d = [jnp.where(causal, b_c[i] - b_r[i] + ig_r[i], -jnp.inf) for i in ids]
    inter = [b_c[i] + m_prev[i] for i in ids]
    m_t = [jnp.maximum(inter[i], jnp.max(d[i], axis=-1, keepdims=True)) for i in ids]
    w_inter = [jnp.exp(inter[i] - m_t[i]) for i in ids]
    s = [qk[i] * jnp.exp(d[i] - m_t[i]) for i in ids]
    num = [jnp.dot(s[i].astype(BF16), v[i], preferred_element_type=F32) + w_inter[i] * qc[i]
           for i in ids]
    den = [jnp.sum(s[i], axis=-1, keepdims=True)
           + w_inter[i] * jnp.sum(q[i].astype(F32) * n_prev[i], axis=-1, keepdims=True) for i in ids]
    hh = [num[i] / jnp.maximum(jnp.abs(den[i]), jnp.exp(-m_t[i])) for i in ids]
    for i, (b, h) in enumerate(chains):
        h_ref[b, :, sls[i]] = (_rms(hh[i], gh_ref[h:h + 1, :])
                               * og_ref[b, :, sls[i]].astype(F32)).astype(BF16)

    g_last = [b_c[i][t - 1:t, :] for i in ids]
    logw = [g_last[i] - b_c[i] + ig_c[i] for i in ids]
    m_new = [jnp.maximum(g_last[i] + m_prev[i], jnp.max(logw[i], axis=0, keepdims=True)) for i in ids]
    ws = [jnp.exp(logw[i] - m_new[i]) for i in ids]
    wc = [jnp.exp(g_last[i] + m_prev[i] - m_new[i]) for i in ids]
    vw = [(v[i].astype(F32) * ws[i]).astype(BF16) for i in ids]
    for i, (b, h) in enumerate(chains):
        c_s[b, h] = wc[i] * c_prev[i] + lax.dot_general(vw[i], k[i], TN_DIMS,
                                                        preferred_element_type=F32)
        n_s[b, h:h + 1, :] = wc[i] * n_prev[i] + jnp.sum(k[i].astype(F32) * ws[i], axis=0,
                                                          keepdims=True)
        m_s[b, :, h:h + 1] = m_new[i]

    @pl.when(c == pl.num_programs(1) - 1)
    def _store_state():
        c_ref[...] = c_s[...]
        n_ref[...] = n_s[...]
        m_ref[...] = m_s[...]


def _mlstm(q, k, v, og, gz, bias, gh, c0, n0, m0):
    b, l, _ = q.shape
    t = min(MLSTM_CHUNK, l)
    nc = l // t
    bt = min(b, MLSTM_STREAMS)
    seq = lambda width: pl.BlockSpec((bt, t, width), lambda i, j: (i, j, 0))
    const = lambda shape: pl.BlockSpec(shape, lambda i, j: (0,) * len(shape))
    cspec = pl.BlockSpec((bt, H_M, DH_M, DH_M), lambda i, j: (i, 0, 0, 0))
    nspec = pl.BlockSpec((bt, H_M, DH_M), lambda i, j: (i, 0, 0))
    mspec = pl.BlockSpec((bt, 1, H_M), lambda i, j: (i, 0, 0))
    return pl.pallas_call(
        functools.partial(_mlstm_kernel, t=t, bt=bt),
        grid=(b // bt, nc),
        in_specs=[seq(D_MODEL), seq(D_MODEL), seq(D_MODEL), seq(D_MODEL), seq(GATE_LANES),
                  const((1, GATE_LANES)), const((H_M, DH_M)), cspec, nspec, mspec],
        out_specs=[seq(D_MODEL), cspec, nspec, mspec],
        out_shape=[jax.ShapeDtypeStruct((b, l, D_MODEL), BF16),
                   jax.ShapeDtypeStruct((b, H_M, DH_M, DH_M), F32),
                   jax.ShapeDtypeStruct((b, H_M, DH_M), F32),
                   jax.ShapeDtypeStruct((b, 1, H_M), F32)],
        scratch_shapes=[pltpu.VMEM((bt, H_M, DH_M, DH_M), F32),
                        pltpu.VMEM((bt, H_M, DH_M), F32),
                        pltpu.VMEM((bt, 1, H_M), F32)],
        compiler_params=pltpu.CompilerParams(
            dimension_semantics=("arbitrary", "arbitrary"), vmem_limit_bytes=V7X_VMEM_LIMIT),
        name="mlstm",
    )(q, k, v, og, gz, bias, gh, c0, n0, m0)


def _gated_gelu(x, u):
    c = -2.0 * math.sqrt(2.0 / math.pi)
    return (x * u) / (1.0 + jnp.exp(x * (c + (c * 0.044715) * (x * x))))


def _ffn_kernel(x_ref, attn_ref, hg_ref, ga_ref, gm_ref, conv0_ref, wout_ref, wup_ref, wdn_ref,
                gpm_ref, gpf_ref, gpo_ref, cw_ref, cb_ref, y_ref, cs_ref, g_s,
                *, parts, l, tiles_per_batch):
    i = pl.program_id(0)

    @pl.when(i % tiles_per_batch == 0)
    def _from_state():
        g_s[:, CONV_CARRY:CONV_PAD, :] = conv0_ref[...]

    @pl.when(i % tiles_per_batch != 0)
    def _from_prev_tile():
        g_s[:, CONV_CARRY:CONV_PAD, :] = g_s[:, l + CONV_CARRY:l + CONV_PAD, :]

    def pre(rows):
        merged = ga_ref[rows, :] * attn_ref[rows, :] + gm_ref[rows, :] * hg_ref[rows, :]
        x1 = x_ref[rows, :] + _rms(jnp.dot(merged, wout_ref[...], preferred_element_type=F32),
                                   gpm_ref[...])
        return x1, _rms(x1, gpf_ref[...]).astype(BF16)

    def up(h2):
        return jnp.dot(h2, wup_ref[...], preferred_element_type=F32)

    def act(ug, bsl, r0, lh):
        nbh = bsl.stop - bsl.start
        g3 = ug[:, D_FF:].reshape(nbh, lh, D_FF)
        g_s[bsl, CONV_PAD + r0:CONV_PAD + r0 + lh, :] = g3
        gconv = cb_ref[...]
        for j in range(CONV_W):
            lo = CONV_CARRY + j + r0
            gconv = gconv + cw_ref[j:j + 1, :] * (g3 if j == CONV_W - 1 else g_s[bsl, lo:lo + lh, :])
        return _gated_gelu(gconv.reshape(nbh * lh, D_FF), ug[:, :D_FF]).astype(BF16)

    def post(x1, a, rows):
        ff = jnp.dot(a, wdn_ref[...], preferred_element_type=F32)
        y_ref[rows, :] = x1 + _rms(ff, gpo_ref[...])

    n = len(parts)
    x1, h2, ug = [None] * n, [None] * n, [None] * n
    x1[0], h2[0] = pre(parts[0][0])
    for i in range(n + 1):
        if i + 1 < n:
            x1[i + 1], h2[i + 1] = pre(parts[i + 1][0])
        if i < n:
            ug[i] = up(h2[i])
        if i >= 1:
            rows, *conv = parts[i - 1]
            post(x1[i - 1], act(ug[i - 1], *conv), rows)
    cs_ref[...] = g_s[:, l + CONV_CARRY:l + CONV_PAD, :]


def _ffn(x2d, attn, hg, ga, gm, conv0, wout, wup, wdn, gpm, gpf, gpo, cw, cb, nb_total, seq):
    n = x2d.shape[0]
    np_ = FFN_PARTS
    if seq >= FFN_ROWS:
        tm, nb, l = FFN_ROWS, 1, FFN_ROWS
        tiles_per_batch = seq // tm
        pl_ = l // np_
        parts = tuple((slice(j * pl_, (j + 1) * pl_), slice(0, 1), j * pl_, pl_) for j in range(np_))
    else:
        tm, nb, l = n, nb_total, seq
        tiles_per_batch = 1
        pb = nb // np_
        parts = tuple((slice(j * pb * l, (j + 1) * pb * l), slice(j * pb, (j + 1) * pb), 0, l)
                      for j in range(np_))
    row = pl.BlockSpec((tm, D_MODEL), lambda i: (i, 0))
    cstate = pl.BlockSpec((nb, CONV_W - 1, D_FF), lambda i: (i // tiles_per_batch, 0, 0))
    return pl.pallas_call(
        functools.partial(_ffn_kernel, parts=parts, l=l, tiles_per_batch=tiles_per_batch),
        grid=(n // tm,),
        in_specs=[row, row, row, row, row, cstate,
                  _resident((D_MODEL, D_MODEL)), _resident((D_MODEL, 2 * D_FF)),
                  _resident((D_FF, D_MODEL)), _resident((1, D_MODEL)), _resident((1, D_MODEL)),
                  _resident((1, D_MODEL)), _resident((CONV_W, D_FF)), _resident((1, D_FF))],
        out_specs=[row, cstate],
        out_shape=[jax.ShapeDtypeStruct((n, D_MODEL), F32),
                   jax.ShapeDtypeStruct((nb_total, CONV_W - 1, D_FF), F32)],
        scratch_shapes=[pltpu.VMEM((nb, l + CONV_PAD, D_FF), F32)],
        compiler_params=pltpu.CompilerParams(
            dimension_semantics=("arbitrary",), vmem_limit_bytes=V7X_VMEM_LIMIT),
        name="merge_ffn",
    )(x2d, attn, hg, ga, gm, conv0, wout, wup, wdn, gpm, gpf, gpo, cw, cb)


def _layer(x, caches, c0, n0, m0, conv0, wts, lam_init):
    (g_pre_mix, w_in, gate_bias, lq1, lk1, lq2, lk2, g_attn_head, g_mlstm_head, wout, g_post_mix,
     g_pre_ffn, wup, conv_w, conv_b, wdn, g_post_ffn) = wts
    b, l, _ = x.shape
    x2d = x.reshape(b * l, D_MODEL)
    (qa, kf, kb, vf, vb, qm, km, vm, om, ga, gm, gz, qn, kn) = _inproj(x2d, g_pre_mix, w_in, b, l)

    if caches is None:
        per_b = qn.shape[0] // b
        attn = _attn_prompt(qa, kb, vb, qn.reshape(b, per_b, GATE_LANES), kn.reshape(b, per_b, GATE_LANES),
                            lq1, lk1, lq2, lk2, g_attn_head, lam_init)
    else:
        cache_k, cache_v, b_off = caches
        attn = _attn_sample(qa, kb, vb, cache_k, cache_v, b_off, b, l,
                            lq1, lk1, lq2, lk2, g_attn_head, lam_init)

    seq3 = lambda a: a.reshape(b, l, a.shape[-1])
    hm, c1, n1, m1 = _mlstm(seq3(qm), seq3(km), seq3(vm), seq3(om), seq3(gz), gate_bias, g_mlstm_head,
                            c0, n0, m0.reshape(b, 1, H_M))

    y, conv1 = _ffn(x2d, attn.reshape(b * l, D_MODEL), hm.reshape(b * l, D_MODEL), ga, gm,
                    conv0, wout, wup, wdn, g_post_mix, g_pre_ffn, g_post_ffn, conv_w, conv_b, b, l)
    return (y.reshape(b, l, D_MODEL), kf.reshape(b, l, H_A, DV_A), vf.reshape(b, l, H_A, DV_A),
            c1, n1, m1.reshape(b, H_M), conv1)


def kernel(x_prompt, x_sample, cache_k, cache_v, state_C, state_n, state_m, state_conv, g_pre_mix, w_in, b_gates, lam_q1, lam_k1, lam_q2, lam_k2, g_attn_head, g_mlstm_head, w_out, g_post_mix, g_pre_ffn, w_up, conv_w, conv_b, w_down, g_post_ffn):
    depth = w_in.shape[0]
    bp = x_prompt.shape[0]
    yp, ys = x_prompt, x_sample
    outs_p, outs_s = [], []
    for li in range(depth):
        lam_init = 0.8 - 0.6 * math.exp(-0.3 * li)
        gate_bias = jnp.concatenate(
            [b_gates[li], jnp.zeros((GATE_LANES - 2 * H_M,), F32)]).reshape(1, GATE_LANES)
        row = lambda a: a.reshape(1, -1)
        wts = (row(g_pre_mix[li]), w_in[li].astype(BF16), gate_bias, row(lam_q1[li]), row(lam_k1[li]),
               row(lam_q2[li]), row(lam_k2[li]), row(g_attn_head[li]), g_mlstm_head[li],
               w_out[li].astype(BF16), row(g_post_mix[li]), row(g_pre_ffn[li]),
               w_up[li].astype(BF16), conv_w[li], row(conv_b[li]), w_down[li].astype(BF16),
               row(g_post_ffn[li]))
        c0 = jnp.zeros((bp, H_M, DH_M, DH_M), F32)
        n0 = jnp.zeros((bp, H_M, DH_M), F32)
        m0 = jnp.zeros((bp, H_M), F32)
        conv0 = jnp.zeros((bp, CONV_W - 1, D_FF), F32)
        yp, *sp = _layer(yp, None, c0, n0, m0, conv0, wts, lam_init)
        bs, past = cache_k.shape[1], cache_k.shape[2]
        caches = (cache_k.reshape(depth * bs, past * H_A, DV_A),
                  cache_v.reshape(depth * bs, past * H_A, DV_A), li * bs)
        ys, *ss = _layer(ys, caches, state_C[li], state_n[li], state_m[li],
                         state_conv[li], wts, lam_init)
        outs_p.append(sp)
        outs_s.append(ss)
    k_p, v_p, c_p, n_p, m_p, conv_p = [jnp.stack([o[i] for o in outs_p]) for i in range(6)]
    k_s, v_s, c_s, n_s, m_s, conv_s = [jnp.stack([o[i] for o in outs_s]) for i in range(6)]
    return (yp, ys, k_p, v_p, c_p, n_p, m_p, conv_p, k_s, v_s, c_s, n_s, m_s, conv_s)
```

```python
import functools
import math

import jax
import jax.numpy as jnp
import numpy as np
from jax import lax
from jax.experimental import pallas as pl
from jax.experimental.pallas import tpu as pltpu

F32 = jnp.float32
BF16 = jnp.bfloat16

D_MODEL = 1024
CHUNK = 64
H_A = 8
DH_A = 64
DV_A = 2 * DH_A
H_M = 4
DH_M = D_MODEL // H_M
D_FF = 2816
CONV_W = 3
EPS = 1e-6
N_SEG_A = 6
GATE_LANES = 128
QK_SCALE = DH_A ** -0.5 * math.log2(math.e)

V7X_VMEM_LIMIT = 56 * 1024 * 1024

PROJ_ROWS = 256
ATTN_TILE = 512
SCORE_BOUND = 80.0
MLSTM_CHUNK = 256
MLSTM_STREAMS = 4
FFN_ROWS = 512
FFN_PARTS = 2
CONV_PAD = 8
CONV_CARRY = CONV_PAD - (CONV_W - 1)

NT_DIMS = (((1,), (1,)), ((), ()))
TN_DIMS = (((0,), (0,)), ((), ()))


def _rms(x, g):
    return x * lax.rsqrt(jnp.mean(x * x, axis=-1, keepdims=True) + EPS) * g


def _resident(shape):
    nd = len(shape)
    return pl.BlockSpec(shape, lambda *_: (0,) * nd, pipeline_mode=pl.Buffered(1))


def _inproj_kernel(x_ref, g_ref, w_ref, qa_ref, kf_ref, kb_ref, vf_ref, vb_ref,
                   qm_ref, km_ref, vm_ref, om_ref, ga_ref, gm_ref, gz_ref, qn_ref, kn_ref, wb_s, wg_s,
                   *, transposed):
    gate0 = N_SEG_A * D_MODEL

    @pl.when(pl.program_id(0) == 0)
    def _align_tail():
        wb_s[...] = w_ref[:, gate0 + 2 * H_M:]
        lane = lax.broadcasted_iota(jnp.int32, wg_s.shape, 1)
        head = w_ref[:, gate0:gate0 + GATE_LANES]
        wg_s[...] = jnp.where(lane < 2 * H_M, head, jnp.zeros_like(head))

    hb = _rms(x_ref[...], g_ref[...]).astype(BF16)

    def seg(j):
        w, jj = (w_ref, j) if j < N_SEG_A else (wb_s, j - N_SEG_A)
        return jnp.dot(hb, w[:, jj * D_MODEL:(jj + 1) * D_MODEL], preferred_element_type=F32)

    def put_heads(ref, val, transpose):
        dst = ref.at[0] if transposed else ref
        for h in range(H_A):
            piece = val[:, h * DV_A:(h + 1) * DV_A]
            dst[h] = (piece.T if transpose else piece).astype(BF16)

    def max_sq_norm(val):
        lane = lax.broadcasted_iota(jnp.int32, (1, GATE_LANES), 1)
        out = jnp.zeros((1, GATE_LANES), F32)
        for h in range(H_A):
            piece = val[:, h * DV_A:(h + 1) * DV_A]
            top = jnp.max(jnp.sum(piece * piece, axis=-1, keepdims=True), axis=0, keepdims=True)
            out = jnp.where(lane == h, top, out)
        return out

    q = seg(0) * QK_SCALE
    put_heads(qa_ref, q, transposed)
    qn_ref[0] = max_sq_norm(q)
    k = seg(1)
    kf_ref[...] = k
    put_heads(kb_ref, k, False)
    kn_ref[0] = max_sq_norm(k)
    v = seg(2)
    vf_ref[...] = v
    put_heads(vb_ref, v, transposed)
    qm_ref[...] = (seg(3) * DH_M ** -0.5).astype(BF16)
    km_ref[...] = seg(4).astype(BF16)
    vm_ref[...] = seg(5).astype(BF16)
    om_ref[...] = jax.nn.sigmoid(seg(6)).astype(BF16)
    ga_ref[...] = jax.nn.sigmoid(seg(7)).astype(BF16)
    gm_ref[...] = jax.nn.sigmoid(seg(8)).astype(BF16)
    gz_ref[...] = jnp.dot(hb, wg_s[...], preferred_element_type=F32)


def _inproj(x2d, g, w, nb, seq):
    n = x2d.shape[0]
    tm = PROJ_ROWS
    transposed = seq >= tm
    if transposed:
        per_b = seq // tm
        head = pl.BlockSpec((1, H_A, tm, DV_A), lambda i: (i // per_b, 0, i % per_b, 0))
        head_t = pl.BlockSpec((1, H_A, DV_A, tm), lambda i: (i // per_b, 0, 0, i % per_b))
        hm16 = jax.ShapeDtypeStruct((nb, H_A, seq, DV_A), BF16)
        hm16_t = jax.ShapeDtypeStruct((nb, H_A, DV_A, seq), BF16)
    else:
        head = head_t = pl.BlockSpec((H_A, tm, DV_A), lambda i: (0, i, 0))
        hm16 = hm16_t = jax.ShapeDtypeStruct((H_A, n, DV_A), BF16)
    row = pl.BlockSpec((tm, D_MODEL), lambda i: (i, 0))
    norm = pl.BlockSpec((1, 1, GATE_LANES), lambda i: (i, 0, 0))
    norm_shape = jax.ShapeDtypeStruct((n // tm, 1, GATE_LANES), F32)
    full32 = jax.ShapeDtypeStruct((n, D_MODEL), F32)
    full16 = jax.ShapeDtypeStruct((n, D_MODEL), BF16)
    return pl.pallas_call(
        functools.partial(_inproj_kernel, transposed=transposed),
        grid=(n // tm,),
        in_specs=[row, _resident((1, D_MODEL)), _resident(w.shape)],
        out_specs=[head_t, row, head, row, head_t, row, row, row, row, row, row,
                   pl.BlockSpec((tm, GATE_LANES), lambda i: (i, 0)), norm, norm],
        out_shape=[hm16_t, full32, hm16, full32, hm16_t, full16, full16, full16,
                   full16, full16, full16, jax.ShapeDtypeStruct((n, GATE_LANES), F32),
                   norm_shape, norm_shape],
        scratch_shapes=[pltpu.VMEM((D_MODEL, w.shape[1] - N_SEG_A * D_MODEL - 2 * H_M), BF16),
                        pltpu.VMEM((D_MODEL, GATE_LANES), BF16)],
        compiler_params=pltpu.CompilerParams(
            dimension_semantics=("arbitrary",), vmem_limit_bytes=V7X_VMEM_LIMIT),
        name="inproj",
    )(x2d, g, w)


def _lam(lq1, lk1, lq2, lk2, lam_init):
    return (jnp.exp(jnp.sum(lq1 * lk1, axis=-1, keepdims=True))
            - jnp.exp(jnp.sum(lq2 * lk2, axis=-1, keepdims=True)) + lam_init)


def _split_maps(q, axis):
    idx = lax.broadcasted_iota(jnp.int32, q.shape, axis)
    zero = jnp.zeros_like(q)
    return jnp.where(idx < DH_A, q, zero), jnp.where(idx >= DH_A, q, zero)


def _attn_prompt_kernel(qi_tab, kj_tab, qt_ref, k_ref, vt_ref, qn_ref, kn_ref, lq1_ref, lk1_ref, lq2_ref,
                        lk2_ref, gcol_ref, o_ref, qz_s, m_s, l_s, acc_s, mode_s, *, lam_init):
    p = pl.program_id(1)
    qi = qi_tab[p]
    kj = kj_tab[p]
    t = ATTN_TILE
    diag = kj == qi
    pairs = [(h, mp) for h in range(H_A) for mp in range(2)]
    n = len(pairs)

    @pl.when(kj == 0)
    def _init():
        for h in range(H_A):
            q1, q2 = _split_maps(qt_ref[h], 0)
            qz_s[0, h] = q1
            qz_s[1, h] = q2
        m_s[...] = jnp.full(m_s.shape, -jnp.inf, F32)
        bound_sq = (jnp.max(qn_ref[...], axis=0, keepdims=True)
                    * jnp.max(kn_ref[...], axis=0, keepdims=True))
        is_bounded = jnp.max(bound_sq) <= SCORE_BOUND * SCORE_BOUND
        mode_s[0] = jnp.where(is_bounded, 1, 0)

        @pl.when(jnp.logical_or(jnp.logical_not(is_bounded), diag))
        def _zero():
            l_s[...] = jnp.zeros(l_s.shape, F32)
            acc_s[...] = jnp.zeros(acc_s.shape, F32)

    def chunk_mask():
        kr = lax.broadcasted_iota(jnp.int32, (t, t), 0) // CHUNK
        qc = lax.broadcasted_iota(jnp.int32, (t, t), 1) // CHUNK
        return kr <= qc

    def scores(h, mp, visible):
        s = jnp.dot(k_ref[h], qz_s[mp, h], preferred_element_type=F32)
        return s if visible is None else jnp.where(visible, s, -jnp.inf)

    def bounded_pass(visible, assign=False):
        s_next = scores(*pairs[0], visible)
        for i, (h, mp) in enumerate(pairs):
            s = s_next
            if i + 1 < n:
                s_next = scores(*pairs[i + 1], visible)
            pr = jnp.exp2(s)
            lt = jnp.sum(pr, axis=0, keepdims=True)
            at = jnp.dot(vt_ref[h], pr.astype(BF16), preferred_element_type=F32)
            l_s[mp, h] = lt if assign else l_s[mp, h] + lt
            acc_s[mp, h] = at if assign else acc_s[mp, h] + at

    def bounded_diag_pass():
        hq = t // 2
        visible = chunk_mask()
        vis_lo, vis_hi = visible[:hq, :hq], visible[:, hq:]

        def halves(h, mp):
            qz = qz_s[mp, h]
            s_lo = jnp.dot(k_ref[h, :hq, :], qz[:, :hq], preferred_element_type=F32)
            s_hi = jnp.dot(k_ref[h], qz[:, hq:], preferred_element_type=F32)
            return jnp.where(vis_lo, s_lo, -jnp.inf), jnp.where(vis_hi, s_hi, -jnp.inf)

        s_next = halves(*pairs[0])
        for i, (h, mp) in enumerate(pairs):
            s_lo, s_hi = s_next
            if i + 1 < n:
                s_next = halves(*pairs[i + 1])
            p_lo, p_hi = jnp.exp2(s_lo), jnp.exp2(s_hi)
            l_s[mp, h] = l_s[mp, h] + jnp.concatenate(
                [jnp.sum(p_lo, axis=0, keepdims=True), jnp.sum(p_hi, axis=0, keepdims=True)], axis=1)
            acc_s[mp, h] = acc_s[mp, h] + jnp.concatenate(
                [jnp.dot(vt_ref[h, :, :hq], p_lo.astype(BF16), preferred_element_type=F32),
                 jnp.dot(vt_ref[h], p_hi.astype(BF16), preferred_element_type=F32)], axis=1)

    def online_pass(visible):
        s_next = scores(*pairs[0], visible)
        for i, (h, mp) in enumerate(pairs):
            s = s_next
            m_old = m_s[mp, h]
            m_new = jnp.maximum(m_old, jnp.max(s, axis=0, keepdims=True))
            if i + 1 < n:
                s_next = scores(*pairs[i + 1], visible)
            alpha = jnp.exp2(m_old - m_new)
            pr = jnp.exp2(s - m_new)
            l_s[mp, h] = alpha * l_s[mp, h] + jnp.sum(pr, axis=0, keepdims=True)
            acc_s[mp, h] = alpha * acc_s[mp, h] + jnp.dot(
                vt_ref[h], pr.astype(BF16), preferred_element_type=F32)
            m_s[mp, h] = m_new

    bounded = mode_s[0] == 1
    off_diag = jnp.logical_not(diag)

    @pl.when(jnp.logical_and(jnp.logical_and(bounded, off_diag), kj == 0))
    def _bounded_first():
        bounded_pass(None, assign=True)

    @pl.when(jnp.logical_and(jnp.logical_and(bounded, off_diag), kj != 0))
    def _bounded_full():
        bounded_pass(None)

    @pl.when(jnp.logical_and(bounded, diag))
    def _bounded_diag():
        bounded_diag_pass()

    @pl.when(jnp.logical_and(jnp.logical_not(bounded), off_diag))
    def _online_full():
        online_pass(None)

    @pl.when(jnp.logical_and(jnp.logical_not(bounded), diag))
    def _online_diag():
        online_pass(chunk_mask())

    @pl.when(diag)
    def _finish():
        lam = _lam(lq1_ref[...], lk1_ref[...], lq2_ref[...], lk2_ref[...], lam_init)
        gain = gcol_ref[...] * (1.0 - lam_init)
        for h in range(H_A):
            ot = acc_s[0, h] * (1.0 / l_s[0, h]) - acc_s[1, h] * (lam / l_s[1, h])
            ms = jnp.mean(ot * ot, axis=0, keepdims=True)
            o_ref[:, h * DV_A:(h + 1) * DV_A] = (ot * lax.rsqrt(ms + EPS) * gain).T.astype(BF16)


def _attn_prompt(qt, k, vt, qn, kn, lq1, lk1, lq2, lk2, gh, lam_init):
    b, _, s, _ = k.shape
    t = ATTN_TILE
    nq = s // t
    pairs = [(i, j) for i in range(nq) for j in range(i + 1)]
    qi_tab = jnp.asarray(np.array([a for a, _ in pairs], np.int32))
    kj_tab = jnp.asarray(np.array([c for _, c in pairs], np.int32))
    qspec = pl.BlockSpec((None, H_A, DV_A, t), lambda bb, p, qi, kj: (bb, 0, 0, qi[p]))
    kspec = pl.BlockSpec((None, H_A, t, DV_A), lambda bb, p, qi, kj: (bb, 0, kj[p], 0))
    vspec = pl.BlockSpec((None, H_A, DV_A, t), lambda bb, p, qi, kj: (bb, 0, 0, kj[p]))
    small = lambda shape: pl.BlockSpec(shape, lambda bb, p, qi, kj: (0, 0))
    nspec = pl.BlockSpec((None,) + qn.shape[1:], lambda bb, p, qi, kj: (bb, 0, 0))
    grid_spec = pltpu.PrefetchScalarGridSpec(
        num_scalar_prefetch=2,
        grid=(b, len(pairs)),
        in_specs=[qspec, kspec, vspec, nspec, nspec, small((1, DH_A)), small((1, DH_A)), small((1, DH_A)),
                  small((1, DH_A)), small((DV_A, 1))],
        out_specs=pl.BlockSpec((None, t, H_A * DV_A), lambda bb, p, qi, kj: (bb, qi[p], 0)),
        scratch_shapes=[pltpu.VMEM((2, H_A, DV_A, t), BF16),
                        pltpu.VMEM((2, H_A, 1, t), F32),
                        pltpu.VMEM((2, H_A, 1, t), F32),
                        pltpu.VMEM((2, H_A, DV_A, t), F32),
                        pltpu.SMEM((1,), jnp.int32)],
    )
    return pl.pallas_call(
        functools.partial(_attn_prompt_kernel, lam_init=lam_init),
        grid_spec=grid_spec,
        out_shape=jax.ShapeDtypeStruct((b, s, H_A * DV_A), BF16),
        compiler_params=pltpu.CompilerParams(
            dimension_semantics=("arbitrary", "arbitrary"), vmem_limit_bytes=V7X_VMEM_LIMIT),
        name="attn_prompt",
    )(qi_tab, kj_tab, qt, k, vt, qn, kn, lq1, lk1, lq2, lk2, gh.reshape(DV_A, 1))


def _attn_sample_kernel(q_ref, kn_ref, vn_ref, ck_ref, cv_ref, lq1_ref, lk1_ref, lq2_ref, lk2_ref,
                        gh_ref, o_ref, *, lam_init):
    lam = _lam(lq1_ref[...], lk1_ref[...], lq2_ref[...], lk2_ref[...], lam_init)
    nq = q_ref.shape[1]
    past = ck_ref.shape[0] // H_A
    zero = jnp.zeros((2 * nq, DV_A), BF16)
    for ha in range(0, H_A, 2):
        hb = ha + 1
        rows_a, rows_b = pl.ds(ha, past, stride=H_A), pl.ds(hb, past, stride=H_A)
        qa = jnp.concatenate(_split_maps(q_ref[ha], 1), axis=0)
        qb = jnp.concatenate(_split_maps(q_ref[hb], 1), axis=0)
        qz = jnp.concatenate([jnp.concatenate([qa, zero], axis=1),
                              jnp.concatenate([zero, qb], axis=1)], axis=0)
        kc = jnp.concatenate([ck_ref[rows_a, :], ck_ref[rows_b, :]], axis=1).astype(BF16)
        kn = jnp.concatenate([kn_ref[ha], kn_ref[hb]], axis=1)
        sc = lax.dot_general(qz, kc, NT_DIMS, preferred_element_type=F32)
        sn = lax.dot_general(qz, kn, NT_DIMS, preferred_element_type=F32)
        m = jnp.maximum(jnp.max(sc, axis=-1, keepdims=True), jnp.max(sn, axis=-1, keepdims=True))
        pc = jnp.exp2(sc - m)
        pn = jnp.exp2(sn - m)
        inv = 1.0 / (jnp.sum(pc, axis=-1, keepdims=True) + jnp.sum(pn, axis=-1, keepdims=True))
        pc = pc * inv
        pn = pn * inv

        def diff_maps(p):
            return jnp.concatenate([p[0:nq] - lam * p[nq:2 * nq],
                                    p[2 * nq:3 * nq] - lam * p[3 * nq:4 * nq]], axis=0)

        vc = jnp.concatenate([cv_ref[rows_a, :], cv_ref[rows_b, :]], axis=1).astype(BF16)
        vn = jnp.concatenate([vn_ref[ha], vn_ref[hb]], axis=1)
        o2 = (jnp.dot(diff_maps(pc).astype(BF16), vc, preferred_element_type=F32)
              + jnp.dot(diff_maps(pn).astype(BF16), vn, preferred_element_type=F32))
        for h, o in ((ha, o2[0:nq, 0:DV_A]), (hb, o2[nq:2 * nq, DV_A:2 * DV_A])):
            o_ref[:, h * DV_A:(h + 1) * DV_A] = (_rms(o, gh_ref[...]) * (1.0 - lam_init)).astype(BF16)


def _attn_sample(q, kn, vn, cache_k, cache_v, b_off, b, l, lq1, lk1, lq2, lk2, gh, lam_init):
    hspec = pl.BlockSpec((H_A, l, DV_A), lambda i: (0, i, 0))
    cspec = pl.BlockSpec((None,) + cache_k.shape[1:], lambda i: (b_off + i, 0, 0))
    small = lambda shape: pl.BlockSpec(shape, lambda i: (0, 0))
    return pl.pallas_call(
        functools.partial(_attn_sample_kernel, lam_init=lam_init),
        grid=(b,),
        in_specs=[hspec, hspec, hspec, cspec, cspec, small((1, DH_A)), small((1, DH_A)),
                  small((1, DH_A)), small((1, DH_A)), small((1, DV_A))],
        out_specs=pl.BlockSpec((None, l, H_A * DV_A), lambda i: (i, 0, 0)),
        out_shape=jax.ShapeDtypeStruct((b, l, H_A * DV_A), BF16),
        compiler_params=pltpu.CompilerParams(
            dimension_semantics=("arbitrary",), vmem_limit_bytes=V7X_VMEM_LIMIT),
        name="attn_sample",
    )(q, kn, vn, cache_k, cache_v, lq1, lk1, lq2, lk2, gh)


def _split3(x):
    hi = x.astype(BF16)
    r1 = x - hi.astype(F32)
    mid = r1.astype(BF16)
    lo = (r1 - mid.astype(F32)).astype(BF16)
    return hi, mid, lo


def _mlstm_kernel(q_ref, k_ref, v_ref, og_ref, gz_ref, bias_ref, gh_ref, c0_ref, n0_ref, m0_ref,
                  h_ref, c_ref, n_ref, m_ref, c_s, n_s, m_s, *, t, bt):
    c = pl.program_id(1)

    @pl.when(c == 0)
    def _load_state():
        c_s[...] = c0_ref[...]
        n_s[...] = n0_ref[...]
        m_s[...] = m0_ref[...]

    lane = lax.broadcasted_iota(jnp.int32, (t, GATE_LANES), 1)
    row = lax.broadcasted_iota(jnp.int32, (t, t), 0)
    col = lax.broadcasted_iota(jnp.int32, (t, t), 1)
    causal = col <= row
    tril = jnp.where(causal, 1.0, 0.0).astype(BF16)
    sel = jnp.where(lax.broadcasted_iota(jnp.int32, (8, GATE_LANES), 0)
                    == lax.broadcasted_iota(jnp.int32, (8, GATE_LANES), 1), 1.0, 0.0).astype(BF16)

    gz = [gz_ref[b] + bias_ref[...] for b in range(bt)]
    lf = [jnp.minimum(g, 0.0) - jnp.log1p(jnp.exp(-jnp.abs(g))) for g in gz]
    lf = [jnp.where((lane >= H_M) & (lane < 2 * H_M), x, 0.0) for x in lf]
    bcum = [sum(jnp.dot(tril, piece, preferred_element_type=F32) for piece in _split3(x)) for x in lf]
    gc = [jnp.where(lane < H_M, gz[b], bcum[b]) for b in range(bt)]
    gr = [sum(lax.dot_general(sel, piece, NT_DIMS, preferred_element_type=F32) for piece in _split3(x))
          for x in gc]

    chains = [(b, h) for b in range(bt) for h in range(H_M)]
    ids = range(len(chains))
    sls = [slice(h * DH_M, (h + 1) * DH_M) for _, h in chains]
    ig_c = [gc[b][:, h:h + 1] for b, h in chains]
    b_c = [gc[b][:, H_M + h:H_M + h + 1] for b, h in chains]
    ig_r = [gr[b][h:h + 1, :] for b, h in chains]
    b_r = [gr[b][H_M + h:H_M + h + 1, :] for b, h in chains]
    m_prev = [m_s[b, :, h:h + 1] for b, h in chains]
    c_prev = [c_s[b, h] for b, h in chains]
    n_prev = [n_s[b, h:h + 1, :] for b, h in chains]
    q = [q_ref[b, :, sls[i]] for i, (b, _) in enumerate(chains)]
    k = [k_ref[b, :, sls[i]] for i, (b, _) in enumerate(chains)]
    v = [v_ref[b, :, sls[i]] for i, (b, _) in enumerate(chains)]

    qk = [lax.dot_general(q[i], k[i], NT_DIMS, preferred_element_type=F32) for i in ids]
    qc = [lax.dot_general(q[i], c_prev[i].astype(BF16), NT_DIMS, preferred_element_type=F32)
          for i in ids]
    d = [jnp.where(causal, b_c[i] - b_r[i] + ig_r[i], -jnp.inf) for i in ids]
    inter = [b_c[i] + m_prev[i] for i in ids]
    m_t = [jnp.maximum(inter[i], jnp.max(d[i], axis=-1, keepdims=True)) for i in ids]
    w_inter = [jnp.exp(inter[i] - m_t[i]) for i in ids]
    s = [qk[i] * jnp.exp(d[i] - m_t[i]) for i in ids]
    num = [jnp.dot(s[i].astype(BF16), v[i], preferred_element_type=F32) + w_inter[i] * qc[i]
           for i in ids]
    den = [jnp.sum(s[i], axis=-1, keepdims=True)
           + w_inter[i] * jnp.sum(q[i].astype(F32) * n_prev[i], axis=-1, keepdims=True) for i in ids]
    hh = [num[i] / jnp.maximum(jnp.abs(den[i]), jnp.exp(-m_t[i])) for i in ids]
    for i, (b, h) in enumerate(chains):
        h_ref[b, :, sls[i]] = (_rms(hh[i], gh_ref[h:h + 1, :])
                               * og_ref[b, :, sls[i]].astype(F32)).astype(BF16)

    g_last = [b_c[i][t - 1:t, :] for i in ids]
    logw = [g_last[i] - b_c[i] + ig_c[i] for i in ids]
    m_new = [jnp.maximum(g_last[i] + m_prev[i], jnp.max(logw[i], axis=0, keepdims=True)) for i in ids]
    ws = [jnp.exp(logw[i] - m_new[i]) for i in ids]
    wc = [jnp.exp(g_last[i] + m_prev[i] - m_new[i]) for i in ids]
    vw = [(v[i].astype(F32) * ws[i]).astype(BF16) for i in ids]
    for i, (b, h) in enumerate(chains):
        c_s[b, h] = wc[i] * c_prev[i] + lax.dot_general(vw[i], k[i], TN_DIMS,
                                                        preferred_element_type=F32)
        n_s[b, h:h + 1, :] = wc[i] * n_prev[i] + jnp.sum(k[i].astype(F32) * ws[i], axis=0,
                                                          keepdims=True)
        m_s[b, :, h:h + 1] = m_new[i]

    @pl.when(c == pl.num_programs(1) - 1)
    def _store_state():
        c_ref[...] = c_s[...]
        n_ref[...] = n_s[...]
        m_ref[...] = m_s[...]


def _mlstm(q, k, v, og, gz, bias, gh, c0, n0, m0):
    b, l, _ = q.shape
    t = min(MLSTM_CHUNK, l)
    nc = l // t
    bt = min(b, MLSTM_STREAMS)
    seq = lambda width: pl.BlockSpec((bt, t, width), lambda i, j: (i, j, 0))
    const = lambda shape: pl.BlockSpec(shape, lambda i, j: (0,) * len(shape))
    cspec = pl.BlockSpec((bt, H_M, DH_M, DH_M), lambda i, j: (i, 0, 0, 0))
    nspec = pl.BlockSpec((bt, H_M, DH_M), lambda i, j: (i, 0, 0))
    mspec = pl.BlockSpec((bt, 1, H_M), lambda i, j: (i, 0, 0))
    return pl.pallas_call(
        functools.partial(_mlstm_kernel, t=t, bt=bt),
        grid=(b // bt, nc),
        in_specs=[seq(D_MODEL), seq(D_MODEL), seq(D_MODEL), seq(D_MODEL), seq(GATE_LANES),
                  const((1, GATE_LANES)), const((H_M, DH_M)), cspec, nspec, mspec],
        out_specs=[seq(D_MODEL), cspec, nspec, mspec],
        out_shape=[jax.ShapeDtypeStruct((b, l, D_MODEL), BF16),
                   jax.ShapeDtypeStruct((b, H_M, DH_M, DH_M), F32),
                   jax.ShapeDtypeStruct((b, H_M, DH_M), F32),
                   jax.ShapeDtypeStruct((b, 1, H_M), F32)],
        scratch_shapes=[pltpu.VMEM((bt, H_M, DH_M, DH_M), F32),
                        pltpu.VMEM((bt, H_M, DH_M), F32),
                        pltpu.VMEM((bt, 1, H_M), F32)],
        compiler_params=pltpu.CompilerParams(
            dimension_semantics=("arbitrary", "arbitrary"), vmem_limit_bytes=V7X_VMEM_LIMIT),
        name="mlstm",
    )(q, k, v, og, gz, bias, gh, c0, n0, m0)


def _gated_gelu(x, u):
    c = -2.0 * math.sqrt(2.0 / math.pi)
    return (x * u) / (1.0 + jnp.exp(x * (c + (c * 0.044715) * (x * x))))


def _ffn_kernel(x_ref, attn_ref, hg_ref, ga_ref, gm_ref, conv0_ref, wout_ref, wup_ref, wdn_ref,
                gpm_ref, gpf_ref, gpo_ref, cw_ref, cb_ref, y_ref, cs_ref, g_s,
                *, parts, l, tiles_per_batch):
    i = pl.program_id(0)

    @pl.when(i % tiles_per_batch == 0)
    def _from_state():
        g_s[:, CONV_CARRY:CONV_PAD, :] = conv0_ref[...]

    @pl.when(i % tiles_per_batch != 0)
    def _from_prev_tile():
        g_s[:, CONV_CARRY:CONV_PAD, :] = g_s[:, l + CONV_CARRY:l + CONV_PAD, :]

    def pre(rows):
        merged = ga_ref[rows, :] * attn_ref[rows, :] + gm_ref[rows, :] * hg_ref[rows, :]
        x1 = x_ref[rows, :] + _rms(jnp.dot(merged, wout_ref[...], preferred_element_type=F32),
                                   gpm_ref[...])
        return x1, _rms(x1, gpf_ref[...]).astype(BF16)

    def up(h2):
        return jnp.dot(h2, wup_ref[...], preferred_element_type=F32)

    def act(ug, bsl, r0, lh):
        nbh = bsl.stop - bsl.start
        g3 = ug[:, D_FF:].reshape(nbh, lh, D_FF)
        g_s[bsl, CONV_PAD + r0:CONV_PAD + r0 + lh, :] = g3
        gconv = cb_ref[...]
        for j in range(CONV_W):
            lo = CONV_CARRY + j + r0
            gconv = gconv + cw_ref[j:j + 1, :] * (g3 if j == CONV_W - 1 else g_s[bsl, lo:lo + lh, :])
        return _gated_gelu(gconv.reshape(nbh * lh, D_FF), ug[:, :D_FF]).astype(BF16)

    def post(x1, a, rows):
        ff = jnp.dot(a, wdn_ref[...], preferred_element_type=F32)
        y_ref[rows, :] = x1 + _rms(ff, gpo_ref[...])

    n = len(parts)
    x1, h2, ug = [None] * n, [None] * n, [None] * n
    x1[0], h2[0] = pre(parts[0][0])
    for i in range(n + 1):
        if i + 1 < n:
            x1[i + 1], h2[i + 1] = pre(parts[i + 1][0])
        if i < n:
            ug[i] = up(h2[i])
        if i >= 1:
            rows, *conv = parts[i - 1]
            post(x1[i - 1], act(ug[i - 1], *conv), rows)
    cs_ref[...] = g_s[:, l + CONV_CARRY:l + CONV_PAD, :]


def _ffn(x2d, attn, hg, ga, gm, conv0, wout, wup, wdn, gpm, gpf, gpo, cw, cb, nb_total, seq):
    n = x2d.shape[0]
    np_ = FFN_PARTS
    if seq >= FFN_ROWS:
        tm, nb, l = FFN_ROWS, 1, FFN_ROWS
        tiles_per_batch = seq // tm
        pl_ = l // np_
        parts = tuple((slice(j * pl_, (j + 1) * pl_), slice(0, 1), j * pl_, pl_) for j in range(np_))
    else:
        tm, nb, l = n, nb_total, seq
        tiles_per_batch = 1
        pb = nb // np_
        parts = tuple((slice(j * pb * l, (j + 1) * pb * l), slice(j * pb, (j + 1) * pb), 0, l)
                      for j in range(np_))
    row = pl.BlockSpec((tm, D_MODEL), lambda i: (i, 0))
    cstate = pl.BlockSpec((nb, CONV_W - 1, D_FF), lambda i: (i // tiles_per_batch, 0, 0))
    return pl.pallas_call(
        functools.partial(_ffn_kernel, parts=parts, l=l, tiles_per_batch=tiles_per_batch),
        grid=(n // tm,),
        in_specs=[row, row, row, row, row, cstate,
                  _resident((D_MODEL, D_MODEL)), _resident((D_MODEL, 2 * D_FF)),
                  _resident((D_FF, D_MODEL)), _resident((1, D_MODEL)), _resident((1, D_MODEL)),
                  _resident((1, D_MODEL)), _resident((CONV_W, D_FF)), _resident((1, D_FF))],
        out_specs=[row, cstate],
        out_shape=[jax.ShapeDtypeStruct((n, D_MODEL), F32),
                   jax.ShapeDtypeStruct((nb_total, CONV_W - 1, D_FF), F32)],
        scratch_shapes=[pltpu.VMEM((nb, l + CONV_PAD, D_FF), F32)],
        compiler_params=pltpu.CompilerParams(
            dimension_semantics=("arbitrary",), vmem_limit_bytes=V7X_VMEM_LIMIT),
        name="merge_ffn",
    )(x2d, attn, hg, ga, gm, conv0, wout, wup, wdn, gpm, gpf, gpo, cw, cb)


def _layer(x, caches, c0, n0, m0, conv0, wts, lam_init):
    (g_pre_mix, w_in, gate_bias, lq1, lk1, lq2, lk2, g_attn_head, g_mlstm_head, wout, g_post_mix,
     g_pre_ffn, wup, conv_w, conv_b, wdn, g_post_ffn) = wts
    b, l, _ = x.shape
    x2d = x.reshape(b * l, D_MODEL)
    (qa, kf, kb, vf, vb, qm, km, vm, om, ga, gm, gz, qn, kn) = _inproj(x2d, g_pre_mix, w_in, b, l)

    if caches is None:
        per_b = qn.shape[0] // b
        attn = _attn_prompt(qa, kb, vb, qn.reshape(b, per_b, GATE_LANES), kn.reshape(b, per_b, GATE_LANES),
                            lq1, lk1, lq2, lk2, g_attn_head, lam_init)
    else:
        cache_k, cache_v, b_off = caches
        attn = _attn_sample(qa, kb, vb, cache_k, cache_v, b_off, b, l,
                            lq1, lk1, lq2, lk2, g_attn_head, lam_init)

    seq3 = lambda a: a.reshape(b, l, a.shape[-1])
    hm, c1, n1, m1 = _mlstm(seq3(qm), seq3(km), seq3(vm), seq3(om), seq3(gz), gate_bias, g_mlstm_head,
                            c0, n0, m0.reshape(b, 1, H_M))

    y, conv1 = _ffn(x2d, attn.reshape(b * l, D_MODEL), hm.reshape(b * l, D_MODEL), ga, gm,
                    conv0, wout, wup, wdn, g_post_mix, g_pre_ffn, g_post_ffn, conv_w, conv_b, b, l)
    return (y.reshape(b, l, D_MODEL), kf.reshape(b, l, H_A, DV_A), vf.reshape(b, l, H_A, DV_A),
            c1, n1, m1.reshape(b, H_M), conv1)


def kernel(x_prompt, x_sample, cache_k, cache_v, state_C, state_n, state_m, state_conv, g_pre_mix, w_in, b_gates, lam_q1, lam_k1, lam_q2, lam_k2, g_attn_head, g_mlstm_head, w_out, g_post_mix, g_pre_ffn, w_up, conv_w, conv_b, w_down, g_post_ffn):
    depth = w_in.shape[0]
    bp = x_prompt.shape[0]
    yp, ys = x_prompt, x_sample
    outs_p, outs_s = [], []
    for li in range(depth):
        lam_init = 0.8 - 0.6 * math.exp(-0.3 * li)
        gate_bias = jnp.concatenate(
            [b_gates[li], jnp.zeros((GATE_LANES - 2 * H_M,), F32)]).reshape(1, GATE_LANES)
        row = lambda a: a.reshape(1, -1)
        wts = (row(g_pre_mix[li]), w_in[li].astype(BF16), gate_bias, row(lam_q1[li]), row(lam_k1[li]),
               row(lam_q2[li]), row(lam_k2[li]), row(g_attn_head[li]), g_mlstm_head[li],
               w_out[li].astype(BF16), row(g_post_mix[li]), row(g_pre_ffn[li]),
               w_up[li].astype(BF16), conv_w[li], row(conv_b[li]), w_down[li].astype(BF16),
               row(g_post_ffn[li]))
        c0 = jnp.zeros((bp, H_M, DH_M, DH_M), F32)
        n0 = jnp.zeros((bp, H_M, DH_M), F32)
        m0 = jnp.zeros((bp, H_M), F32)
        conv0 = jnp.zeros((bp, CONV_W - 1, D_FF), F32)
        yp, *sp = _layer(yp, None, c0, n0, m0, conv0, wts, lam_init)
        bs, past = cache_k.shape[1], cache_k.shape[2]
        caches = (cache_k.reshape(depth * bs, past * H_A, DV_A),
                  cache_v.reshape(depth * bs, past * H_A, DV_A), li * bs)
        ys, *ss = _layer(ys, caches, state_C[li], state_n[li], state_m[li],
                         state_conv[li], wts, lam_init)
        outs_p.append(sp)
        outs_s.append(ss)
    k_p, v_p, c_p, n_p, m_p, conv_p = [jnp.stack([o[i] for o in outs_p]) for i in range(6)]
    k_s, v_s, c_s, n_s, m_s, conv_s = [jnp.stack([o[i] for o in outs_s]) for i in range(6)]
    return (yp, ys, k_p, v_p, c_p, n_p, m_p, conv_p, k_s, v_s, c_s, n_s, m_s, conv_s)
```
